```python
import math
import jax
import jax.numpy as jnp
from jax import lax
import numpy as np

D_MODEL = 4096
BATCH = 1
SEQ = 8192
DEPTH = 1

HEAD_DIM = 128
EPS = 1e-6
DN_HEADS = 16
DN_DK = HEAD_DIM
DN_DV = HEAD_DIM
DN_CONV = 4
DN_CHUNK = 64
DN_QK = DN_HEADS * DN_DK
DN_VW = DN_HEADS * DN_DV
DN_CONV_CH = 2 * DN_QK + DN_VW
SWA_GROUPS = ((128, 1), (512, 4), (2048, 16))
SWA_HEADS_PER_GROUP = 8
SWA_BLOCK = 128
SWA_W = SWA_HEADS_PER_GROUP * HEAD_DIM
N_SWA_HEADS = len(SWA_GROUPS) * SWA_HEADS_PER_GROUP
REL_BUCKETS = 32
REL_MAX_DIST = 2048
N_EXPERTS = 64
N_EXPERT_GROUPS = 8
TOPK_GROUPS = 4
TOP_K = 8
D_EXPERT = 768
D_SHARED = 768
ROUTED_SCALE = 2.5
MOE_BLOCK = 256
IN_SPLITS = (DN_QK, DN_QK, DN_VW, DN_VW, DN_HEADS, DN_HEADS) + (SWA_W,) * (3 * len(SWA_GROUPS)) + (D_MODEL, D_MODEL)
N_IN = sum(IN_SPLITS)

kernel_name = 'hybrid_deltanet_dilated_swa_moe_block'


def rms_norm(x, w):
    xf = x.astype(jnp.float32)
    y = xf * lax.rsqrt(jnp.mean(xf * xf, axis=-1, keepdims=True) + EPS)
    return (y * w.astype(jnp.float32)).astype(x.dtype)


def l2_normalize(x):
    xf = x.astype(jnp.float32)
    return xf * lax.rsqrt(jnp.sum(xf * xf, axis=-1, keepdims=True) + EPS)


def causal_depthwise_conv(x, w):
    K, C = w.shape
    return lax.conv_general_dilated(x, w[:, None, :], window_strides=(1,), padding=[(K - 1, 0)],
                                    dimension_numbers=('NWC', 'WIO', 'NWC'), feature_group_count=C)


def t5_causal_bucket(dist):
    max_exact = REL_BUCKETS // 2
    d = jnp.maximum(dist, 0)
    log_ratio = jnp.log(jnp.maximum(d, 1).astype(jnp.float32) / max_exact) / math.log(REL_MAX_DIST / max_exact)
    large = jnp.minimum(max_exact + (log_ratio * (REL_BUCKETS - max_exact)).astype(jnp.int32), REL_BUCKETS - 1)
    return jnp.where(d < max_exact, d, large)


def gated_delta_chunked(q, k, v, g, beta):
    B, S, H, Dk = q.shape
    Dv = v.shape[-1]
    C = DN_CHUNK
    N = S // C

    def chunks(t):
        t = t.reshape((B, N, C, H) + t.shape[3:])
        return jnp.moveaxis(t, (1, 3), (0, 2))

    qc, kc, vc, bc = chunks(q), chunks(k), chunks(v), chunks(beta)
    gc = jnp.cumsum(chunks(g), axis=-1)
    causal = jnp.tril(jnp.ones((C, C), dtype=bool))
    strict = jnp.tril(jnp.ones((C, C), dtype=bool), -1)
    diff = gc[..., :, None] - gc[..., None, :]
    decay = jnp.where(causal, jnp.exp(jnp.where(causal, diff, 0.0)), 0.0)
    kb = kc * bc[..., None]
    lower = jnp.where(strict, jnp.einsum('nbhid,nbhjd->nbhij', kb, kc) * decay, 0.0)
    a_mat = lower + jnp.eye(C, dtype=lower.dtype)
    rhs = jnp.concatenate([vc * bc[..., None], kb * jnp.exp(gc)[..., None]], axis=-1)
    sol = lax.linalg.triangular_solve(a_mat, rhs, left_side=True, lower=True, unit_diagonal=True)
    u, w = sol[..., :Dv], sol[..., Dv:]
    attn = jnp.einsum('nbhid,nbhjd->nbhij', qc, kc) * decay
    q_dec = qc * jnp.exp(gc)[..., None]
    g_last = gc[..., -1]
    k_dec = kc * jnp.exp(g_last[..., None] - gc)[..., None]

    def step(state, inp):
        u_n, w_n, attn_n, q_n, k_n, gl_n = inp
        v_new = u_n - jnp.einsum('bhcd,bhde->bhce', w_n, state)
        o_n = jnp.einsum('bhcd,bhde->bhce', q_n, state) + jnp.einsum('bhij,bhje->bhie', attn_n, v_new)
        state = state * jnp.exp(gl_n)[..., None, None] + jnp.einsum('bhcd,bhce->bhde', k_n, v_new)
        return state, o_n

    s0 = jnp.zeros((B, H, Dk, Dv), jnp.float32)
    _, o = lax.scan(step, s0, (u, w, attn, q_dec, k_dec, g_last))
    return jnp.moveaxis(o, (0, 2), (1, 3)).reshape(B, S, H, Dv)


def gated_deltanet(q, k, v, z, b, a, conv_w, a_log, dt_bias, norm_w):
    B, S, _ = q.shape
    qkv = jax.nn.silu(causal_depthwise_conv(jnp.concatenate([q, k, v], axis=-1), conv_w))
    q, k, v = jnp.split(qkv, [DN_QK, 2 * DN_QK], axis=-1)
    q = l2_normalize(q.reshape(B, S, DN_HEADS, DN_DK)) * (DN_DK ** -0.5)
    k = l2_normalize(k.reshape(B, S, DN_HEADS, DN_DK))
    v = v.reshape(B, S, DN_HEADS, DN_DV).astype(jnp.float32)
    beta = jax.nn.sigmoid(b.astype(jnp.float32))
    g = -jnp.exp(a_log.astype(jnp.float32)) * jax.nn.softplus(a.astype(jnp.float32) + dt_bias.astype(jnp.float32))
    o = gated_delta_chunked(q, k, v, g, beta)
    zf = z.reshape(B, S, DN_HEADS, DN_DV).astype(jnp.float32)
    o = o * lax.rsqrt(jnp.mean(o * o, axis=-1, keepdims=True) + EPS) * norm_w.astype(jnp.float32) * jax.nn.silu(zf)
    return o.reshape(B, S, DN_VW).astype(z.dtype)


def dilated_window_group(q, k, v, bias_table, window, dilation):
    B, S, H, E = q.shape
    back = window // dilation
    blk = SWA_BLOCK
    span = dilation * blk
    s_pad = -(-S // span) * span
    L = s_pad // dilation
    nb = L // blk

    def to_blocks(t):
        t = jnp.pad(t, ((0, 0), (0, s_pad - S), (0, 0), (0, 0)))
        t = jnp.swapaxes(t.reshape(B, L, dilation, H, E), 1, 2)
        return t.reshape(B, dilation, nb, blk, H, E)

    def band(t):
        prev = jnp.pad(t, ((0, 0), (0, 0), (1, 0), (0, 0), (0, 0), (0, 0)))[:, :, :-1]
        return jnp.concatenate([prev, t], axis=3)

    def from_blocks(t):
        t = t.reshape((B, dilation, L) + t.shape[4:])
        t = jnp.swapaxes(t, 1, 2).reshape((B, s_pad) + t.shape[3:])
        return t[:, :S]

    qb = to_blocks(q)
    kk, vv = band(to_blocks(k)), band(to_blocks(v))
    qi = jnp.arange(blk, dtype=jnp.int32)[:, None]
    kj = jnp.arange(2 * blk, dtype=jnp.int32)[None, :]
    steps = qi + blk - kj
    n_idx = jnp.arange(nb, dtype=jnp.int32)[:, None, None]
    valid = (steps >= 0) & (steps <= back) & (n_idx * blk - blk + kj >= 0)
    bias = jnp.moveaxis(bias_table[t5_causal_bucket(steps * dilation)], -1, 0)
    s = jnp.einsum('brnqhe,brnkhe->brnhqk', qb, kk).astype(jnp.float32) * (E ** -0.5) \
        + bias.astype(jnp.float32)[None, None, None]
    s = jnp.where(valid[None, None, :, None], s, -jnp.inf)
    m = jnp.max(s, axis=-1, keepdims=True)
    p = jnp.exp(s - m)
    den = jnp.sum(p, axis=-1, keepdims=True)
    o = jnp.einsum('brnhqk,brnkhe->brnqhe', p / den, vv.astype(jnp.float32))
    lse = jnp.swapaxes((m + jnp.log(den))[..., 0], -1, -2)
    return from_blocks(o), from_blocks(lse)


def dilated_swa(qkv_parts, rel_bias):
    B, S, _ = qkv_parts[0].shape
    outs, lses = [], []
    for gi, (window, dilation) in enumerate(SWA_GROUPS):
        q, k, v = (t.reshape(B, S, SWA_HEADS_PER_GROUP, HEAD_DIM) for t in qkv_parts[3 * gi:3 * gi + 3])
        heads = slice(gi * SWA_HEADS_PER_GROUP, (gi + 1) * SWA_HEADS_PER_GROUP)
        o, lse = dilated_window_group(q, k, v, rel_bias[:, heads], window, dilation)
        outs.append(o)
        lses.append(lse)
    wts = jax.nn.softmax(jnp.stack(lses), axis=0)
    o = jnp.sum(wts[..., None] * jnp.stack(outs), axis=0)
    return o.reshape(B, S, SWA_W).astype(qkv_parts[0].dtype)


def swiglu(x, wg, wu, wd):
    return (jax.nn.silu(x @ wg) * (x @ wu)) @ wd


def moe_ffn(h, w_router, router_bias, w1, w3, w2, ws1, ws3, ws2):
    T, D = h.shape
    E, K, G = N_EXPERTS, TOP_K, N_EXPERT_GROUPS
    scores = jax.nn.sigmoid(h.astype(jnp.float32) @ w_router.astype(jnp.float32))
    sel = scores + router_bias.astype(jnp.float32)
    grp = lax.top_k(sel.reshape(T, G, E // G), 2)[0].sum(-1)
    _, top_g = lax.top_k(grp, TOPK_GROUPS)
    gmask = jnp.any(top_g[:, :, None] == jnp.arange(G)[None, None, :], axis=1)
    sel = jnp.where(jnp.repeat(gmask, E // G, axis=1), sel, -jnp.inf)
    _, idx = lax.top_k(sel, K)
    gate = jnp.take_along_axis(scores, idx, axis=1)
    gate = gate / jnp.sum(gate, axis=-1, keepdims=True) * ROUTED_SCALE
    A = T * K
    flat_e = idx.reshape(A).astype(jnp.int32)
    order = jnp.argsort(flat_e)
    se = flat_e[order]
    st = (order // K).astype(jnp.int32)
    sw = gate.reshape(A)[order]
    counts = jnp.bincount(flat_e, length=E)
    padded = (counts + MOE_BLOCK - 1) // MOE_BLOCK * MOE_BLOCK
    pad_end = jnp.cumsum(padded)
    pad_start = pad_end - padded
    rank = jnp.arange(A, dtype=jnp.int32) - (jnp.cumsum(counts) - counts)[se]
    dest = pad_start[se] + rank
    nb = -(-A // MOE_BLOCK) + E
    cap = nb * MOE_BLOCK
    tok = jnp.zeros((cap,), jnp.int32).at[dest].set(st)
    wgt = jnp.zeros((cap,), jnp.float32).at[dest].set(sw)
    blk_e = jnp.minimum(jnp.searchsorted(pad_end, jnp.arange(nb, dtype=jnp.int32) * MOE_BLOCK, side='right'), E - 1)

    def expert_block(acc, inp):
        t_idx, wts, e = inp
        xb = h[t_idx]
        y = swiglu(xb, w1[e], w3[e], w2[e])
        return acc.at[t_idx].add(y * wts[:, None].astype(y.dtype)), None

    routed, _ = lax.scan(expert_block, jnp.zeros_like(h),
                         (tok.reshape(nb, MOE_BLOCK), wgt.reshape(nb, MOE_BLOCK), blk_e))
    return routed + swiglu(h, ws1, ws3, ws2)


def setup_inputs(seed: int = 0) -> dict:
    key = jax.random.key(seed)
    ks = jax.random.split(key, 32)
    f32 = jnp.float32
    Ld = DEPTH

    def nrm(k, shape, scale):
        return jax.random.normal(k, shape, f32) * scale

    def gain(k, shape):
        return 1.0 + 0.05 * jax.random.normal(k, shape, f32)

    dt = jnp.exp(jax.random.uniform(ks[9], (Ld, DN_HEADS), f32, math.log(1e-3), math.log(1e-1)))
    return {
        'x': nrm(ks[0], (BATCH, SEQ, D_MODEL), 1.0),
        'c': nrm(ks[1], (BATCH, D_MODEL), 1.0),
        'w_mod': nrm(ks[2], (Ld, D_MODEL, 6 * D_MODEL), 0.5 * D_MODEL ** -0.5),
        'b_mod': nrm(ks[3], (Ld, 6 * D_MODEL), 0.02),
        'norm_pre1': gain(ks[4], (Ld, D_MODEL)),
        'norm_post1': gain(ks[5], (Ld, D_MODEL)),
        'w_in': nrm(ks[6], (Ld, D_MODEL, N_IN), D_MODEL ** -0.5),
        'conv_w': nrm(ks[7], (Ld, DN_CONV, DN_CONV_CH), DN_CONV ** -0.5),
        'a_log': jnp.log(jax.random.uniform(ks[8], (Ld, DN_HEADS), f32, 1.0, 16.0)),
        'dt_bias': dt + jnp.log(-jnp.expm1(-dt)),
        'dn_norm_w': gain(ks[10], (Ld, DN_DV)),
        'w_up_dn': nrm(ks[11], (Ld, DN_VW, D_MODEL), DN_VW ** -0.5),
        'w_up_swa': nrm(ks[12], (Ld, SWA_W, D_MODEL), SWA_W ** -0.5),
        'w_out': nrm(ks[13], (Ld, D_MODEL, D_MODEL), D_MODEL ** -0.5),
        'rel_bias': nrm(ks[14], (REL_BUCKETS, N_SWA_HEADS), 0.5),
        'norm_pre2': gain(ks[15], (Ld, D_MODEL)),
        'norm_post2': gain(ks[16], (Ld, D_MODEL)),
        'w_router': nrm(ks[17], (Ld, D_MODEL, N_EXPERTS), D_MODEL ** -0.5),
        'router_bias': nrm(ks[18], (Ld, N_EXPERTS), 0.01),
        'w1': nrm(ks[19], (Ld, N_EXPERTS, D_MODEL, D_EXPERT), D_MODEL ** -0.5),
        'w3': nrm(ks[20], (Ld, N_EXPERTS, D_MODEL, D_EXPERT), D_MODEL ** -0.5),
        'w2': nrm(ks[21], (Ld, N_EXPERTS, D_EXPERT, D_MODEL), D_EXPERT ** -0.5),
        'ws1': nrm(ks[22], (Ld, D_MODEL, D_SHARED), D_MODEL ** -0.5),
        'ws3': nrm(ks[23], (Ld, D_MODEL, D_SHARED), D_MODEL ** -0.5),
        'ws2': nrm(ks[24], (Ld, D_SHARED, D_MODEL), D_SHARED ** -0.5),
    }


def reference(x, c, w_mod, b_mod, norm_pre1, norm_post1, w_in, conv_w, a_log, dt_bias, dn_norm_w,
              w_up_dn, w_up_swa, w_out, rel_bias, norm_pre2, norm_post2, w_router, router_bias,
              w1, w3, w2, ws1, ws3, ws2):
    B, S, D = x.shape
    split_points = np.cumsum(IN_SPLITS)[:-1].tolist()
    n_swa = 3 * len(SWA_GROUPS)
    for layer in range(DEPTH):
        mod = jax.nn.silu(c) @ w_mod[layer] + b_mod[layer]
        sh1, sc1, gt1, sh2, sc2, gt2 = (m[:, None, :] for m in jnp.split(mod, 6, axis=-1))
        h = rms_norm(x, norm_pre1[layer]) * (1.0 + sc1) + sh1
        parts = jnp.split(h @ w_in[layer], split_points, axis=-1)
        q_dn, k_dn, v_dn, z_dn, b_dn, a_dn = parts[:6]
        swa_parts = parts[6:6 + n_swa]
        gate_dn, gate_swa = parts[6 + n_swa], parts[7 + n_swa]
        y_dn = gated_deltanet(q_dn, k_dn, v_dn, z_dn, b_dn, a_dn, conv_w[layer], a_log[layer],
                              dt_bias[layer], dn_norm_w[layer]) @ w_up_dn[layer]
        y_swa = dilated_swa(swa_parts, rel_bias) @ w_up_swa[layer]
        mixed = jax.nn.sigmoid(gate_dn) * y_dn + jax.nn.sigmoid(gate_swa) * y_swa
        x = x + gt1 * rms_norm(mixed @ w_out[layer], norm_post1[layer])
        h = rms_norm(x, norm_pre2[layer]) * (1.0 + sc2) + sh2
        y = moe_ffn(h.reshape(B * S, D), w_router[layer], router_bias[layer], w1[layer], w3[layer],
                    w2[layer], ws1[layer], ws3[layer], ws2[layer]).reshape(B, S, D)
        x = x + gt2 * rms_norm(y, norm_post2[layer])
    return x
```

```python
import functools
import math

import jax
import jax.numpy as jnp
import numpy as np
from jax import lax
from jax.experimental import pallas as pl
from jax.experimental.pallas import tpu as pltpu

D_MODEL = 4096
HEAD_DIM = 128
EPS = 1e-6
DN_HEADS = 16
DN_DK = HEAD_DIM
DN_DV = HEAD_DIM
DN_CONV = 4
DN_CHUNK = 64
DN_QK = DN_HEADS * DN_DK
DN_VW = DN_HEADS * DN_DV
SWA_GROUPS = ((128, 1), (512, 4), (2048, 16))
SWA_HEADS_PER_GROUP = 8
SWA_BLOCK = 128
SWA_W = SWA_HEADS_PER_GROUP * HEAD_DIM
REL_BUCKETS = 32
REL_MAX_DIST = 2048
N_EXPERTS = 64
N_EXPERT_GROUPS = 8
TOPK_GROUPS = 4
TOP_K = 8
D_EXPERT = 768
ROUTED_SCALE = 2.5
MOE_BLOCK = 256
IN_SPLITS = (DN_QK, DN_QK, DN_VW, DN_VW, DN_HEADS, DN_HEADS) + (SWA_W,) * (3 * len(SWA_GROUPS)) + (D_MODEL, D_MODEL)

VMEM_LIMIT_BYTES = 56 * 1024 * 1024

F32 = jnp.float32
BF16 = jnp.bfloat16


def _mm_kernel(a_ref, b_ref, o_ref):
    a = a_ref[...].astype(BF16)
    b = b_ref[...].astype(BF16)
    o_ref[...] = jnp.dot(a, b, preferred_element_type=F32).astype(o_ref.dtype)


def dense_matmul(a, b, *, tm, tn, out_dtype, name):
    m, k = a.shape
    n = b.shape[1]
    assert m % tm == 0
    return pl.pallas_call(
        _mm_kernel,
        grid=(m // tm, pl.cdiv(n, tn)),
        in_specs=[pl.BlockSpec((tm, k), lambda i, j: (i, 0)),
                  pl.BlockSpec((k, tn), lambda i, j: (0, j))],
        out_specs=pl.BlockSpec((tm, tn), lambda i, j: (i, j)),
        out_shape=jax.ShapeDtypeStruct((m, n), out_dtype),
        compiler_params=pltpu.CompilerParams(
            dimension_semantics=("parallel", "arbitrary"),
            vmem_limit_bytes=VMEM_LIMIT_BYTES),
        name=name,
    )(a, b)


def _gmm1_kernel(blk_e_ref, x_ref, w1_ref, w3_ref, h_ref):
    del blk_e_ref
    x = x_ref[...]
    g = jnp.dot(x, w1_ref[0].astype(BF16), preferred_element_type=F32)
    u = jnp.dot(x, w3_ref[0].astype(BF16), preferred_element_type=F32)
    h_ref[...] = (g * jax.nn.sigmoid(g) * u).astype(h_ref.dtype)


def grouped_gate_up(xs, blk_e, w1, w3, *, tf):
    rows, d = xs.shape
    nb = rows // MOE_BLOCK
    f = w1.shape[2]
    grid_spec = pltpu.PrefetchScalarGridSpec(
        num_scalar_prefetch=1,
        grid=(f // tf, nb),
        in_specs=[pl.BlockSpec((MOE_BLOCK, d), lambda j, i, e: (i, 0)),
                  pl.BlockSpec((1, d, tf), lambda j, i, e: (e[i], 0, j)),
                  pl.BlockSpec((1, d, tf), lambda j, i, e: (e[i], 0, j))],
        out_specs=pl.BlockSpec((MOE_BLOCK, tf), lambda j, i, e: (i, j)),
    )
    return pl.pallas_call(
        _gmm1_kernel,
        grid_spec=grid_spec,
        out_shape=jax.ShapeDtypeStruct((rows, f), BF16),
        compiler_params=pltpu.CompilerParams(
            dimension_semantics=("arbitrary", "arbitrary"),
            vmem_limit_bytes=VMEM_LIMIT_BYTES),
        name="moe_gate_up",
    )(blk_e, xs, w1, w3)


def _gmm2_kernel(blk_e_ref, h_ref, w2_ref, y_ref):
    del blk_e_ref
    y_ref[...] = jnp.dot(h_ref[...], w2_ref[0].astype(BF16), preferred_element_type=F32).astype(y_ref.dtype)


def grouped_down(hs, blk_e, w2, *, tn):
    rows, f = hs.shape
    nb = rows // MOE_BLOCK
    d = w2.shape[2]
    grid_spec = pltpu.PrefetchScalarGridSpec(
        num_scalar_prefetch=1,
        grid=(d // tn, nb),
        in_specs=[pl.BlockSpec((MOE_BLOCK, f), lambda j, i, e: (i, 0)),
                  pl.BlockSpec((1, f, tn), lambda j, i, e: (e[i], 0, j))],
        out_specs=pl.BlockSpec((MOE_BLOCK, tn), lambda j, i, e: (i, j)),
    )
    return pl.pallas_call(
        _gmm2_kernel,
        grid_spec=grid_spec,
        out_shape=jax.ShapeDtypeStruct((rows, d), F32),
        compiler_params=pltpu.CompilerParams(
            dimension_semantics=("arbitrary", "arbitrary"),
            vmem_limit_bytes=VMEM_LIMIT_BYTES),
        name="moe_down",
    )(blk_e, hs, w2)


def _rms_norm(x, w):
    xf = x.astype(F32)
    y = xf * lax.rsqrt(jnp.mean(xf * xf, axis=-1, keepdims=True) + EPS)
    return y * w.astype(F32)


def _l2_normalize(x):
    return x * lax.rsqrt(jnp.sum(x * x, axis=-1, keepdims=True) + EPS)


def _causal_depthwise_conv(x, w):
    K, C = w.shape
    return lax.conv_general_dilated(x, w[:, None, :], window_strides=(1,), padding=[(K - 1, 0)],
                                    dimension_numbers=('NWC', 'WIO', 'NWC'), feature_group_count=C)


def _t5_causal_bucket(dist):
    max_exact = REL_BUCKETS // 2
    d = jnp.maximum(dist, 0)
    log_ratio = jnp.log(jnp.maximum(d, 1).astype(F32) / max_exact) / math.log(REL_MAX_DIST / max_exact)
    large = jnp.minimum(max_exact + (log_ratio * (REL_BUCKETS - max_exact)).astype(jnp.int32), REL_BUCKETS - 1)
    return jnp.where(d < max_exact, d, large)


def _gated_delta_chunked(q, k, v, g, beta):
    B, S, H, Dk = q.shape
    Dv = v.shape[-1]
    C = DN_CHUNK
    N = S // C
    hi = lax.Precision.HIGHEST

    def chunks(t):
        t = t.reshape((B, N, C, H) + t.shape[3:])
        return jnp.moveaxis(t, (1, 3), (0, 2))

    qc, kc, vc, bc = chunks(q), chunks(k), chunks(v), chunks(beta)
    gc = jnp.cumsum(chunks(g), axis=-1)
    causal = jnp.tril(jnp.ones((C, C), dtype=bool))
    strict = jnp.tril(jnp.ones((C, C), dtype=bool), -1)
    diff = gc[..., :, None] - gc[..., None, :]
    decay = jnp.where(causal, jnp.exp(jnp.where(causal, diff, 0.0)), 0.0)
    kb = kc * bc[..., None]
    lower = jnp.where(strict, jnp.einsum('nbhid,nbhjd->nbhij', kb, kc, precision=hi) * decay, 0.0)
    a_mat = lower + jnp.eye(C, dtype=lower.dtype)
    rhs = jnp.concatenate([vc * bc[..., None], kb * jnp.exp(gc)[..., None]], axis=-1)
    sol = lax.linalg.triangular_solve(a_mat, rhs, left_side=True, lower=True, unit_diagonal=True)
    u, w = sol[..., :Dv], sol[..., Dv:]
    attn = jnp.einsum('nbhid,nbhjd->nbhij', qc, kc, precision=hi) * decay
    q_dec = qc * jnp.exp(gc)[..., None]
    g_last = gc[..., -1]
    k_dec = kc * jnp.exp(g_last[..., None] - gc)[..., None]

    def step(state, inp):
        u_n, w_n, attn_n, q_n, k_n, gl_n = inp
        v_new = u_n - jnp.einsum('bhcd,bhde->bhce', w_n, state, precision=hi)
        o_n = (jnp.einsum('bhcd,bhde->bhce', q_n, state, precision=hi)
               + jnp.einsum('bhij,bhje->bhie', attn_n, v_new, precision=hi))
        state = state * jnp.exp(gl_n)[..., None, None] + jnp.einsum('bhcd,bhce->bhde', k_n, v_new, precision=hi)
        return state, o_n

    s0 = jnp.zeros((B, H, Dk, Dv), F32)
    _, o = lax.scan(step, s0, (u, w, attn, q_dec, k_dec, g_last))
    return jnp.moveaxis(o, (0, 2), (1, 3)).reshape(B, S, H, Dv)


def _gated_deltanet(q, k, v, z, b, a, conv_w, a_log, dt_bias, norm_w):
    B, S, _ = q.shape
    qkv = jax.nn.silu(_causal_depthwise_conv(jnp.concatenate([q, k, v], axis=-1), conv_w))
    q, k, v = jnp.split(qkv, [DN_QK, 2 * DN_QK], axis=-1)
    q = _l2_normalize(q.reshape(B, S, DN_HEADS, DN_DK)) * (DN_DK ** -0.5)
    k = _l2_normalize(k.reshape(B, S, DN_HEADS, DN_DK))
    v = v.reshape(B, S, DN_HEADS, DN_DV)
    beta = jax.nn.sigmoid(b)
    g = -jnp.exp(a_log) * jax.nn.softplus(a + dt_bias)
    o = _gated_delta_chunked(q, k, v, g, beta)
    zf = z.reshape(B, S, DN_HEADS, DN_DV)
    o = o * lax.rsqrt(jnp.mean(o * o, axis=-1, keepdims=True) + EPS) * norm_w * jax.nn.silu(zf)
    return o.reshape(B, S, DN_VW)


def _dilated_window_group(q, k, v, bias_table, window, dilation):
    B, S, H, E = q.shape
    back = window // dilation
    blk = SWA_BLOCK
    span = dilation * blk
    s_pad = -(-S // span) * span
    L = s_pad // dilation
    nb = L // blk

    def to_blocks(t):
        t = jnp.pad(t, ((0, 0), (0, s_pad - S), (0, 0), (0, 0)))
        t = jnp.swapaxes(t.reshape(B, L, dilation, H, E), 1, 2)
        return t.reshape(B, dilation, nb, blk, H, E)

    def band(t):
        prev = jnp.pad(t, ((0, 0), (0, 0), (1, 0), (0, 0), (0, 0), (0, 0)))[:, :, :-1]
        return jnp.concatenate([prev, t], axis=3)

    def from_blocks(t):
        t = t.reshape((B, dilation, L) + t.shape[4:])
        t = jnp.swapaxes(t, 1, 2).reshape((B, s_pad) + t.shape[3:])
        return t[:, :S]

    qb = to_blocks(q)
    kk, vv = band(to_blocks(k)), band(to_blocks(v))
    qi = jnp.arange(blk, dtype=jnp.int32)[:, None]
    kj = jnp.arange(2 * blk, dtype=jnp.int32)[None, :]
    steps = qi + blk - kj
    n_idx = jnp.arange(nb, dtype=jnp.int32)[:, None, None]
    valid = (steps >= 0) & (steps <= back) & (n_idx * blk - blk + kj >= 0)
    bias = jnp.moveaxis(bias_table[_t5_causal_bucket(steps * dilation)], -1, 0)
    s = jnp.einsum('brnqhe,brnkhe->brnhqk', qb, kk).astype(F32) * (E ** -0.5) + bias.astype(F32)[None, None, None]
    s = jnp.where(valid[None, None, :, None], s, -jnp.inf)
    m = jnp.max(s, axis=-1, keepdims=True)
    p = jnp.exp(s - m)
    den = jnp.sum(p, axis=-1, keepdims=True)
    o = jnp.einsum('brnhqk,brnkhe->brnqhe', p / den, vv.astype(F32))
    lse = jnp.swapaxes((m + jnp.log(den))[..., 0], -1, -2)
    return from_blocks(o), from_blocks(lse)


def _dilated_swa(qkv_parts, rel_bias):
    B, S, _ = qkv_parts[0].shape
    outs, lses = [], []
    for gi, (window, dilation) in enumerate(SWA_GROUPS):
        q, k, v = (t.reshape(B, S, SWA_HEADS_PER_GROUP, HEAD_DIM) for t in qkv_parts[3 * gi:3 * gi + 3])
        heads = slice(gi * SWA_HEADS_PER_GROUP, (gi + 1) * SWA_HEADS_PER_GROUP)
        o, lse = _dilated_window_group(q, k, v, rel_bias[:, heads], window, dilation)
        outs.append(o)
        lses.append(lse)
    wts = jax.nn.softmax(jnp.stack(lses), axis=0)
    o = jnp.sum(wts[..., None] * jnp.stack(outs), axis=0)
    return o.reshape(B, S, SWA_W)


def _route(h, w_router, router_bias):
    T = h.shape[0]
    E, K, G = N_EXPERTS, TOP_K, N_EXPERT_GROUPS
    scores = jax.nn.sigmoid(jnp.dot(h, w_router, precision=lax.Precision.HIGHEST))
    sel = scores + router_bias
    grp = lax.top_k(sel.reshape(T, G, E // G), 2)[0].sum(-1)
    _, top_g = lax.top_k(grp, TOPK_GROUPS)
    gmask = jnp.any(top_g[:, :, None] == jnp.arange(G)[None, None, :], axis=1)
    sel = jnp.where(jnp.repeat(gmask, E // G, axis=1), sel, -jnp.inf)
    _, idx = lax.top_k(sel, K)
    gate = jnp.take_along_axis(scores, idx, axis=1)
    gate = gate / jnp.sum(gate, axis=-1, keepdims=True) * ROUTED_SCALE
    return idx, gate


def _dispatch_plan(idx, gate):
    T = idx.shape[0]
    E, K = N_EXPERTS, TOP_K
    A = T * K
    flat_e = idx.reshape(A).astype(jnp.int32)
    order = jnp.argsort(flat_e)
    se = flat_e[order]
    st = (order // K).astype(jnp.int32)
    sw = gate.reshape(A)[order]
    counts = jnp.bincount(flat_e, length=E)
    padded = (counts + MOE_BLOCK - 1) // MOE_BLOCK * MOE_BLOCK
    pad_end = jnp.cumsum(padded)
    pad_start = pad_end - padded
    rank = jnp.arange(A, dtype=jnp.int32) - (jnp.cumsum(counts) - counts)[se]
    dest = pad_start[se] + rank
    nb = -(-A // MOE_BLOCK) + E
    cap = nb * MOE_BLOCK
    tok = jnp.zeros((cap,), jnp.int32).at[dest].set(st)
    wgt = jnp.zeros((cap,), F32).at[dest].set(sw)
    blk_e = jnp.minimum(jnp.searchsorted(pad_end, jnp.arange(nb, dtype=jnp.int32) * MOE_BLOCK, side='right'),
                        E - 1).astype(jnp.int32)
    return tok, wgt, blk_e


def kernel(x, c, w_mod, b_mod, norm_pre1, norm_post1, w_in, conv_w, a_log, dt_bias, dn_norm_w, w_up_dn,
           w_up_swa, w_out, rel_bias, norm_pre2, norm_post2, w_router, router_bias, w1, w3, w2, ws1, ws3, ws2):
    B, S, D = x.shape
    T = B * S
    split_points = np.cumsum(IN_SPLITS)[:-1].tolist()
    n_swa = 3 * len(SWA_GROUPS)
    depth = w_mod.shape[0]
    for layer in range(depth):
        sc = jnp.broadcast_to(jax.nn.silu(c), (8, D))
        mod = dense_matmul(sc, w_mod[layer], tm=8, tn=1024, out_dtype=F32, name="adaln_mod")[:B] + b_mod[layer]
        sh1, sc1, gt1, sh2, sc2, gt2 = (m[:, None, :] for m in jnp.split(mod, 6, axis=-1))
        h = (_rms_norm(x, norm_pre1[layer]) * (1.0 + sc1) + sh1).astype(BF16)
        proj = dense_matmul(h.reshape(T, D), w_in[layer], tm=1024, tn=512, out_dtype=F32, name="in_proj")
        parts = jnp.split(proj.reshape(B, S, -1), split_points, axis=-1)
        q_dn, k_dn, v_dn, z_dn, b_dn, a_dn = parts[:6]
        swa_parts = parts[6:6 + n_swa]
        gate_dn, gate_swa = parts[6 + n_swa], parts[7 + n_swa]
        o_dn = _gated_deltanet(q_dn, k_dn, v_dn, z_dn, b_dn, a_dn, conv_w[layer], a_log[layer],
                               dt_bias[layer], dn_norm_w[layer])
        y_dn = dense_matmul(o_dn.reshape(T, DN_VW).astype(BF16), w_up_dn[layer], tm=1024, tn=512,
                            out_dtype=F32, name="up_dn").reshape(B, S, D)
        o_swa = _dilated_swa(swa_parts, rel_bias)
        y_swa = dense_matmul(o_swa.reshape(T, SWA_W).astype(BF16), w_up_swa[layer], tm=1024, tn=512,
                             out_dtype=F32, name="up_swa").reshape(B, S, D)
        mixed = (jax.nn.sigmoid(gate_dn) * y_dn + jax.nn.sigmoid(gate_swa) * y_swa).astype(BF16)
        o = dense_matmul(mixed.reshape(T, D), w_out[layer], tm=1024, tn=512, out_dtype=F32,
                         name="out_proj").reshape(B, S, D)
        x = x + gt1 * _rms_norm(o, norm_post1[layer])
        h2 = (_rms_norm(x, norm_pre2[layer]) * (1.0 + sc2) + sh2).reshape(T, D)
        idx, gate = _route(h2, w_router[layer], router_bias[layer])
        tok, wgt, blk_e = _dispatch_plan(idx, gate)
        h2b = h2.astype(BF16)
        xs = jnp.take(h2b, tok, axis=0)
        hs = grouped_gate_up(xs, blk_e, w1[layer], w3[layer], tf=256)
        ys = grouped_down(hs, blk_e, w2[layer], tn=1024)
        routed = jnp.zeros((T, D), F32).at[tok].add(ys * wgt[:, None])
        sg = dense_matmul(h2b, ws1[layer], tm=1024, tn=768, out_dtype=F32, name="shared_gate")
        su = dense_matmul(h2b, ws3[layer], tm=1024, tn=768, out_dtype=F32, name="shared_up")
        shared = dense_matmul((jax.nn.silu(sg) * su).astype(BF16), ws2[layer], tm=1024, tn=512,
                              out_dtype=F32, name="shared_down")
        y = (routed + shared).reshape(B, S, D)
        x = x + gt2 * _rms_norm(y, norm_post2[layer])
    return x
```

```python
import math

import jax
import jax.numpy as jnp
import numpy as np
from jax import lax
from jax.experimental import pallas as pl
from jax.experimental.pallas import tpu as pltpu

D_MODEL = 4096
HEAD_DIM = 128
EPS = 1e-6
DN_HEADS = 16
DN_CONV = 4
DN_QK = DN_HEADS * HEAD_DIM
DN_VW = DN_HEADS * HEAD_DIM
SWA_GROUPS = ((128, 1), (512, 4), (2048, 16))
SWA_HEADS_PER_GROUP = 8
SWA_BLOCK = 128
SWA_W = SWA_HEADS_PER_GROUP * HEAD_DIM
REL_BUCKETS = 32
REL_MAX_DIST = 2048
N_EXPERTS = 64
N_EXPERT_GROUPS = 8
TOPK_GROUPS = 4
TOP_K = 8
D_EXPERT = 768
ROUTED_SCALE = 2.5
MOE_BLOCK = 256

COL_BA = 4 * DN_QK
COL_SWA = COL_BA + 2 * DN_HEADS
COL_GATES = COL_SWA + 3 * len(SWA_GROUPS) * SWA_W
COL_END = COL_GATES + 2 * D_MODEL

VMEM_LIMIT_BYTES = 56 * 1024 * 1024

DN_TILE = 256
DN_HEADS_PER_STEP = 4
DN_INV_BASE = 16
CONV_PAD = 8
COMBINE_TILE = 64

F32 = jnp.float32
BF16 = jnp.bfloat16

_NT = (((1,), (1,)), ((), ()))
_TN = (((0,), (0,)), ((), ()))


def _compiler_params(semantics):
    return pltpu.CompilerParams(dimension_semantics=semantics, vmem_limit_bytes=VMEM_LIMIT_BYTES)


def _mm_kernel(a_ref, b_ref, o_ref):
    a = a_ref[...].astype(BF16)
    b = b_ref[...].astype(BF16)
    o_ref[...] = jnp.dot(a, b, preferred_element_type=F32).astype(o_ref.dtype)


def dense_matmul(a, b, *, tm, tn, out_dtype, name, col_lo=0, col_hi=None):
    m, k = a.shape
    col_hi = b.shape[1] if col_hi is None else col_hi
    n = col_hi - col_lo
    assert m % tm == 0 and col_lo % tn == 0
    off = col_lo // tn
    return pl.pallas_call(
        _mm_kernel,
        grid=(m // tm, pl.cdiv(n, tn)),
        in_specs=[pl.BlockSpec((tm, k), lambda i, j: (i, 0)),
                  pl.BlockSpec((k, tn), lambda i, j: (0, j + off))],
        out_specs=pl.BlockSpec((tm, tn), lambda i, j: (i, j)),
        out_shape=jax.ShapeDtypeStruct((m, n), out_dtype),
        compiler_params=_compiler_params(("parallel", "arbitrary")),
        name=name,
    )(a, b)


def _bdot(a, b, dims=None):
    a = a.astype(BF16)
    b = b.astype(BF16)
    if dims is None:
        return jnp.dot(a, b, preferred_element_type=F32)
    return lax.dot_general(a, b, dims, preferred_element_type=F32)


def _unit_lower_inverse(low, row, col):
    c = low.shape[0]
    s = DN_INV_BASE
    same = (row // s) == (col // s)
    ld = jnp.where(same, low, 0.0)
    x = jnp.where(row == col, 1.0, 0.0) - ld
    p = _bdot(ld, ld)
    steps = int(math.log2(s)) - 1
    for it in range(steps):
        x = x + _bdot(x, p)
        if it + 1 < steps:
            p = _bdot(p, p)
    while s < c:
        off = ((row // (2 * s)) == (col // (2 * s))) & ((row // s) != (col // s))
        b = jnp.where(off, low, 0.0)
        x = x - _bdot(_bdot(x, b), x)
        s *= 2
    return x


def _dn_kernel(q_ref, k_ref, v_ref, z_ref, cwq_ref, cwk_ref, cwv_ref, bcol_ref, gcol_ref, grow_ref, nw_ref,
               o_ref, state_ref, xe_ref):
    i = pl.program_id(1)
    c = q_ref.shape[0]
    gw = q_ref.shape[1]

    @pl.when(i == 0)
    def _():
        state_ref[...] = jnp.zeros_like(state_ref)
        xe_ref[0:CONV_PAD, :] = jnp.zeros((CONV_PAD, xe_ref.shape[1]), F32)

    xe_ref[CONV_PAD:CONV_PAD + c, 0:gw] = q_ref[...]
    xe_ref[CONV_PAD:CONV_PAD + c, gw:2 * gw] = k_ref[...]
    xe_ref[CONV_PAD:CONV_PAD + c, 2 * gw:3 * gw] = v_ref[...]

    def conv_silu(lo, w_ref):
        acc = w_ref[DN_CONV - 1:DN_CONV, :] * xe_ref[CONV_PAD:CONV_PAD + c, lo:lo + gw]
        for j in range(DN_CONV - 1):
            r0 = CONV_PAD - (DN_CONV - 1) + j
            acc = acc + w_ref[j:j + 1, :] * xe_ref[r0:r0 + c, lo:lo + gw]
        return acc * jax.nn.sigmoid(acc)

    qc = conv_silu(0, cwq_ref)
    kc = conv_silu(gw, cwk_ref)
    vc = conv_silu(2 * gw, cwv_ref)
    xe_ref[0:CONV_PAD, :] = xe_ref[c:c + CONV_PAD, :]

    row = lax.broadcasted_iota(jnp.int32, (c, c), 0)
    col = lax.broadcasted_iota(jnp.int32, (c, c), 1)
    ge = row >= col
    gt = row > col
    bcols = bcol_ref[0]
    gcols = gcol_ref[0]
    grows = grow_ref[0]
    for j in range(DN_HEADS_PER_STEP):
        hs = slice(j * HEAD_DIM, (j + 1) * HEAD_DIM)
        qh, kh, vh = qc[:, hs], kc[:, hs], vc[:, hs]
        qn = qh * (lax.rsqrt(jnp.sum(qh * qh, axis=-1, keepdims=True) + EPS) * (HEAD_DIM ** -0.5))
        kn = kh * lax.rsqrt(jnp.sum(kh * kh, axis=-1, keepdims=True) + EPS)
        bcol = bcols[:, j:j + 1]
        gcol = gcols[:, j:j + 1]
        grow = grows[j:j + 1, :]
        diff = gcol - grow
        decay = jnp.where(ge, jnp.exp(jnp.where(ge, diff, 0.0)), 0.0)
        kb = kn * bcol
        kf = kn.astype(BF16)
        low = jnp.where(gt, _bdot(kb, kf, _NT) * decay, 0.0)
        attn = _bdot(qn, kf, _NT) * decay
        tinv = _unit_lower_inverse(low, row, col)
        eg = jnp.exp(gcol)
        rhs = jnp.concatenate([vh * bcol, kb * eg], axis=1)
        uw = _bdot(tinv, rhs)
        u, w = uw[:, :HEAD_DIM], uw[:, HEAD_DIM:]
        s = state_ref[j]
        sb = s.astype(BF16)
        v_new = u - _bdot(w, sb)
        vnb = v_new.astype(BF16)
        o = _bdot(qn * eg, sb) + _bdot(attn, vnb)
        glast = grow[:, c - 1:c]
        kdec = kn * jnp.exp(glast - gcol)
        state_ref[j] = s * jnp.exp(glast) + _bdot(kdec, vnb, _TN)
        zf = z_ref[:, hs]
        on = o * lax.rsqrt(jnp.mean(o * o, axis=-1, keepdims=True) + EPS) * nw_ref[...] * (zf * jax.nn.sigmoid(zf))
        o_ref[:, hs] = on.astype(o_ref.dtype)


def _dn_gates(ba, a_log, dt_bias):
    t = ba.shape[0]
    hps = DN_HEADS_PER_STEP
    ng = DN_HEADS // hps
    beta = jax.nn.sigmoid(ba[:, :DN_HEADS])
    g = -jnp.exp(a_log) * jax.nn.softplus(ba[:, DN_HEADS:2 * DN_HEADS] + dt_bias)
    gc = jnp.cumsum(g.reshape(t // DN_TILE, DN_TILE, DN_HEADS), axis=1).reshape(t, DN_HEADS)
    bcol = beta.reshape(t, ng, hps).transpose(1, 0, 2)
    gcol = gc.reshape(t, ng, hps).transpose(1, 0, 2)
    grow = gc.reshape(t, ng, hps).transpose(1, 2, 0)
    return bcol, gcol, grow


def gated_deltanet(proj_dn, ba, conv_w, a_log, dt_bias, norm_w):
    t = proj_dn.shape[0]
    c = DN_TILE
    hps = DN_HEADS_PER_STEP
    gw = hps * HEAD_DIM
    ng = DN_HEADS // hps
    bcol, gcol, grow = _dn_gates(ba, a_log, dt_bias)
    nw = norm_w.reshape(1, HEAD_DIM).astype(F32)
    return pl.pallas_call(
        _dn_kernel,
        grid=(ng, t // c),
        in_specs=[pl.BlockSpec((c, gw), lambda g, i: (i, g)),
                  pl.BlockSpec((c, gw), lambda g, i: (i, ng + g)),
                  pl.BlockSpec((c, gw), lambda g, i: (i, 2 * ng + g)),
                  pl.BlockSpec((c, gw), lambda g, i: (i, 3 * ng + g)),
                  pl.BlockSpec((DN_CONV, gw), lambda g, i: (0, g)),
                  pl.BlockSpec((DN_CONV, gw), lambda g, i: (0, ng + g)),
                  pl.BlockSpec((DN_CONV, gw), lambda g, i: (0, 2 * ng + g)),
                  pl.BlockSpec((1, c, hps), lambda g, i: (g, i, 0)),
                  pl.BlockSpec((1, c, hps), lambda g, i: (g, i, 0)),
                  pl.BlockSpec((1, hps, c), lambda g, i: (g, 0, i)),
                  pl.BlockSpec((1, HEAD_DIM), lambda g, i: (0, 0))],
        out_specs=pl.BlockSpec((c, gw), lambda g, i: (i, g)),
        out_shape=jax.ShapeDtypeStruct((t, DN_VW), BF16),
        scratch_shapes=[pltpu.VMEM((hps, HEAD_DIM, HEAD_DIM), F32),
                        pltpu.VMEM((c + CONV_PAD, 3 * gw), F32)],
        compiler_params=_compiler_params(("parallel", "arbitrary")),
        name="gated_deltanet",
    )(proj_dn, proj_dn, proj_dn, proj_dn, conv_w, conv_w, conv_w, bcol, gcol, grow, nw)


def _swa_kernel(q_ref, kp_ref, kc_ref, vp_ref, vc_ref, bp_ref, bc_ref, o_ref, lse_ref):
    n = pl.program_id(1)
    blk = SWA_BLOCK
    qi = lax.broadcasted_iota(jnp.int32, (blk, blk), 0)
    kj = lax.broadcasted_iota(jnp.int32, (blk, blk), 1)
    valid_prev = (kj >= qi) & (n > 0)
    valid_cur = kj <= qi
    lane = lax.broadcasted_iota(jnp.int32, (blk, HEAD_DIM), 1)
    scale = HEAD_DIM ** -0.5
    lse_all = jnp.zeros((blk, HEAD_DIM), F32)
    for h in range(SWA_HEADS_PER_GROUP):
        hs = slice(h * HEAD_DIM, (h + 1) * HEAD_DIM)
        q = q_ref[:, hs]
        sp = lax.dot_general(q, kp_ref[:, hs], _NT, preferred_element_type=F32) * scale + bp_ref[h]
        sc = lax.dot_general(q, kc_ref[:, hs], _NT, preferred_element_type=F32) * scale + bc_ref[h]
        sp = jnp.where(valid_prev, sp, -jnp.inf)
        sc = jnp.where(valid_cur, sc, -jnp.inf)
        m = jnp.maximum(jnp.max(sp, axis=-1, keepdims=True), jnp.max(sc, axis=-1, keepdims=True))
        pp = jnp.exp(sp - m)
        pc = jnp.exp(sc - m)
        den = jnp.sum(pp, axis=-1, keepdims=True) + jnp.sum(pc, axis=-1, keepdims=True)
        o = (jnp.dot(pp.astype(BF16), vp_ref[:, hs], preferred_element_type=F32)
             + jnp.dot(pc.astype(BF16), vc_ref[:, hs], preferred_element_type=F32))
        o_ref[:, hs] = (o / den).astype(o_ref.dtype)
        lse_all = jnp.where(lane == h, m + jnp.log(den), lse_all)
    lse_ref[...] = lse_all


def _t5_causal_bucket(dist):
    max_exact = REL_BUCKETS // 2
    d = jnp.maximum(dist, 0)
    log_ratio = jnp.log(jnp.maximum(d, 1).astype(F32) / max_exact) / math.log(REL_MAX_DIST / max_exact)
    large = jnp.minimum(max_exact + (log_ratio * (REL_BUCKETS - max_exact)).astype(jnp.int32), REL_BUCKETS - 1)
    return jnp.where(d < max_exact, d, large)


def _swa_bias_blocks(rel_bias_g, dilation):
    blk = SWA_BLOCK
    qi = jnp.arange(blk, dtype=jnp.int32)[:, None]
    kj = jnp.arange(blk, dtype=jnp.int32)[None, :]
    bp = jnp.moveaxis(rel_bias_g[_t5_causal_bucket((qi + blk - kj) * dilation)], -1, 0)
    bc = jnp.moveaxis(rel_bias_g[_t5_causal_bucket((qi - kj) * dilation)], -1, 0)
    return bp.astype(F32), bc.astype(F32)


def swa_group(qkv, rel_bias_g, gi, dilation):
    t, width = qkv.shape
    blk = SWA_BLOCK
    length = t // dilation
    assert t % (dilation * blk) == 0
    nb = length // blk
    x = qkv.reshape(length, dilation * width)
    cpb = width // SWA_W
    bp, bc = _swa_bias_blocks(rel_bias_g, dilation)
    base = gi * 3

    def col(r, which):
        return r * cpb + base + which

    o, lse = pl.pallas_call(
        _swa_kernel,
        grid=(dilation, nb),
        in_specs=[pl.BlockSpec((blk, SWA_W), lambda r, n: (n, col(r, 0))),
                  pl.BlockSpec((blk, SWA_W), lambda r, n: (jnp.maximum(n - 1, 0), col(r, 1))),
                  pl.BlockSpec((blk, SWA_W), lambda r, n: (n, col(r, 1))),
                  pl.BlockSpec((blk, SWA_W), lambda r, n: (jnp.maximum(n - 1, 0), col(r, 2))),
                  pl.BlockSpec((blk, SWA_W), lambda r, n: (n, col(r, 2))),
                  pl.BlockSpec((SWA_HEADS_PER_GROUP, blk, blk), lambda r, n: (0, 0, 0)),
                  pl.BlockSpec((SWA_HEADS_PER_GROUP, blk, blk), lambda r, n: (0, 0, 0))],
        out_specs=[pl.BlockSpec((blk, SWA_W), lambda r, n: (n, r)),
                   pl.BlockSpec((blk, HEAD_DIM), lambda r, n: (n, r))],
        out_shape=[jax.ShapeDtypeStruct((length, dilation * SWA_W), BF16),
                   jax.ShapeDtypeStruct((length, dilation * HEAD_DIM), F32)],
        compiler_params=_compiler_params(("parallel", "arbitrary")),
        name=f"swa_d{dilation}",
    )(x, x, x, x, x, bp, bc)
    return o.reshape(t, SWA_W), lse.reshape(t, HEAD_DIM)


def dilated_swa(qkv, rel_bias):
    t = qkv.shape[0]
    outs, lses = [], []
    for gi, (_, dilation) in enumerate(SWA_GROUPS):
        heads = slice(gi * SWA_HEADS_PER_GROUP, (gi + 1) * SWA_HEADS_PER_GROUP)
        o, lse = swa_group(qkv, rel_bias[:, heads], gi, dilation)
        outs.append(o.reshape(t, SWA_HEADS_PER_GROUP, HEAD_DIM).astype(F32))
        lses.append(lse[:, :SWA_HEADS_PER_GROUP])
    wts = jax.nn.softmax(jnp.stack(lses), axis=0)
    o = jnp.sum(wts[..., None] * jnp.stack(outs), axis=0)
    return o.reshape(t, SWA_W)


def _expert_changed(blk_e_ref, i):
    prev = blk_e_ref[jnp.maximum(i - 1, 0)]
    return (i == 0) | (blk_e_ref[i] != prev)


def _gmm1_kernel(blk_e_ref, nused_ref, x_ref, w1_ref, w3_ref, h_ref, w1b_ref, w3b_ref):
    i = pl.program_id(1)

    @pl.when(i < nused_ref[0])
    def _():
        @pl.when(_expert_changed(blk_e_ref, i))
        def _():
            w1b_ref[...] = w1_ref[0].astype(BF16)
            w3b_ref[...] = w3_ref[0].astype(BF16)

        x = x_ref[...]
        g = jnp.dot(x, w1b_ref[...], preferred_element_type=F32)
        u = jnp.dot(x, w3b_ref[...], preferred_element_type=F32)
        h_ref[...] = (g * jax.nn.sigmoid(g) * u).astype(h_ref.dtype)

    @pl.when(i >= nused_ref[0])
    def _():
        h_ref[...] = jnp.zeros_like(h_ref)


def grouped_gate_up(xs, blk_e, n_used, w1, w3, *, tf):
    rows, d = xs.shape
    nb = rows // MOE_BLOCK
    f = w1.shape[2]

    def row_blk(i, nu):
        return jnp.minimum(i, nu[0] - 1)

    grid_spec = pltpu.PrefetchScalarGridSpec(
        num_scalar_prefetch=2,
        grid=(f // tf, nb),
        in_specs=[pl.BlockSpec((MOE_BLOCK, d), lambda j, i, e, nu: (row_blk(i, nu), 0)),
                  pl.BlockSpec((1, d, tf), lambda j, i, e, nu: (e[row_blk(i, nu)], 0, j)),
                  pl.BlockSpec((1, d, tf), lambda j, i, e, nu: (e[row_blk(i, nu)], 0, j))],
        out_specs=pl.BlockSpec((MOE_BLOCK, tf), lambda j, i, e, nu: (i, j)),
        scratch_shapes=[pltpu.VMEM((d, tf), BF16), pltpu.VMEM((d, tf), BF16)],
    )
    return pl.pallas_call(
        _gmm1_kernel,
        grid_spec=grid_spec,
        out_shape=jax.ShapeDtypeStruct((rows, f), BF16),
        compiler_params=_compiler_params(("arbitrary", "arbitrary")),
        name="moe_gate_up",
    )(blk_e, n_used, xs, w1, w3)


def _gmm2_kernel(blk_e_ref, nused_ref, h_ref, w2_ref, y_ref, w2b_ref):
    i = pl.program_id(0)

    @pl.when(i < nused_ref[0])
    def _():
        @pl.when(_expert_changed(blk_e_ref, i))
        def _():
            w2b_ref[...] = w2_ref[0].astype(BF16)

        y_ref[...] = jnp.dot(h_ref[...], w2b_ref[...], preferred_element_type=F32).astype(y_ref.dtype)

    @pl.when(i >= nused_ref[0])
    def _():
        y_ref[...] = jnp.zeros_like(y_ref)


def grouped_down(hs, blk_e, n_used, w2):
    rows, f = hs.shape
    nb = rows // MOE_BLOCK
    d = w2.shape[2]

    def row_blk(i, nu):
        return jnp.minimum(i, nu[0] - 1)

    grid_spec = pltpu.PrefetchScalarGridSpec(
        num_scalar_prefetch=2,
        grid=(nb,),
        in_specs=[pl.BlockSpec((MOE_BLOCK, f), lambda i, e, nu: (row_blk(i, nu), 0)),
                  pl.BlockSpec((1, f, d), lambda i, e, nu: (e[row_blk(i, nu)], 0, 0))],
        out_specs=pl.BlockSpec((MOE_BLOCK, d), lambda i, e, nu: (i, 0)),
        scratch_shapes=[pltpu.VMEM((f, d), BF16)],
    )
    return pl.pallas_call(
        _gmm2_kernel,
        grid_spec=grid_spec,
        out_shape=jax.ShapeDtypeStruct((rows, d), F32),
        compiler_params=_compiler_params(("arbitrary",)),
        name="moe_down",
    )(blk_e, n_used, hs, w2)


def _row_copy(y_hbm, gbuf, sem, slot, src_row, dst_row):
    return pltpu.make_async_copy(y_hbm.at[pl.ds(src_row, 1), :], gbuf.at[slot, pl.ds(dst_row, 1), :], sem.at[slot])


def _combine_kernel(dcur_ref, dnext_ref, y_hbm, gate_ref, shared_ref, x_ref, gt_ref, nw_ref, o_ref, gbuf, sem):
    i = pl.program_id(0)
    n = pl.num_programs(0)
    tm = x_ref.shape[0]
    slot = lax.rem(i, 2)

    def start_tile(d_ref, s):
        def body(t, carry):
            for k in range(TOP_K):
                _row_copy(y_hbm, gbuf, sem, s, d_ref[t * TOP_K + k], k * tm + t).start()
            return carry
        lax.fori_loop(0, tm, body, 0)

    @pl.when(i == 0)
    def _():
        start_tile(dcur_ref, 0)

    @pl.when(i + 1 < n)
    def _():
        start_tile(dnext_ref, 1 - slot)

    def wait_body(t, carry):
        _row_copy(y_hbm, gbuf, sem, slot, 0, 0).wait()
        return carry
    lax.fori_loop(0, tm * TOP_K, wait_body, 0)

    gate = gate_ref[...]
    acc = shared_ref[...].astype(F32)
    for k in range(TOP_K):
        acc = acc + gate[:, k:k + 1] * gbuf[slot, k * tm:(k + 1) * tm, :]
    y = acc * lax.rsqrt(jnp.mean(acc * acc, axis=-1, keepdims=True) + EPS) * nw_ref[...]
    o_ref[...] = x_ref[...] + gt_ref[...] * y


def moe_combine(ys, dest, gate, shared, x, gt, nw):
    t, d = x.shape
    tm = COMBINE_TILE
    nt = t // tm
    dflat = dest.reshape(t * TOP_K).astype(jnp.int32)
    return pl.pallas_call(
        _combine_kernel,
        grid=(nt,),
        in_specs=[pl.BlockSpec((tm * TOP_K,), lambda i: (i,), memory_space=pltpu.SMEM),
                  pl.BlockSpec((tm * TOP_K,), lambda i: (jnp.minimum(i + 1, nt - 1),), memory_space=pltpu.SMEM),
                  pl.BlockSpec(memory_space=pl.ANY),
                  pl.BlockSpec((tm, TOP_K), lambda i: (i, 0)),
                  pl.BlockSpec((tm, d), lambda i: (i, 0)),
                  pl.BlockSpec((tm, d), lambda i: (i, 0)),
                  pl.BlockSpec((1, d), lambda i: (0, 0)),
                  pl.BlockSpec((1, d), lambda i: (0, 0))],
        out_specs=pl.BlockSpec((tm, d), lambda i: (i, 0)),
        out_shape=jax.ShapeDtypeStruct((t, d), F32),
        scratch_shapes=[pltpu.VMEM((2, TOP_K * tm, d), F32), pltpu.SemaphoreType.DMA((2,))],
        compiler_params=_compiler_params(("arbitrary",)),
        name="moe_combine",
    )(dflat, dflat, ys, gate, shared, x, gt, nw)


def _rms_norm(x, w):
    xf = x.astype(F32)
    y = xf * lax.rsqrt(jnp.mean(xf * xf, axis=-1, keepdims=True) + EPS)
    return y * w.astype(F32)


def _route(h, w_router, router_bias):
    T = h.shape[0]
    E, K, G = N_EXPERTS, TOP_K, N_EXPERT_GROUPS
    scores = jax.nn.sigmoid(jnp.dot(h, w_router, precision=lax.Precision.HIGHEST))
    sel = scores + router_bias
    grp = lax.top_k(sel.reshape(T, G, E // G), 2)[0].sum(-1)
    _, top_g = lax.top_k(grp, TOPK_GROUPS)
    gmask = jnp.any(top_g[:, :, None] == jnp.arange(G)[None, None, :], axis=1)
    sel = jnp.where(jnp.repeat(gmask, E // G, axis=1), sel, -jnp.inf)
    _, idx = lax.top_k(sel, K)
    gate = jnp.take_along_axis(scores, idx, axis=1)
    gate = gate / jnp.sum(gate, axis=-1, keepdims=True) * ROUTED_SCALE
    return idx, gate


def _dispatch_plan(idx):
    T = idx.shape[0]
    E, K = N_EXPERTS, TOP_K
    A = T * K
    flat_e = idx.reshape(A).astype(jnp.int32)
    order = jnp.argsort(flat_e)
    se = flat_e[order]
    st = (order // K).astype(jnp.int32)
    counts = jnp.bincount(flat_e, length=E)
    padded = (counts + MOE_BLOCK - 1) // MOE_BLOCK * MOE_BLOCK
    pad_end = jnp.cumsum(padded)
    pad_start = pad_end - padded
    rank = jnp.arange(A, dtype=jnp.int32) - (jnp.cumsum(counts) - counts)[se]
    dest_sorted = (pad_start[se] + rank).astype(jnp.int32)
    nb = -(-A // MOE_BLOCK) + E
    cap = nb * MOE_BLOCK
    tok = jnp.zeros((cap,), jnp.int32).at[dest_sorted].set(st)
    dest = jnp.zeros((A,), jnp.int32).at[order].set(dest_sorted).reshape(T, K)
    blk_e = jnp.minimum(jnp.searchsorted(pad_end, jnp.arange(nb, dtype=jnp.int32) * MOE_BLOCK, side='right'),
                        E - 1).astype(jnp.int32)
    n_used = (pad_end[-1] // MOE_BLOCK).astype(jnp.int32).reshape(1)
    return tok, dest, blk_e, n_used


def kernel(x, c, w_mod, b_mod, norm_pre1, norm_post1, w_in, conv_w, a_log, dt_bias, dn_norm_w, w_up_dn,
           w_up_swa, w_out, rel_bias, norm_pre2, norm_post2, w_router, router_bias, w1, w3, w2, ws1, ws3, ws2):
    B, S, D = x.shape
    assert B == 1
    T = B * S
    x = x.reshape(T, D)
    depth = w_mod.shape[0]
    for layer in range(depth):
        sc = jnp.broadcast_to(jax.nn.silu(c), (8, D))
        mod = dense_matmul(sc, w_mod[layer], tm=8, tn=1024, out_dtype=F32, name="adaln_mod")[:B] + b_mod[layer]
        sh1, sc1, gt1, sh2, sc2, gt2 = jnp.split(mod, 6, axis=-1)
        h = (_rms_norm(x, norm_pre1[layer]) * (1.0 + sc1) + sh1).astype(BF16)
        wl = w_in[layer]
        proj_dn = dense_matmul(h, wl, tm=1024, tn=512, out_dtype=F32, name="in_proj_dn", col_lo=0, col_hi=COL_BA)
        ba = dense_matmul(h, wl, tm=1024, tn=128, out_dtype=F32, name="in_proj_ba", col_lo=COL_BA,
                          col_hi=COL_BA + 128)
        w_swa = wl[:, COL_SWA:COL_GATES].astype(BF16)
        w_gates = wl[:, COL_GATES:COL_END].astype(BF16)
        qkv_swa = dense_matmul(h, w_swa, tm=1024, tn=1024, out_dtype=BF16, name="in_proj_swa")
        gates = dense_matmul(h, w_gates, tm=1024, tn=1024, out_dtype=BF16, name="in_proj_gates")
        o_dn = gated_deltanet(proj_dn, ba, conv_w[layer], a_log[layer], dt_bias[layer], dn_norm_w[layer])
        y_dn = dense_matmul(o_dn, w_up_dn[layer], tm=1024, tn=512, out_dtype=F32, name="up_dn")
        o_swa = dilated_swa(qkv_swa, rel_bias).astype(BF16)
        y_swa = dense_matmul(o_swa, w_up_swa[layer], tm=1024, tn=512, out_dtype=F32, name="up_swa")
        gate_dn = gates[:, :D].astype(F32)
        gate_swa = gates[:, D:].astype(F32)
        mixed = (jax.nn.sigmoid(gate_dn) * y_dn + jax.nn.sigmoid(gate_swa) * y_swa).astype(BF16)
        o = dense_matmul(mixed, w_out[layer], tm=1024, tn=512, out_dtype=F32, name="out_proj")
        x = x + gt1 * _rms_norm(o, norm_post1[layer])
        h2 = _rms_norm(x, norm_pre2[layer]) * (1.0 + sc2) + sh2
        idx, gate = _route(h2, w_router[layer], router_bias[layer])
        tok, dest, blk_e, n_used = _dispatch_plan(idx)
        h2b = h2.astype(BF16)
        xs = jnp.take(h2b, tok, axis=0)
        hs = grouped_gate_up(xs, blk_e, n_used, w1[layer], w3[layer], tf=256)
        ys = grouped_down(hs, blk_e, n_used, w2[layer])
        sg = dense_matmul(h2b, ws1[layer], tm=1024, tn=768, out_dtype=F32, name="shared_gate")
        su = dense_matmul(h2b, ws3[layer], tm=1024, tn=768, out_dtype=F32, name="shared_up")
        shared = dense_matmul((jax.nn.silu(sg) * su).astype(BF16), ws2[layer], tm=1024, tn=512,
                              out_dtype=F32, name="shared_down")
        x = moe_combine(ys, dest, gate, shared, x, gt2, norm_post2[layer].reshape(1, D))
    return x.reshape(B, S, D)
```

```python
import math

import jax
import jax.numpy as jnp
from jax import lax
from jax.experimental import pallas as pl
from jax.experimental.pallas import tpu as pltpu

D_MODEL = 4096
HEAD_DIM = 128
EPS = 1e-6
DN_HEADS = 16
DN_CONV = 4
DN_QK = DN_HEADS * HEAD_DIM
DN_VW = DN_HEADS * HEAD_DIM
SWA_GROUPS = ((128, 1), (512, 4), (2048, 16))
SWA_HEADS_PER_GROUP = 8
SWA_BLOCK = 128
SWA_W = SWA_HEADS_PER_GROUP * HEAD_DIM
REL_BUCKETS = 32
REL_MAX_DIST = 2048
N_EXPERTS = 64
N_EXPERT_GROUPS = 8
TOPK_GROUPS = 4
TOP_K = 8
D_EXPERT = 768
ROUTED_SCALE = 2.5
MOE_BLOCK = 256

COL_BA = 4 * DN_QK
COL_SWA = COL_BA + 2 * DN_HEADS
COL_GATES = COL_SWA + 3 * len(SWA_GROUPS) * SWA_W
COL_END = COL_GATES + 2 * D_MODEL

LANES = 128
VMEM_LIMIT_BYTES = 56 * 1024 * 1024

DN_TILE = 256
DN_HEADS_PER_STEP = 4
DN_INV_BASE = 16
CONV_PAD = 8
ROUTE_TILE = 256
PLAN_TILE = 512
DISPATCH_TILE = 128
COMBINE_TILE = 64

F32 = jnp.float32
BF16 = jnp.bfloat16
U32 = jnp.uint32
I32 = jnp.int32
HI_MASK = 0xFFFF0000

_NT = (((1,), (1,)), ((), ()))
_TN = (((0,), (0,)), ((), ()))


def _compiler_params(semantics):
    return pltpu.CompilerParams(dimension_semantics=semantics, vmem_limit_bytes=VMEM_LIMIT_BYTES)


def _pack_bf16_pair(lo, hi):
    lo_bits = pltpu.bitcast(lo.astype(BF16).astype(F32), U32) >> jnp.uint32(16)
    hi_bits = pltpu.bitcast(hi.astype(BF16).astype(F32), U32) & jnp.uint32(HI_MASK)
    return lo_bits | hi_bits


def _unpack_bf16_pair(w):
    lo = pltpu.bitcast(w << jnp.uint32(16), F32)
    hi = pltpu.bitcast(w & jnp.uint32(HI_MASK), F32)
    return lo, hi


def _mm_kernel(a_ref, b_ref, o_ref):
    a = a_ref[...].astype(BF16)
    b = b_ref[...].astype(BF16)
    o_ref[...] = jnp.dot(a, b, preferred_element_type=F32).astype(o_ref.dtype)


def dense_matmul(a, b, *, tm, tn, out_dtype, name, col_lo=0, col_hi=None):
    m, k = a.shape
    col_hi = b.shape[1] if col_hi is None else col_hi
    n = col_hi - col_lo
    assert m % tm == 0 and col_lo % tn == 0
    off = col_lo // tn
    return pl.pallas_call(
        _mm_kernel,
        grid=(m // tm, pl.cdiv(n, tn)),
        in_specs=[pl.BlockSpec((tm, k), lambda i, j: (i, 0)),
                  pl.BlockSpec((k, tn), lambda i, j: (0, j + off))],
        out_specs=pl.BlockSpec((tm, tn), lambda i, j: (i, j)),
        out_shape=jax.ShapeDtypeStruct((m, n), out_dtype),
        compiler_params=_compiler_params(("parallel", "arbitrary")),
        name=name,
    )(a, b)


def _swiglu_up_kernel(a_ref, wg_ref, wu_ref, o_ref):
    a = a_ref[...]
    g = jnp.dot(a, wg_ref[...].astype(BF16), preferred_element_type=F32)
    u = jnp.dot(a, wu_ref[...].astype(BF16), preferred_element_type=F32)
    o_ref[...] = (g * jax.nn.sigmoid(g) * u).astype(o_ref.dtype)


def swiglu_up(a, wg, wu, *, tm, tn, name):
    m, k = a.shape
    n = wg.shape[1]
    assert m % tm == 0 and n % tn == 0
    return pl.pallas_call(
        _swiglu_up_kernel,
        grid=(m // tm, n // tn),
        in_specs=[pl.BlockSpec((tm, k), lambda i, j: (i, 0)),
                  pl.BlockSpec((k, tn), lambda i, j: (0, j)),
                  pl.BlockSpec((k, tn), lambda i, j: (0, j))],
        out_specs=pl.BlockSpec((tm, tn), lambda i, j: (i, j)),
        out_shape=jax.ShapeDtypeStruct((m, n), BF16),
        compiler_params=_compiler_params(("parallel", "arbitrary")),
        name=name,
    )(a, wg, wu)


def _prenorm_kernel(x_ref, nw_ref, sc_ref, sh_ref, h_ref):
    x = x_ref[...]
    y = x * lax.rsqrt(jnp.mean(x * x, axis=-1, keepdims=True) + EPS) * nw_ref[...]
    h_ref[...] = (y * (1.0 + sc_ref[...]) + sh_ref[...]).astype(h_ref.dtype)


def prenorm_modulate(x, nw, sc, sh, *, tm=512):
    t, d = x.shape
    vec = pl.BlockSpec((1, d), lambda i: (0, 0))
    return pl.pallas_call(
        _prenorm_kernel,
        grid=(t // tm,),
        in_specs=[pl.BlockSpec((tm, d), lambda i: (i, 0)), vec, vec, vec],
        out_specs=pl.BlockSpec((tm, d), lambda i: (i, 0)),
        out_shape=jax.ShapeDtypeStruct((t, d), BF16),
        compiler_params=_compiler_params(("parallel",)),
        name="prenorm_modulate",
    )(x, nw, sc, sh)


def _mm_residue_kernel(a_ref, b_ref, o_ref, res_ref):
    d = o_ref.shape[0]
    rows = o_ref.shape[1]
    res = jnp.dot(a_ref[...], b_ref[...], preferred_element_type=F32)
    if d == 1:
        o_ref[0] = res.astype(o_ref.dtype)
        return
    nslab = res.shape[1] // LANES
    for s in range(nslab):
        res_ref[s] = res[:, s * LANES:(s + 1) * LANES]
    for r in range(d):
        for s in range(nslab):
            o_ref[r, :, s * LANES:(s + 1) * LANES] = res_ref[s, pl.ds(r, rows, stride=d), :].astype(o_ref.dtype)


def project_residue_major(a, b, *, dilation, col_lo, col_hi, tm, tn, name):
    m, k = a.shape
    n = col_hi - col_lo
    assert m % tm == 0 and n % tn == 0 and col_lo % tn == 0 and tm % (dilation * 16) == 0
    off = col_lo // tn
    rows = tm // dilation
    return pl.pallas_call(
        _mm_residue_kernel,
        grid=(m // tm, n // tn),
        in_specs=[pl.BlockSpec((tm, k), lambda i, j: (i, 0)),
                  pl.BlockSpec((k, tn), lambda i, j: (0, j + off))],
        out_specs=pl.BlockSpec((dilation, rows, tn), lambda i, j: (0, i, j)),
        out_shape=jax.ShapeDtypeStruct((dilation, m // dilation, n), BF16),
        scratch_shapes=[pltpu.VMEM((tn // LANES, tm, LANES), F32)],
        compiler_params=_compiler_params(("parallel", "arbitrary")),
        name=name,
    )(a, b)


def _bdot(a, b, dims=None):
    a = a.astype(BF16)
    b = b.astype(BF16)
    if dims is None:
        return jnp.dot(a, b, preferred_element_type=F32)
    return lax.dot_general(a, b, dims, preferred_element_type=F32)


def _unit_lower_inverse(low, row, col):
    c = low.shape[0]
    s = DN_INV_BASE
    same = (row // s) == (col // s)
    ld = jnp.where(same, low, 0.0)
    x = jnp.where(row == col, 1.0, 0.0) - ld
    p = _bdot(ld, ld)
    steps = int(math.log2(s)) - 1
    for it in range(steps):
        x = x + _bdot(x, p)
        if it + 1 < steps:
            p = _bdot(p, p)
    while s < c:
        off = ((row // (2 * s)) == (col // (2 * s))) & ((row // s) != (col // s))
        b = jnp.where(off, low, 0.0)
        x = x - _bdot(_bdot(x, b), x)
        s *= 2
    return x


def _dn_kernel(q_ref, k_ref, v_ref, z_ref, cwq_ref, cwk_ref, cwv_ref, bcol_ref, gcol_ref, grow_ref, nw_ref,
               o_ref, state_ref, xe_ref):
    i = pl.program_id(1)
    c = q_ref.shape[0]
    gw = q_ref.shape[1]

    @pl.when(i == 0)
    def _():
        state_ref[...] = jnp.zeros_like(state_ref)
        xe_ref[0:CONV_PAD, :] = jnp.zeros((CONV_PAD, xe_ref.shape[1]), F32)

    xe_ref[CONV_PAD:CONV_PAD + c, 0:gw] = q_ref[...]
    xe_ref[CONV_PAD:CONV_PAD + c, gw:2 * gw] = k_ref[...]
    xe_ref[CONV_PAD:CONV_PAD + c, 2 * gw:3 * gw] = v_ref[...]

    def conv_silu(lo, w_ref):
        acc = w_ref[DN_CONV - 1:DN_CONV, :] * xe_ref[CONV_PAD:CONV_PAD + c, lo:lo + gw]
        for j in range(DN_CONV - 1):
            r0 = CONV_PAD - (DN_CONV - 1) + j
            acc = acc + w_ref[j:j + 1, :] * xe_ref[r0:r0 + c, lo:lo + gw]
        return acc * jax.nn.sigmoid(acc)

    qc = conv_silu(0, cwq_ref)
    kc = conv_silu(gw, cwk_ref)
    vc = conv_silu(2 * gw, cwv_ref)
    xe_ref[0:CONV_PAD, :] = xe_ref[c:c + CONV_PAD, :]

    row = lax.broadcasted_iota(I32, (c, c), 0)
    col = lax.broadcasted_iota(I32, (c, c), 1)
    ge = row >= col
    gt = row > col
    bcols = bcol_ref[0]
    gcols = gcol_ref[0]
    grows = grow_ref[0]
    for j in range(DN_HEADS_PER_STEP):
        hs = slice(j * HEAD_DIM, (j + 1) * HEAD_DIM)
        qh, kh, vh = qc[:, hs], kc[:, hs], vc[:, hs]
        qn = qh * (lax.rsqrt(jnp.sum(qh * qh, axis=-1, keepdims=True) + EPS) * (HEAD_DIM ** -0.5))
        kn = kh * lax.rsqrt(jnp.sum(kh * kh, axis=-1, keepdims=True) + EPS)
        bcol = bcols[:, j:j + 1]
        gcol = gcols[:, j:j + 1]
        grow = grows[j:j + 1, :]
        diff = gcol - grow
        decay = jnp.where(ge, jnp.exp(jnp.where(ge, diff, 0.0)), 0.0)
        kb = kn * bcol
        kf = kn.astype(BF16)
        low = jnp.where(gt, _bdot(kb, kf, _NT) * decay, 0.0)
        attn = _bdot(qn, kf, _NT) * decay
        tinv = _unit_lower_inverse(low, row, col)
        eg = jnp.exp(gcol)
        rhs = jnp.concatenate([vh * bcol, kb * eg], axis=1)
        uw = _bdot(tinv, rhs)
        u, w = uw[:, :HEAD_DIM], uw[:, HEAD_DIM:]
        s = state_ref[j]
        sb = s.astype(BF16)
        v_new = u - _bdot(w, sb)
        vnb = v_new.astype(BF16)
        o = _bdot(qn * eg, sb) + _bdot(attn, vnb)
        glast = grow[:, c - 1:c]
        kdec = kn * jnp.exp(glast - gcol)
        state_ref[j] = s * jnp.exp(glast) + _bdot(kdec, vnb, _TN)
        zf = z_ref[:, hs]
        on = o * lax.rsqrt(jnp.mean(o * o, axis=-1, keepdims=True) + EPS) * nw_ref[...] * (zf * jax.nn.sigmoid(zf))
        o_ref[:, hs] = on.astype(o_ref.dtype)


def _dn_gates(ba, a_log, dt_bias):
    t = ba.shape[0]
    hps = DN_HEADS_PER_STEP
    ng = DN_HEADS // hps
    beta = jax.nn.sigmoid(ba[:, :DN_HEADS])
    g = -jnp.exp(a_log) * jax.nn.softplus(ba[:, DN_HEADS:2 * DN_HEADS] + dt_bias)
    gc = jnp.cumsum(g.reshape(t // DN_TILE, DN_TILE, DN_HEADS), axis=1).reshape(t, DN_HEADS)
    bcol = beta.reshape(t, ng, hps).transpose(1, 0, 2)
    gcol = gc.reshape(t, ng, hps).transpose(1, 0, 2)
    grow = gc.reshape(t, ng, hps).transpose(1, 2, 0)
    return bcol, gcol, grow


def gated_deltanet(proj_dn, ba, conv_w, a_log, dt_bias, norm_w):
    t = proj_dn.shape[0]
    c = DN_TILE
    hps = DN_HEADS_PER_STEP
    gw = hps * HEAD_DIM
    ng = DN_HEADS // hps
    bcol, gcol, grow = _dn_gates(ba, a_log, dt_bias)
    nw = norm_w.reshape(1, HEAD_DIM).astype(F32)
    return pl.pallas_call(
        _dn_kernel,
        grid=(ng, t // c),
        in_specs=[pl.BlockSpec((c, gw), lambda g, i: (i, g)),
                  pl.BlockSpec((c, gw), lambda g, i: (i, ng + g)),
                  pl.BlockSpec((c, gw), lambda g, i: (i, 2 * ng + g)),
                  pl.BlockSpec((c, gw), lambda g, i: (i, 3 * ng + g)),
                  pl.BlockSpec((DN_CONV, gw), lambda g, i: (0, g)),
                  pl.BlockSpec((DN_CONV, gw), lambda g, i: (0, ng + g)),
                  pl.BlockSpec((DN_CONV, gw), lambda g, i: (0, 2 * ng + g)),
                  pl.BlockSpec((1, c, hps), lambda g, i: (g, i, 0)),
                  pl.BlockSpec((1, c, hps), lambda g, i: (g, i, 0)),
                  pl.BlockSpec((1, hps, c), lambda g, i: (g, 0, i)),
                  pl.BlockSpec((1, HEAD_DIM), lambda g, i: (0, 0))],
        out_specs=pl.BlockSpec((c, gw), lambda g, i: (i, g)),
        out_shape=jax.ShapeDtypeStruct((t, DN_VW), BF16),
        scratch_shapes=[pltpu.VMEM((hps, HEAD_DIM, HEAD_DIM), F32),
                        pltpu.VMEM((c + CONV_PAD, 3 * gw), F32)],
        compiler_params=_compiler_params(("parallel", "arbitrary")),
        name="gated_deltanet",
    )(proj_dn, proj_dn, proj_dn, proj_dn, conv_w, conv_w, conv_w, bcol, gcol, grow, nw)


def _swa_kernel(q_ref, kp_ref, kc_ref, vp_ref, vc_ref, bp_ref, bc_ref, o_ref, lse_ref):
    n = pl.program_id(1)
    blk = SWA_BLOCK
    qi = lax.broadcasted_iota(I32, (blk, blk), 0)
    kj = lax.broadcasted_iota(I32, (blk, blk), 1)
    valid_prev = (kj >= qi) & (n > 0)
    valid_cur = kj <= qi
    lane = lax.broadcasted_iota(I32, (blk, HEAD_DIM), 1)
    scale = HEAD_DIM ** -0.5
    lse_all = jnp.zeros((blk, HEAD_DIM), F32)
    for h in range(SWA_HEADS_PER_GROUP):
        hs = slice(h * HEAD_DIM, (h + 1) * HEAD_DIM)
        q = q_ref[:, hs]
        sp = lax.dot_general(q, kp_ref[:, hs], _NT, preferred_element_type=F32) * scale + bp_ref[h]
        sc = lax.dot_general(q, kc_ref[:, hs], _NT, preferred_element_type=F32) * scale + bc_ref[h]
        sp = jnp.where(valid_prev, sp, -jnp.inf)
        sc = jnp.where(valid_cur, sc, -jnp.inf)
        m = jnp.maximum(jnp.max(sp, axis=-1, keepdims=True), jnp.max(sc, axis=-1, keepdims=True))
        pp = jnp.exp(sp - m)
        pc = jnp.exp(sc - m)
        den = jnp.sum(pp, axis=-1, keepdims=True) + jnp.sum(pc, axis=-1, keepdims=True)
        o = (jnp.dot(pp.astype(BF16), vp_ref[:, hs], preferred_element_type=F32)
             + jnp.dot(pc.astype(BF16), vc_ref[:, hs], preferred_element_type=F32))
        o_ref[:, hs] = (o / den).astype(o_ref.dtype)
        lse_all = jnp.where(lane == h, m + jnp.log(den), lse_all)
    lse_ref[...] = lse_all


def _t5_causal_bucket(dist):
    max_exact = REL_BUCKETS // 2
    d = jnp.maximum(dist, 0)
    log_ratio = jnp.log(jnp.maximum(d, 1).astype(F32) / max_exact) / math.log(REL_MAX_DIST / max_exact)
    large = jnp.minimum(max_exact + (log_ratio * (REL_BUCKETS - max_exact)).astype(I32), REL_BUCKETS - 1)
    return jnp.where(d < max_exact, d, large)


def _swa_bias_blocks(rel_bias_g, dilation):
    blk = SWA_BLOCK
    qi = jnp.arange(blk, dtype=I32)[:, None]
    kj = jnp.arange(blk, dtype=I32)[None, :]
    bp = jnp.moveaxis(rel_bias_g[_t5_causal_bucket((qi + blk - kj) * dilation)], -1, 0)
    bc = jnp.moveaxis(rel_bias_g[_t5_causal_bucket((qi - kj) * dilation)], -1, 0)
    return bp.astype(F32), bc.astype(F32)


def swa_group(qkv, rel_bias_g, dilation):
    d, length, _ = qkv.shape
    blk = SWA_BLOCK
    nb = length // blk
    bp, bc = _swa_bias_blocks(rel_bias_g, dilation)

    def prev(n):
        return jnp.maximum(n - 1, 0)

    def blk3(f):
        return pl.BlockSpec((None, blk, SWA_W), f)

    return pl.pallas_call(
        _swa_kernel,
        grid=(d, nb),
        in_specs=[blk3(lambda r, n: (r, n, 0)),
                  blk3(lambda r, n: (r, prev(n), 1)), blk3(lambda r, n: (r, n, 1)),
                  blk3(lambda r, n: (r, prev(n), 2)), blk3(lambda r, n: (r, n, 2)),
                  pl.BlockSpec((SWA_HEADS_PER_GROUP, blk, blk), lambda r, n: (0, 0, 0)),
                  pl.BlockSpec((SWA_HEADS_PER_GROUP, blk, blk), lambda r, n: (0, 0, 0))],
        out_specs=[blk3(lambda r, n: (r, n, 0)),
                   pl.BlockSpec((None, blk, HEAD_DIM), lambda r, n: (r, n, 0))],
        out_shape=[jax.ShapeDtypeStruct((d, length, SWA_W), BF16),
                   jax.ShapeDtypeStruct((d, length, HEAD_DIM), F32)],
        compiler_params=_compiler_params(("parallel", "arbitrary")),
        name=f"swa_d{dilation}",
    )(qkv, qkv, qkv, qkv, qkv, bp, bc)


def _mix_kernel(odn_ref, o1_ref, o4_ref, o16_ref, l1_ref, l4_ref, l16_ref, gd_ref, gs_ref, wd_ref, ws_ref,
                out_ref, a_ref, oscr_ref, lscr_ref):
    j = pl.program_id(1)
    tm = odn_ref.shape[0]
    groups = ((o1_ref, l1_ref), (o4_ref, l4_ref), (o16_ref, l16_ref))

    @pl.when(j == 0)
    def _():
        for g, (o_ref, l_ref) in enumerate(groups):
            d = o_ref.shape[0]
            rows = tm // d
            for r in range(d):
                if d == 1:
                    lscr_ref[g] = l_ref[0]
                else:
                    lscr_ref[g, pl.ds(r, rows, stride=d), :] = l_ref[r]
                for s in range(SWA_HEADS_PER_GROUP):
                    blk = o_ref[r, :, s * HEAD_DIM:(s + 1) * HEAD_DIM].astype(F32)
                    if d == 1:
                        oscr_ref[g, s] = blk
                    else:
                        oscr_ref[g, s, pl.ds(r, rows, stride=d), :] = blk
        ls = [lscr_ref[g] for g in range(len(groups))]
        m = jnp.maximum(jnp.maximum(ls[0], ls[1]), ls[2])
        es = [jnp.exp(l - m) for l in ls]
        inv = 1.0 / (es[0] + es[1] + es[2])
        for s in range(SWA_HEADS_PER_GROUP):
            acc = jnp.zeros((tm, HEAD_DIM), F32)
            for g in range(len(groups)):
                acc = acc + (es[g] * inv)[:, s:s + 1] * oscr_ref[g, s]
            a_ref[:, s * HEAD_DIM:(s + 1) * HEAD_DIM] = acc.astype(a_ref.dtype)

    y_dn = jnp.dot(odn_ref[...], wd_ref[...].astype(BF16), preferred_element_type=F32)
    y_swa = jnp.dot(a_ref[...], ws_ref[...].astype(BF16), preferred_element_type=F32)
    mixed = jax.nn.sigmoid(gd_ref[...].astype(F32)) * y_dn + jax.nn.sigmoid(gs_ref[...].astype(F32)) * y_swa
    out_ref[...] = mixed.astype(out_ref.dtype)


def mix_up(o_dn, swa_os, swa_ls, gates, w_up_dn, w_up_swa, *, tm=512, tn=512):
    t = o_dn.shape[0]
    dm = w_up_dn.shape[1]
    nj = dm // tn
    in_specs = [pl.BlockSpec((tm, o_dn.shape[1]), lambda i, j: (i, 0))]
    for o in swa_os:
        d = o.shape[0]
        in_specs.append(pl.BlockSpec((d, tm // d, SWA_W), lambda i, j: (0, i, 0)))
    for l in swa_ls:
        d = l.shape[0]
        in_specs.append(pl.BlockSpec((d, tm // d, HEAD_DIM), lambda i, j: (0, i, 0)))
    in_specs += [pl.BlockSpec((tm, tn), lambda i, j: (i, j)),
                 pl.BlockSpec((tm, tn), lambda i, j: (i, nj + j)),
                 pl.BlockSpec((w_up_dn.shape[0], tn), lambda i, j: (0, j)),
                 pl.BlockSpec((w_up_swa.shape[0], tn), lambda i, j: (0, j))]
    ng = len(swa_os)
    return pl.pallas_call(
        _mix_kernel,
        grid=(t // tm, nj),
        in_specs=in_specs,
        out_specs=pl.BlockSpec((tm, tn), lambda i, j: (i, j)),
        out_shape=jax.ShapeDtypeStruct((t, dm), BF16),
        scratch_shapes=[pltpu.VMEM((tm, SWA_W), BF16),
                        pltpu.VMEM((ng, SWA_HEADS_PER_GROUP, tm, HEAD_DIM), F32),
                        pltpu.VMEM((ng, tm, HEAD_DIM), F32)],
        compiler_params=_compiler_params(("parallel", "arbitrary")),
        name="mix_up",
    )(o_dn, *swa_os, *swa_ls, gates, gates, w_up_dn, w_up_swa)


def _route_kernel(o_ref, x_ref, gt1_ref, np1_ref, np2_ref, sc2_ref, sh2_ref, wr_ref, rb_ref,
                  x1_ref, h2b_ref, h2p_ref, idx_ref, gate_ref):
    o = o_ref[...]
    x1 = x_ref[...] + gt1_ref[...] * (o * lax.rsqrt(jnp.mean(o * o, axis=-1, keepdims=True) + EPS) * np1_ref[...])
    x1_ref[...] = x1
    h2 = (x1 * lax.rsqrt(jnp.mean(x1 * x1, axis=-1, keepdims=True) + EPS) * np2_ref[...]) * (1.0 + sc2_ref[...]) \
        + sh2_ref[...]
    h2b_ref[...] = h2.astype(BF16)
    half = h2.shape[1] // 2
    h2p_ref[...] = _pack_bf16_pair(h2[:, :half], h2[:, half:])

    tm = h2.shape[0]
    e, g, w = N_EXPERTS, N_EXPERT_GROUPS, N_EXPERTS // N_EXPERT_GROUPS
    logits = lax.dot_general(wr_ref[...], h2, _NT, precision=lax.Precision.HIGHEST,
                             preferred_element_type=F32)
    scores = jax.nn.sigmoid(logits)
    sel = (scores + rb_ref[...]).reshape(g, w, tm)
    scores3 = scores.reshape(g, w, tm)
    neg = -jnp.inf
    wi = lax.broadcasted_iota(I32, (g, w, tm), 1)
    gi = lax.broadcasted_iota(I32, (g, w, tm), 0)
    m1 = jnp.max(sel, axis=1, keepdims=True)
    first = jnp.min(jnp.where(sel == m1, wi, w), axis=1, keepdims=True)
    m2 = jnp.max(jnp.where(wi == first, neg, sel), axis=1, keepdims=True)
    grp = m1 + m2
    gi1 = lax.broadcasted_iota(I32, (g, 1, tm), 0)
    chosen = jnp.zeros((g, 1, tm), F32)
    for _ in range(TOPK_GROUPS):
        m = jnp.max(grp, axis=0, keepdims=True)
        fg = jnp.min(jnp.where(grp == m, gi1, g), axis=0, keepdims=True)
        hit = gi1 == fg
        chosen = jnp.where(hit, 1.0, chosen)
        grp = jnp.where(hit, neg, grp)
    cand = jnp.where(chosen > 0.5, sel, neg)
    ei = gi * w + wi
    idx_rows, sc_rows = [], []
    for _ in range(TOP_K):
        m = jnp.max(jnp.max(cand, axis=0, keepdims=True), axis=1, keepdims=True)
        fe = jnp.min(jnp.min(jnp.where(cand == m, ei, e), axis=0, keepdims=True), axis=1, keepdims=True)
        hit = ei == fe
        sc = jnp.sum(jnp.sum(jnp.where(hit, scores3, 0.0), axis=0, keepdims=True), axis=1, keepdims=True)
        cand = jnp.where(hit, neg, cand)
        idx_rows.append(fe.reshape(1, tm))
        sc_rows.append(sc.reshape(1, tm))
    idx = jnp.concatenate(idx_rows, axis=0)
    sc = jnp.concatenate(sc_rows, axis=0)
    idx_ref[...] = idx
    gate_ref[...] = sc / jnp.sum(sc, axis=0, keepdims=True) * ROUTED_SCALE


def moe_route(o, x, gt1, np1, np2, sc2, sh2, w_router, router_bias):
    t, d = x.shape
    tm = ROUTE_TILE
    wr_t = w_router.T.astype(F32)
    rb = router_bias.reshape(N_EXPERTS, 1).astype(F32)

    def row(i):
        return (i, 0)

    def fixed(i):
        return (0, 0)

    vec = pl.BlockSpec((1, d), fixed)
    return pl.pallas_call(
        _route_kernel,
        grid=(t // tm,),
        in_specs=[pl.BlockSpec((tm, d), row), pl.BlockSpec((tm, d), row), vec, vec, vec, vec, vec,
                  pl.BlockSpec((N_EXPERTS, d), fixed), pl.BlockSpec((N_EXPERTS, 1), fixed)],
        out_specs=[pl.BlockSpec((tm, d), row), pl.BlockSpec((tm, d), row), pl.BlockSpec((tm, d // 2), row),
                   pl.BlockSpec((TOP_K, tm), lambda i: (0, i)), pl.BlockSpec((TOP_K, tm), lambda i: (0, i))],
        out_shape=[jax.ShapeDtypeStruct((t, d), F32), jax.ShapeDtypeStruct((t, d), BF16),
                   jax.ShapeDtypeStruct((t, d // 2), U32),
                   jax.ShapeDtypeStruct((TOP_K, t), I32), jax.ShapeDtypeStruct((TOP_K, t), F32)],
        compiler_params=_compiler_params(("parallel",)),
        name="moe_route",
    )(o, x, gt1, np1, np2, sc2, sh2, wr_t, rb)


def _plan_kernel(idx_ref, dest_ref, blk_e_ref, pad_end_ref, nused_ref, counts_ref, carry_ref, pstart_ref):
    p = pl.program_id(0)
    i = pl.program_id(1)
    e = N_EXPERTS
    tn = idx_ref.shape[1]
    idx = idx_ref[...]
    ei = lax.broadcasted_iota(I32, (e, tn), 0)
    member = jnp.zeros((e, tn), F32)
    for k in range(TOP_K):
        member = member + jnp.where(idx[k:k + 1, :] == ei, 1.0, 0.0)
    tile_counts = jnp.sum(member, axis=1, keepdims=True)

    @pl.when((p == 0) & (i == 0))
    def _():
        counts_ref[...] = jnp.zeros_like(counts_ref)

    @pl.when(p == 0)
    def _():
        counts_ref[...] += tile_counts

    @pl.when((p == 1) & (i == 0))
    def _():
        counts = counts_ref[...]
        padded = jnp.ceil(counts * (1.0 / MOE_BLOCK)) * MOE_BLOCK
        sub = lax.broadcasted_iota(I32, (e, e), 0)
        lan = lax.broadcasted_iota(I32, (e, e), 1)
        start_row = jnp.sum(jnp.where(sub < lan, padded, 0.0), axis=0, keepdims=True)
        end_row = jnp.sum(jnp.where(sub <= lan, padded, 0.0), axis=0, keepdims=True)
        pstart_ref[...] = jnp.sum(jnp.where(sub == lan, start_row, 0.0), axis=1, keepdims=True)
        carry_ref[...] = jnp.zeros_like(carry_ref)
        pad_end_ref[...] = end_row.astype(I32)
        nb = blk_e_ref.shape[0]
        bstart = (lax.broadcasted_iota(I32, (nb, e), 0) * MOE_BLOCK).astype(F32)
        below = jnp.sum(jnp.where(end_row <= bstart, 1.0, 0.0), axis=1, keepdims=True)
        blk_e_ref[...] = jnp.minimum(below, e - 1.0).astype(I32)
        nused_ref[...] = (end_row[:, e - 1:e] * (1.0 / MOE_BLOCK)).astype(I32)

    @pl.when(p == 1)
    def _():
        tt = lax.broadcasted_iota(I32, (tn, tn), 0)
        tc = lax.broadcasted_iota(I32, (tn, tn), 1)
        before = jnp.where(tt < tc, 1.0, 0.0).astype(BF16)
        prefix = jnp.dot(member.astype(BF16), before, preferred_element_type=F32)
        base = prefix + carry_ref[...] + pstart_ref[...]
        rows = []
        for k in range(TOP_K):
            rows.append(jnp.sum(jnp.where(idx[k:k + 1, :] == ei, base, 0.0), axis=0, keepdims=True))
        dest_ref[...] = jnp.concatenate(rows, axis=0).astype(I32)
        carry_ref[...] += tile_counts


def moe_plan(idx, nb):
    k, t = idx.shape
    tn = PLAN_TILE
    e = N_EXPERTS
    return pl.pallas_call(
        _plan_kernel,
        grid=(2, t // tn),
        in_specs=[pl.BlockSpec((k, tn), lambda p, i: (0, i))],
        out_specs=[pl.BlockSpec((k, tn), lambda p, i: (0, i * p)),
                   pl.BlockSpec((nb, 1), lambda p, i: (0, 0)),
                   pl.BlockSpec((1, e), lambda p, i: (0, 0)),
                   pl.BlockSpec((1, 1), lambda p, i: (0, 0))],
        out_shape=[jax.ShapeDtypeStruct((k, t), I32), jax.ShapeDtypeStruct((nb, 1), I32),
                   jax.ShapeDtypeStruct((1, e), I32), jax.ShapeDtypeStruct((1, 1), I32)],
        scratch_shapes=[pltpu.VMEM((e, 1), F32), pltpu.VMEM((e, 1), F32), pltpu.VMEM((e, 1), F32)],
        compiler_params=_compiler_params(("arbitrary", "arbitrary")),
        name="moe_plan",
    )(idx)


def _dispatch_kernel(dest_ref, pend_ref, h_hbm, xs_hbm, zero_ref, sem):
    i = pl.program_id(0)
    tm = dest_ref.shape[0] // TOP_K

    def pad_copy(e):
        start = pl.multiple_of(jnp.maximum(pend_ref[e] - MOE_BLOCK, 0), MOE_BLOCK)
        return pltpu.make_async_copy(zero_ref, xs_hbm.at[pl.ds(start, MOE_BLOCK), :], sem.at[1])

    @pl.when(i == 0)
    def _():
        zero_ref[...] = jnp.zeros_like(zero_ref)

        def zstart(e, c):
            pad_copy(e).start()
            return c
        lax.fori_loop(0, N_EXPERTS, zstart, 0)

        def zwait(e, c):
            pad_copy(e).wait()
            return c
        lax.fori_loop(0, N_EXPERTS, zwait, 0)

        first_unused = lax.shift_right_logical(pend_ref[N_EXPERTS - 1], int(math.log2(MOE_BLOCK)))
        n_blocks = xs_hbm.shape[0] // MOE_BLOCK

        def tail_copy(b):
            start = pl.multiple_of(b * MOE_BLOCK, MOE_BLOCK)
            return pltpu.make_async_copy(zero_ref, xs_hbm.at[pl.ds(start, MOE_BLOCK), :], sem.at[1])

        def tstart(b, c):
            tail_copy(b).start()
            return c
        lax.fori_loop(first_unused, n_blocks, tstart, 0)

        def twait(b, c):
            tail_copy(b).wait()
            return c
        lax.fori_loop(first_unused, n_blocks, twait, 0)

    def row_copy(t, k):
        return pltpu.make_async_copy(h_hbm.at[pl.ds(i * tm + t, 1), :],
                                     xs_hbm.at[pl.ds(dest_ref[t * TOP_K + k], 1), :], sem.at[0])

    def start(t, c):
        for k in range(TOP_K):
            row_copy(t, k).start()
        return c
    lax.fori_loop(0, tm, start, 0)

    def wait(t, c):
        for k in range(TOP_K):
            row_copy(t, k).wait()
        return c
    lax.fori_loop(0, tm, wait, 0)


def moe_dispatch(h_packed, dest_tk, pad_end, rows):
    t, w = h_packed.shape
    tm = DISPATCH_TILE
    dflat = dest_tk.reshape(t * TOP_K)
    return pl.pallas_call(
        _dispatch_kernel,
        grid=(t // tm,),
        in_specs=[pl.BlockSpec((tm * TOP_K,), lambda i: (i,), memory_space=pltpu.SMEM),
                  pl.BlockSpec((N_EXPERTS,), lambda i: (0,), memory_space=pltpu.SMEM),
                  pl.BlockSpec(memory_space=pl.ANY)],
        out_specs=pl.BlockSpec(memory_space=pl.ANY),
        out_shape=jax.ShapeDtypeStruct((rows, w), U32),
        scratch_shapes=[pltpu.VMEM((MOE_BLOCK, w), U32), pltpu.SemaphoreType.DMA((2,))],
        compiler_params=_compiler_params(("arbitrary",)),
        name="moe_dispatch",
    )(dflat, pad_end.reshape(N_EXPERTS), h_packed)


def _expert_changed(blk_e_ref, i):
    prev = blk_e_ref[jnp.maximum(i - 1, 0)]
    return (i == 0) | (blk_e_ref[i] != prev)


def _gmm1_kernel(blk_e_ref, nused_ref, x_ref, w1_ref, w3_ref, h_ref, w1b_ref, w3b_ref):
    i = pl.program_id(1)

    @pl.when(i < nused_ref[0])
    def _():
        @pl.when(_expert_changed(blk_e_ref, i))
        def _():
            w1b_ref[...] = w1_ref[0].astype(BF16)
            w3b_ref[...] = w3_ref[0].astype(BF16)

        lo, hi = _unpack_bf16_pair(x_ref[...])
        lo = lo.astype(BF16)
        hi = hi.astype(BF16)
        half = lo.shape[1]
        g = (jnp.dot(lo, w1b_ref[:half, :], preferred_element_type=F32)
             + jnp.dot(hi, w1b_ref[half:, :], preferred_element_type=F32))
        u = (jnp.dot(lo, w3b_ref[:half, :], preferred_element_type=F32)
             + jnp.dot(hi, w3b_ref[half:, :], preferred_element_type=F32))
        h_ref[...] = (g * jax.nn.sigmoid(g) * u).astype(h_ref.dtype)

    @pl.when(i >= nused_ref[0])
    def _():
        h_ref[...] = jnp.zeros_like(h_ref)


def grouped_gate_up(xs, blk_e, n_used, w1, w3, *, tf):
    rows, half = xs.shape
    d = 2 * half
    nb = rows // MOE_BLOCK
    f = w1.shape[2]

    def row_blk(i, nu):
        return jnp.minimum(i, nu[0] - 1)

    grid_spec = pltpu.PrefetchScalarGridSpec(
        num_scalar_prefetch=2,
        grid=(f // tf, nb),
        in_specs=[pl.BlockSpec((MOE_BLOCK, half), lambda j, i, e, nu: (row_blk(i, nu), 0)),
                  pl.BlockSpec((1, d, tf), lambda j, i, e, nu: (e[row_blk(i, nu)], 0, j)),
                  pl.BlockSpec((1, d, tf), lambda j, i, e, nu: (e[row_blk(i, nu)], 0, j))],
        out_specs=pl.BlockSpec((MOE_BLOCK, tf), lambda j, i, e, nu: (i, j)),
        scratch_shapes=[pltpu.VMEM((d, tf), BF16), pltpu.VMEM((d, tf), BF16)],
    )
    return pl.pallas_call(
        _gmm1_kernel,
        grid_spec=grid_spec,
        out_shape=jax.ShapeDtypeStruct((rows, f), BF16),
        compiler_params=_compiler_params(("arbitrary", "arbitrary")),
        name="moe_gate_up",
    )(blk_e, n_used, xs, w1, w3)


def _gmm2_kernel(blk_e_ref, nused_ref, h_ref, w2_ref, y_ref, w2b_ref):
    i = pl.program_id(0)

    @pl.when(i < nused_ref[0])
    def _():
        @pl.when(_expert_changed(blk_e_ref, i))
        def _():
            w2b_ref[...] = w2_ref[0].astype(BF16)

        y = jnp.dot(h_ref[...], w2b_ref[...], preferred_element_type=F32)
        half = y.shape[1] // 2
        y_ref[...] = _pack_bf16_pair(y[:, :half], y[:, half:])

    @pl.when(i >= nused_ref[0])
    def _():
        y_ref[...] = jnp.zeros_like(y_ref)


def grouped_down(hs, blk_e, n_used, w2):
    rows, f = hs.shape
    nb = rows // MOE_BLOCK
    d = w2.shape[2]

    def row_blk(i, nu):
        return jnp.minimum(i, nu[0] - 1)

    grid_spec = pltpu.PrefetchScalarGridSpec(
        num_scalar_prefetch=2,
        grid=(nb,),
        in_specs=[pl.BlockSpec((MOE_BLOCK, f), lambda i, e, nu: (row_blk(i, nu), 0)),
                  pl.BlockSpec((1, f, d), lambda i, e, nu: (e[row_blk(i, nu)], 0, 0))],
        out_specs=pl.BlockSpec((MOE_BLOCK, d // 2), lambda i, e, nu: (i, 0)),
        scratch_shapes=[pltpu.VMEM((f, d), BF16)],
    )
    return pl.pallas_call(
        _gmm2_kernel,
        grid_spec=grid_spec,
        out_shape=jax.ShapeDtypeStruct((rows, d // 2), U32),
        compiler_params=_compiler_params(("arbitrary",)),
        name="moe_down",
    )(blk_e, n_used, hs, w2)


def _combine_kernel(dcur_ref, dnext_ref, y_hbm, gate_ref, shared_ref, x_ref, gt_ref, nw_ref, o_ref, gbuf, sem):
    i = pl.program_id(0)
    n = pl.num_programs(0)
    tm = x_ref.shape[0]
    slot = lax.rem(i, 2)

    def row_copy(s, src_row, dst_row):
        return pltpu.make_async_copy(y_hbm.at[pl.ds(src_row, 1), :], gbuf.at[s, pl.ds(dst_row, 1), :], sem.at[s])

    def start_tile(d_ref, s):
        def body(t, carry):
            for k in range(TOP_K):
                row_copy(s, d_ref[t * TOP_K + k], k * tm + t).start()
            return carry
        lax.fori_loop(0, tm, body, 0)

    @pl.when(i == 0)
    def _():
        start_tile(dcur_ref, 0)

    @pl.when(i + 1 < n)
    def _():
        start_tile(dnext_ref, 1 - slot)

    def wait_body(t, carry):
        for k in range(TOP_K):
            row_copy(slot, 0, 0).wait()
        return carry
    lax.fori_loop(0, tm, wait_body, 0)

    gate = gate_ref[...]
    half = gbuf.shape[2]
    acc_lo = shared_ref[:, :half].astype(F32)
    acc_hi = shared_ref[:, half:].astype(F32)
    for k in range(TOP_K):
        lo, hi = _unpack_bf16_pair(gbuf[slot, k * tm:(k + 1) * tm, :])
        gk = gate[:, k:k + 1]
        acc_lo = acc_lo + gk * lo
        acc_hi = acc_hi + gk * hi
    ms = (jnp.sum(acc_lo * acc_lo, axis=-1, keepdims=True) + jnp.sum(acc_hi * acc_hi, axis=-1, keepdims=True)) \
        * (1.0 / (2 * half))
    r = lax.rsqrt(ms + EPS)
    o_ref[:, :half] = x_ref[:, :half] + gt_ref[:, :half] * (acc_lo * r * nw_ref[:, :half])
    o_ref[:, half:] = x_ref[:, half:] + gt_ref[:, half:] * (acc_hi * r * nw_ref[:, half:])


def moe_combine(ys, dest_tk, gate_tk, shared, x, gt, nw):
    t, d = x.shape
    tm = COMBINE_TILE
    nt = t // tm
    dflat = dest_tk.reshape(t * TOP_K)
    return pl.pallas_call(
        _combine_kernel,
        grid=(nt,),
        in_specs=[pl.BlockSpec((tm * TOP_K,), lambda i: (i,), memory_space=pltpu.SMEM),
                  pl.BlockSpec((tm * TOP_K,), lambda i: (jnp.minimum(i + 1, nt - 1),), memory_space=pltpu.SMEM),
                  pl.BlockSpec(memory_space=pl.ANY),
                  pl.BlockSpec((tm, TOP_K), lambda i: (i, 0)),
                  pl.BlockSpec((tm, d), lambda i: (i, 0)),
                  pl.BlockSpec((tm, d), lambda i: (i, 0)),
                  pl.BlockSpec((1, d), lambda i: (0, 0)),
                  pl.BlockSpec((1, d), lambda i: (0, 0))],
        out_specs=pl.BlockSpec((tm, d), lambda i: (i, 0)),
        out_shape=jax.ShapeDtypeStruct((t, d), F32),
        scratch_shapes=[pltpu.VMEM((2, TOP_K * tm, d // 2), U32), pltpu.SemaphoreType.DMA((2,))],
        compiler_params=_compiler_params(("arbitrary",)),
        name="moe_combine",
    )(dflat, dflat, ys, gate_tk, shared, x, gt, nw)


def kernel(x, c, w_mod, b_mod, norm_pre1, norm_post1, w_in, conv_w, a_log, dt_bias, dn_norm_w, w_up_dn,
           w_up_swa, w_out, rel_bias, norm_pre2, norm_post2, w_router, router_bias, w1, w3, w2, ws1, ws3, ws2):
    B, S, D = x.shape
    assert B == 1
    T = B * S
    x = x.reshape(T, D)
    depth = w_mod.shape[0]
    n_swa_cols = 3 * SWA_W
    nb = -(-(T * TOP_K) // MOE_BLOCK) + N_EXPERTS
    for layer in range(depth):
        sc = jnp.broadcast_to(jax.nn.silu(c), (8, D))
        mod = dense_matmul(sc, w_mod[layer], tm=8, tn=1024, out_dtype=F32, name="adaln_mod")[:B] + b_mod[layer]
        sh1, sc1, gt1, sh2, sc2, gt2 = jnp.split(mod, 6, axis=-1)

        def vec(v):
            return v[layer].reshape(1, D)

        h = prenorm_modulate(x, vec(norm_pre1), sc1, sh1)
        wl = w_in[layer]
        proj_dn = dense_matmul(h, wl, tm=1024, tn=512, out_dtype=F32, name="in_proj_dn", col_lo=0, col_hi=COL_BA)
        ba = dense_matmul(h, wl, tm=1024, tn=128, out_dtype=F32, name="in_proj_ba", col_lo=COL_BA,
                          col_hi=COL_BA + 128)
        w_swa = wl[:, COL_SWA:COL_GATES].astype(BF16)
        w_gates = wl[:, COL_GATES:COL_END].astype(BF16)
        gates = dense_matmul(h, w_gates, tm=1024, tn=1024, out_dtype=BF16, name="in_proj_gates")
        o_dn = gated_deltanet(proj_dn, ba, conv_w[layer], a_log[layer], dt_bias[layer], dn_norm_w[layer])
        swa_os, swa_ls = [], []
        for gi, (_, dilation) in enumerate(SWA_GROUPS):
            qkv = project_residue_major(h, w_swa, dilation=dilation, col_lo=gi * n_swa_cols,
                                        col_hi=(gi + 1) * n_swa_cols, tm=1024, tn=1024,
                                        name=f"in_proj_swa_d{dilation}")
            heads = slice(gi * SWA_HEADS_PER_GROUP, (gi + 1) * SWA_HEADS_PER_GROUP)
            o_g, lse_g = swa_group(qkv, rel_bias[:, heads], dilation)
            swa_os.append(o_g)
            swa_ls.append(lse_g)
        mixed = mix_up(o_dn, swa_os, swa_ls, gates, w_up_dn[layer], w_up_swa[layer])
        o = dense_matmul(mixed, w_out[layer], tm=1024, tn=512, out_dtype=F32, name="out_proj")
        x1, h2b, h2p, idx, gate = moe_route(o, x, gt1, vec(norm_post1), vec(norm_pre2), sc2, sh2,
                                            w_router[layer], router_bias[layer])
        dest, blk_e, pad_end, n_used = moe_plan(idx, nb)
        dest_tk = dest.T
        gate_tk = gate.T
        xs = moe_dispatch(h2p, dest_tk, pad_end, nb * MOE_BLOCK)
        blk_e = blk_e.reshape(nb)
        n_used = n_used.reshape(1)
        hs = grouped_gate_up(xs, blk_e, n_used, w1[layer], w3[layer], tf=D_EXPERT // 3)
        ys = grouped_down(hs, blk_e, n_used, w2[layer])
        hsh = swiglu_up(h2b, ws1[layer], ws3[layer], tm=1024, tn=D_EXPERT // 2, name="shared_gate_up")
        shared = dense_matmul(hsh, ws2[layer], tm=1024, tn=512, out_dtype=F32, name="shared_down")
        x = moe_combine(ys, dest_tk, gate_tk, shared, x1, gt2, vec(norm_post2))
    return x.reshape(B, S, D)
```

```python
import functools
import math

import jax
import jax.numpy as jnp
from jax import lax
from jax.experimental import pallas as pl
from jax.experimental.pallas import tpu as pltpu

D_MODEL = 4096
HEAD_DIM = 128
EPS = 1e-6
DN_HEADS = 16
DN_CONV = 4
DN_QK = DN_HEADS * HEAD_DIM
DN_VW = DN_HEADS * HEAD_DIM
SWA_GROUPS = ((128, 1), (512, 4), (2048, 16))
SWA_HEADS_PER_GROUP = 8
SWA_BLOCK = 128
SWA_W = SWA_HEADS_PER_GROUP * HEAD_DIM
REL_BUCKETS = 32
REL_MAX_DIST = 2048
N_EXPERTS = 64
N_EXPERT_GROUPS = 8
TOPK_GROUPS = 4
TOP_K = 8
D_EXPERT = 768
ROUTED_SCALE = 2.5
MOE_BLOCK = 256

COL_BA = 4 * DN_QK
COL_SWA = COL_BA + 2 * DN_HEADS
COL_GATES = COL_SWA + 3 * len(SWA_GROUPS) * SWA_W
COL_END = COL_GATES + 2 * D_MODEL

LANES = 128
VMEM_LIMIT_BYTES = 56 * 1024 * 1024

DN_TILE = 256
DN_HEADS_PER_STEP = 4
DN_INV_BASE = 16
CONV_PAD = 8
ROUTE_TILE = 256
PLAN_TILE = 512
DISPATCH_TILE = 128
COMBINE_TILE = 64

F32 = jnp.float32
BF16 = jnp.bfloat16
U32 = jnp.uint32
I32 = jnp.int32
HI_MASK = 0xFFFF0000

_NT = (((1,), (1,)), ((), ()))


def _compiler_params(semantics):
    return pltpu.CompilerParams(dimension_semantics=semantics, vmem_limit_bytes=VMEM_LIMIT_BYTES)


def _pack_bf16_pair(lo, hi):
    lo_bits = pltpu.bitcast(lo.astype(BF16).astype(F32), U32) >> jnp.uint32(16)
    hi_bits = pltpu.bitcast(hi.astype(BF16).astype(F32), U32) & jnp.uint32(HI_MASK)
    return lo_bits | hi_bits


def _unpack_bf16_pair(w):
    lo = pltpu.bitcast(w << jnp.uint32(16), F32)
    hi = pltpu.bitcast(w & jnp.uint32(HI_MASK), F32)
    return lo, hi


def _mm_kernel(a_ref, b_ref, o_ref):
    a = a_ref[...].astype(BF16)
    b = b_ref[...].astype(BF16)
    o_ref[...] = jnp.dot(a, b, preferred_element_type=F32).astype(o_ref.dtype)


def dense_matmul(a, b, *, tm, tn, out_dtype, name, col_lo=0, col_hi=None):
    m, k = a.shape
    col_hi = b.shape[1] if col_hi is None else col_hi
    n = col_hi - col_lo
    assert m % tm == 0 and col_lo % tn == 0
    off = col_lo // tn
    return pl.pallas_call(
        _mm_kernel,
        grid=(m // tm, pl.cdiv(n, tn)),
        in_specs=[pl.BlockSpec((tm, k), lambda i, j: (i, 0)),
                  pl.BlockSpec((k, tn), lambda i, j: (0, j + off))],
        out_specs=pl.BlockSpec((tm, tn), lambda i, j: (i, j)),
        out_shape=jax.ShapeDtypeStruct((m, n), out_dtype),
        compiler_params=_compiler_params(("parallel", "arbitrary")),
        name=name,
    )(a, b)


def _swiglu_up_kernel(a_ref, wg_ref, wu_ref, o_ref):
    a = a_ref[...]
    g = jnp.dot(a, wg_ref[...].astype(BF16), preferred_element_type=F32)
    u = jnp.dot(a, wu_ref[...].astype(BF16), preferred_element_type=F32)
    o_ref[...] = (g * jax.nn.sigmoid(g) * u).astype(o_ref.dtype)


def swiglu_up(a, wg, wu, *, tm, tn, name):
    m, k = a.shape
    n = wg.shape[1]
    assert m % tm == 0 and n % tn == 0
    return pl.pallas_call(
        _swiglu_up_kernel,
        grid=(m // tm, n // tn),
        in_specs=[pl.BlockSpec((tm, k), lambda i, j: (i, 0)),
                  pl.BlockSpec((k, tn), lambda i, j: (0, j)),
                  pl.BlockSpec((k, tn), lambda i, j: (0, j))],
        out_specs=pl.BlockSpec((tm, tn), lambda i, j: (i, j)),
        out_shape=jax.ShapeDtypeStruct((m, n), BF16),
        compiler_params=_compiler_params(("parallel", "arbitrary")),
        name=name,
    )(a, wg, wu)


def _repack_kernel(w_ref, o_ref, *, lo):
    o_ref[...] = w_ref[:, lo:lo + o_ref.shape[1]].astype(o_ref.dtype)


def repack_columns(w, lo, hi, *, tk=128):
    k, n = w.shape
    return pl.pallas_call(
        functools.partial(_repack_kernel, lo=lo),
        grid=(k // tk,),
        in_specs=[pl.BlockSpec((tk, n), lambda i: (i, 0))],
        out_specs=pl.BlockSpec((tk, hi - lo), lambda i: (i, 0)),
        out_shape=jax.ShapeDtypeStruct((k, hi - lo), BF16),
        compiler_params=_compiler_params(("parallel",)),
        name="repack_columns",
    )(w)


def _prenorm_kernel(x_ref, nw_ref, sc_ref, sh_ref, h_ref):
    x = x_ref[...]
    y = x * lax.rsqrt(jnp.mean(x * x, axis=-1, keepdims=True) + EPS) * nw_ref[...]
    h_ref[...] = (y * (1.0 + sc_ref[...]) + sh_ref[...]).astype(h_ref.dtype)


def prenorm_modulate(x, nw, sc, sh, *, tm=512):
    t, d = x.shape
    vec = pl.BlockSpec((1, d), lambda i: (0, 0))
    return pl.pallas_call(
        _prenorm_kernel,
        grid=(t // tm,),
        in_specs=[pl.BlockSpec((tm, d), lambda i: (i, 0)), vec, vec, vec],
        out_specs=pl.BlockSpec((tm, d), lambda i: (i, 0)),
        out_shape=jax.ShapeDtypeStruct((t, d), BF16),
        compiler_params=_compiler_params(("parallel",)),
        name="prenorm_modulate",
    )(x, nw, sc, sh)


def _mm_residue_kernel(a_ref, b_ref, o_ref, res_ref):
    d = o_ref.shape[0]
    rows = o_ref.shape[1]
    res = jnp.dot(a_ref[...], b_ref[...], preferred_element_type=F32)
    if d == 1:
        o_ref[0] = res.astype(o_ref.dtype)
        return
    nslab = res.shape[1] // LANES
    for s in range(nslab):
        res_ref[s] = res[:, s * LANES:(s + 1) * LANES]
    for r in range(d):
        for s in range(nslab):
            o_ref[r, :, s * LANES:(s + 1) * LANES] = res_ref[s, pl.ds(r, rows, stride=d), :].astype(o_ref.dtype)


def project_residue_major(a, b, *, dilation, col_lo, col_hi, tm, tn, name):
    m, k = a.shape
    n = col_hi - col_lo
    assert m % tm == 0 and n % tn == 0 and col_lo % tn == 0 and tm % (dilation * 16) == 0
    off = col_lo // tn
    rows = tm // dilation
    return pl.pallas_call(
        _mm_residue_kernel,
        grid=(m // tm, n // tn),
        in_specs=[pl.BlockSpec((tm, k), lambda i, j: (i, 0)),
                  pl.BlockSpec((k, tn), lambda i, j: (0, j + off))],
        out_specs=pl.BlockSpec((dilation, rows, tn), lambda i, j: (0, i, j)),
        out_shape=jax.ShapeDtypeStruct((dilation, m // dilation, n), BF16),
        scratch_shapes=[pltpu.VMEM((tn // LANES, tm, LANES), F32)],
        compiler_params=_compiler_params(("parallel", "arbitrary")),
        name=name,
    )(a, b)


_HEAD_BATCH = ((0,), (0,))
_BMM = (((2,), (1,)), _HEAD_BATCH)
_BMM_NT = (((2,), (2,)), _HEAD_BATCH)
_BMM_TN = (((1,), (1,)), _HEAD_BATCH)


def _bdot(a, b, dims=_BMM):
    return lax.dot_general(a.astype(BF16), b.astype(BF16), dims, preferred_element_type=F32)


def _unit_lower_inverse(low, row, col):
    c = low.shape[-1]
    s = DN_INV_BASE
    same = (row // s) == (col // s)
    ld = jnp.where(same, low, 0.0)
    x = jnp.where(row == col, 1.0, 0.0) - ld
    p = _bdot(ld, ld)
    steps = int(math.log2(s)) - 1
    for it in range(steps):
        x = x + _bdot(x, p)
        if it + 1 < steps:
            p = _bdot(p, p)
    while s < c:
        off = ((row // (2 * s)) == (col // (2 * s))) & ((row // s) != (col // s))
        b = jnp.where(off, low, 0.0)
        x = x - _bdot(_bdot(x, b), x)
        s *= 2
    return x


def _dn_kernel(q_ref, k_ref, v_ref, z_ref, cwq_ref, cwk_ref, cwv_ref, bcol_ref, gcol_ref, grow_ref, nw_ref,
               o_ref, state_ref, xe_ref):
    i = pl.program_id(1)
    c = q_ref.shape[0]
    gw = q_ref.shape[1]

    @pl.when(i == 0)
    def _():
        state_ref[...] = jnp.zeros_like(state_ref)
        xe_ref[0:CONV_PAD, :] = jnp.zeros((CONV_PAD, xe_ref.shape[1]), F32)

    xe_ref[CONV_PAD:CONV_PAD + c, 0:gw] = q_ref[...]
    xe_ref[CONV_PAD:CONV_PAD + c, gw:2 * gw] = k_ref[...]
    xe_ref[CONV_PAD:CONV_PAD + c, 2 * gw:3 * gw] = v_ref[...]

    def conv_silu(lo, w_ref):
        acc = w_ref[DN_CONV - 1:DN_CONV, :] * xe_ref[CONV_PAD:CONV_PAD + c, lo:lo + gw]
        for j in range(DN_CONV - 1):
            r0 = CONV_PAD - (DN_CONV - 1) + j
            acc = acc + w_ref[j:j + 1, :] * xe_ref[r0:r0 + c, lo:lo + gw]
        return acc * jax.nn.sigmoid(acc)

    hps = gw // HEAD_DIM

    def heads(x):
        return jnp.stack([x[:, j * HEAD_DIM:(j + 1) * HEAD_DIM] for j in range(hps)])

    qh = heads(conv_silu(0, cwq_ref))
    kh = heads(conv_silu(gw, cwk_ref))
    vh = heads(conv_silu(2 * gw, cwv_ref))
    xe_ref[0:CONV_PAD, :] = xe_ref[c:c + CONV_PAD, :]

    row = lax.broadcasted_iota(I32, (c, c), 0)
    col = lax.broadcasted_iota(I32, (c, c), 1)
    ge = row >= col
    gt = row > col
    bcols, gcols, grows = bcol_ref[0], gcol_ref[0], grow_ref[0]
    bcol = jnp.stack([bcols[:, j:j + 1] for j in range(hps)])
    gcol = jnp.stack([gcols[:, j:j + 1] for j in range(hps)])
    grow = jnp.stack([grows[j:j + 1, :] for j in range(hps)])

    qn = qh * (lax.rsqrt(jnp.sum(qh * qh, axis=-1, keepdims=True) + EPS) * (HEAD_DIM ** -0.5))
    kn = kh * lax.rsqrt(jnp.sum(kh * kh, axis=-1, keepdims=True) + EPS)
    diff = gcol - grow
    decay = jnp.where(ge, jnp.exp(jnp.where(ge, diff, 0.0)), 0.0)
    kb = kn * bcol
    kf = kn.astype(BF16)
    low = jnp.where(gt, _bdot(kb, kf, _BMM_NT) * decay, 0.0)
    attn = _bdot(qn, kf, _BMM_NT) * decay
    tinv = _unit_lower_inverse(low, row, col)
    eg = jnp.exp(gcol)
    rhs = jnp.concatenate([vh * bcol, kb * eg], axis=2)
    uw = _bdot(tinv, rhs)
    u, w = uw[:, :, :HEAD_DIM], uw[:, :, HEAD_DIM:]
    s = state_ref[...]
    sb = s.astype(BF16)
    v_new = u - _bdot(w, sb)
    vnb = v_new.astype(BF16)
    o = _bdot(qn * eg, sb) + _bdot(attn, vnb)
    glast = grow[:, :, c - 1:c]
    kdec = kn * jnp.exp(glast - gcol)
    state_ref[...] = s * jnp.exp(glast) + _bdot(kdec, vnb, _BMM_TN)
    on = o * lax.rsqrt(jnp.mean(o * o, axis=-1, keepdims=True) + EPS) * nw_ref[...]
    for j in range(hps):
        hs = slice(j * HEAD_DIM, (j + 1) * HEAD_DIM)
        zf = z_ref[:, hs]
        o_ref[:, hs] = (on[j] * (zf * jax.nn.sigmoid(zf))).astype(o_ref.dtype)


def _dn_gates(ba, a_log, dt_bias):
    t = ba.shape[0]
    hps = DN_HEADS_PER_STEP
    ng = DN_HEADS // hps
    beta = jax.nn.sigmoid(ba[:, :DN_HEADS])
    g = -jnp.exp(a_log) * jax.nn.softplus(ba[:, DN_HEADS:2 * DN_HEADS] + dt_bias)
    gc = jnp.cumsum(g.reshape(t // DN_TILE, DN_TILE, DN_HEADS), axis=1).reshape(t, DN_HEADS)
    bcol = beta.reshape(t, ng, hps).transpose(1, 0, 2)
    gcol = gc.reshape(t, ng, hps).transpose(1, 0, 2)
    grow = gc.reshape(t, ng, hps).transpose(1, 2, 0)
    return bcol, gcol, grow


def gated_deltanet(proj_dn, ba, conv_w, a_log, dt_bias, norm_w):
    t = proj_dn.shape[0]
    c = DN_TILE
    hps = DN_HEADS_PER_STEP
    gw = hps * HEAD_DIM
    ng = DN_HEADS // hps
    bcol, gcol, grow = _dn_gates(ba, a_log, dt_bias)
    nw = norm_w.reshape(1, HEAD_DIM).astype(F32)
    return pl.pallas_call(
        _dn_kernel,
        grid=(ng, t // c),
        in_specs=[pl.BlockSpec((c, gw), lambda g, i: (i, g)),
                  pl.BlockSpec((c, gw), lambda g, i: (i, ng + g)),
                  pl.BlockSpec((c, gw), lambda g, i: (i, 2 * ng + g)),
                  pl.BlockSpec((c, gw), lambda g, i: (i, 3 * ng + g)),
                  pl.BlockSpec((DN_CONV, gw), lambda g, i: (0, g)),
                  pl.BlockSpec((DN_CONV, gw), lambda g, i: (0, ng + g)),
                  pl.BlockSpec((DN_CONV, gw), lambda g, i: (0, 2 * ng + g)),
                  pl.BlockSpec((1, c, hps), lambda g, i: (g, i, 0)),
                  pl.BlockSpec((1, c, hps), lambda g, i: (g, i, 0)),
                  pl.BlockSpec((1, hps, c), lambda g, i: (g, 0, i)),
                  pl.BlockSpec((1, HEAD_DIM), lambda g, i: (0, 0))],
        out_specs=pl.BlockSpec((c, gw), lambda g, i: (i, g)),
        out_shape=jax.ShapeDtypeStruct((t, DN_VW), BF16),
        scratch_shapes=[pltpu.VMEM((hps, HEAD_DIM, HEAD_DIM), F32),
                        pltpu.VMEM((c + CONV_PAD, 3 * gw), F32)],
        compiler_params=_compiler_params(("parallel", "arbitrary")),
        name="gated_deltanet",
    )(proj_dn, proj_dn, proj_dn, proj_dn, conv_w, conv_w, conv_w, bcol, gcol, grow, nw)


def _swa_kernel(q_ref, kp_ref, kc_ref, vp_ref, vc_ref, bp_ref, bc_ref, o_ref, lse_ref):
    n = pl.program_id(1)
    blk = SWA_BLOCK
    qi = lax.broadcasted_iota(I32, (blk, blk), 0)
    kj = lax.broadcasted_iota(I32, (blk, blk), 1)
    valid_prev = (kj >= qi) & (n > 0)
    valid_cur = kj <= qi
    scale = HEAD_DIM ** -0.5
    nh = SWA_HEADS_PER_GROUP

    def heads(ref):
        return jnp.stack([ref[:, h * HEAD_DIM:(h + 1) * HEAD_DIM] for h in range(nh)])

    q = heads(q_ref)
    sp = lax.dot_general(q, heads(kp_ref), _BMM_NT, preferred_element_type=F32) * scale + bp_ref[...]
    sc = lax.dot_general(q, heads(kc_ref), _BMM_NT, preferred_element_type=F32) * scale + bc_ref[...]
    sp = jnp.where(valid_prev, sp, -jnp.inf)
    sc = jnp.where(valid_cur, sc, -jnp.inf)
    m = jnp.maximum(jnp.max(sp, axis=-1, keepdims=True), jnp.max(sc, axis=-1, keepdims=True))
    pp = jnp.exp(sp - m)
    pc = jnp.exp(sc - m)
    den = jnp.sum(pp, axis=-1, keepdims=True) + jnp.sum(pc, axis=-1, keepdims=True)
    o = (lax.dot_general(pp.astype(BF16), heads(vp_ref), _BMM, preferred_element_type=F32)
         + lax.dot_general(pc.astype(BF16), heads(vc_ref), _BMM, preferred_element_type=F32))
    o = o / den
    lse = m + jnp.log(den)
    lane = lax.broadcasted_iota(I32, (blk, HEAD_DIM), 1)
    lse_all = jnp.zeros((blk, HEAD_DIM), F32)
    for h in range(nh):
        o_ref[:, h * HEAD_DIM:(h + 1) * HEAD_DIM] = o[h].astype(o_ref.dtype)
        lse_all = jnp.where(lane == h, lse[h], lse_all)
    lse_ref[...] = lse_all


def _t5_causal_bucket(dist):
    max_exact = REL_BUCKETS // 2
    d = jnp.maximum(dist, 0)
    log_ratio = jnp.log(jnp.maximum(d, 1).astype(F32) / max_exact) / math.log(REL_MAX_DIST / max_exact)
    large = jnp.minimum(max_exact + (log_ratio * (REL_BUCKETS - max_exact)).astype(I32), REL_BUCKETS - 1)
    return jnp.where(d < max_exact, d, large)


def _swa_bias_blocks(rel_bias_g, dilation):
    blk = SWA_BLOCK
    qi = jnp.arange(blk, dtype=I32)[:, None]
    kj = jnp.arange(blk, dtype=I32)[None, :]
    bp = jnp.moveaxis(rel_bias_g[_t5_causal_bucket((qi + blk - kj) * dilation)], -1, 0)
    bc = jnp.moveaxis(rel_bias_g[_t5_causal_bucket((qi - kj) * dilation)], -1, 0)
    return bp.astype(F32), bc.astype(F32)


def swa_group(qkv, rel_bias_g, dilation):
    d, length, _ = qkv.shape
    blk = SWA_BLOCK
    nb = length // blk
    bp, bc = _swa_bias_blocks(rel_bias_g, dilation)

    def prev(n):
        return jnp.maximum(n - 1, 0)

    def blk3(f):
        return pl.BlockSpec((None, blk, SWA_W), f)

    return pl.pallas_call(
        _swa_kernel,
        grid=(d, nb),
        in_specs=[blk3(lambda r, n: (r, n, 0)),
                  blk3(lambda r, n: (r, prev(n), 1)), blk3(lambda r, n: (r, n, 1)),
                  blk3(lambda r, n: (r, prev(n), 2)), blk3(lambda r, n: (r, n, 2)),
                  pl.BlockSpec((SWA_HEADS_PER_GROUP, blk, blk), lambda r, n: (0, 0, 0)),
                  pl.BlockSpec((SWA_HEADS_PER_GROUP, blk, blk), lambda r, n: (0, 0, 0))],
        out_specs=[blk3(lambda r, n: (r, n, 0)),
                   pl.BlockSpec((None, blk, HEAD_DIM), lambda r, n: (r, n, 0))],
        out_shape=[jax.ShapeDtypeStruct((d, length, SWA_W), BF16),
                   jax.ShapeDtypeStruct((d, length, HEAD_DIM), F32)],
        compiler_params=_compiler_params(("parallel", "arbitrary")),
        name=f"swa_d{dilation}",
    )(qkv, qkv, qkv, qkv, qkv, bp, bc)


def _mix_kernel(odn_ref, o1_ref, o4_ref, o16_ref, l1_ref, l4_ref, l16_ref, gd_ref, gs_ref, wd_ref, ws_ref,
                out_ref, a_ref, oscr_ref, lscr_ref):
    j = pl.program_id(1)
    tm = odn_ref.shape[0]
    groups = ((o1_ref, l1_ref), (o4_ref, l4_ref), (o16_ref, l16_ref))

    @pl.when(j == 0)
    def _():
        for g, (o_ref, l_ref) in enumerate(groups):
            d = o_ref.shape[0]
            rows = tm // d
            for r in range(d):
                if d == 1:
                    lscr_ref[g] = l_ref[0]
                else:
                    lscr_ref[g, pl.ds(r, rows, stride=d), :] = l_ref[r]
                for s in range(SWA_HEADS_PER_GROUP):
                    blk = o_ref[r, :, s * HEAD_DIM:(s + 1) * HEAD_DIM].astype(F32)
                    if d == 1:
                        oscr_ref[g, s] = blk
                    else:
                        oscr_ref[g, s, pl.ds(r, rows, stride=d), :] = blk
        ls = [lscr_ref[g] for g in range(len(groups))]
        m = jnp.maximum(jnp.maximum(ls[0], ls[1]), ls[2])
        es = [jnp.exp(l - m) for l in ls]
        inv = 1.0 / (es[0] + es[1] + es[2])
        for s in range(SWA_HEADS_PER_GROUP):
            acc = jnp.zeros((tm, HEAD_DIM), F32)
            for g in range(len(groups)):
                acc = acc + (es[g] * inv)[:, s:s + 1] * oscr_ref[g, s]
            a_ref[:, s * HEAD_DIM:(s + 1) * HEAD_DIM] = acc.astype(a_ref.dtype)

    y_dn = jnp.dot(odn_ref[...], wd_ref[...].astype(BF16), preferred_element_type=F32)
    y_swa = jnp.dot(a_ref[...], ws_ref[...].astype(BF16), preferred_element_type=F32)
    mixed = jax.nn.sigmoid(gd_ref[...].astype(F32)) * y_dn + jax.nn.sigmoid(gs_ref[...].astype(F32)) * y_swa
    out_ref[...] = mixed.astype(out_ref.dtype)


def mix_up(o_dn, swa_os, swa_ls, gates, w_up_dn, w_up_swa, *, tm=512, tn=512):
    t = o_dn.shape[0]
    dm = w_up_dn.shape[1]
    nj = dm // tn
    in_specs = [pl.BlockSpec((tm, o_dn.shape[1]), lambda i, j: (i, 0))]
    for o in swa_os:
        d = o.shape[0]
        in_specs.append(pl.BlockSpec((d, tm // d, SWA_W), lambda i, j: (0, i, 0)))
    for l in swa_ls:
        d = l.shape[0]
        in_specs.append(pl.BlockSpec((d, tm // d, HEAD_DIM), lambda i, j: (0, i, 0)))
    in_specs += [pl.BlockSpec((tm, tn), lambda i, j: (i, j)),
                 pl.BlockSpec((tm, tn), lambda i, j: (i, nj + j)),
                 pl.BlockSpec((w_up_dn.shape[0], tn), lambda i, j: (0, j)),
                 pl.BlockSpec((w_up_swa.shape[0], tn), lambda i, j: (0, j))]
    ng = len(swa_os)
    return pl.pallas_call(
        _mix_kernel,
        grid=(t // tm, nj),
        in_specs=in_specs,
        out_specs=pl.BlockSpec((tm, tn), lambda i, j: (i, j)),
        out_shape=jax.ShapeDtypeStruct((t, dm), BF16),
        scratch_shapes=[pltpu.VMEM((tm, SWA_W), BF16),
                        pltpu.VMEM((ng, SWA_HEADS_PER_GROUP, tm, HEAD_DIM), F32),
                        pltpu.VMEM((ng, tm, HEAD_DIM), F32)],
        compiler_params=_compiler_params(("parallel", "arbitrary")),
        name="mix_up",
    )(o_dn, *swa_os, *swa_ls, gates, gates, w_up_dn, w_up_swa)


def _route_kernel(o_ref, x_ref, gt1_ref, np1_ref, np2_ref, sc2_ref, sh2_ref, wr_ref, rb_ref,
                  x1_ref, h2b_ref, h2p_ref, idx_ref, gate_ref):
    o = o_ref[...]
    x1 = x_ref[...] + gt1_ref[...] * (o * lax.rsqrt(jnp.mean(o * o, axis=-1, keepdims=True) + EPS) * np1_ref[...])
    x1_ref[...] = x1
    h2 = (x1 * lax.rsqrt(jnp.mean(x1 * x1, axis=-1, keepdims=True) + EPS) * np2_ref[...]) * (1.0 + sc2_ref[...]) \
        + sh2_ref[...]
    h2b_ref[...] = h2.astype(BF16)
    half = h2.shape[1] // 2
    h2p_ref[...] = _pack_bf16_pair(h2[:, :half], h2[:, half:])

    tm = h2.shape[0]
    e, g, w = N_EXPERTS, N_EXPERT_GROUPS, N_EXPERTS // N_EXPERT_GROUPS
    logits = lax.dot_general(wr_ref[...], h2, _NT, precision=lax.Precision.HIGHEST,
                             preferred_element_type=F32)
    scores = jax.nn.sigmoid(logits)
    sel = (scores + rb_ref[...]).reshape(g, w, tm)
    scores3 = scores.reshape(g, w, tm)
    neg = -jnp.inf
    wi = lax.broadcasted_iota(I32, (g, w, tm), 1)
    gi = lax.broadcasted_iota(I32, (g, w, tm), 0)
    m1 = jnp.max(sel, axis=1, keepdims=True)
    first = jnp.min(jnp.where(sel == m1, wi, w), axis=1, keepdims=True)
    m2 = jnp.max(jnp.where(wi == first, neg, sel), axis=1, keepdims=True)
    grp = m1 + m2
    gi1 = lax.broadcasted_iota(I32, (g, 1, tm), 0)
    chosen = jnp.zeros((g, 1, tm), F32)
    for _ in range(TOPK_GROUPS):
        m = jnp.max(grp, axis=0, keepdims=True)
        fg = jnp.min(jnp.where(grp == m, gi1, g), axis=0, keepdims=True)
        hit = gi1 == fg
        chosen = jnp.where(hit, 1.0, chosen)
        grp = jnp.where(hit, neg, grp)
    cand = jnp.where(chosen > 0.5, sel, neg)
    ei = gi * w + wi
    idx_rows, sc_rows = [], []
    for _ in range(TOP_K):
        m = jnp.max(jnp.max(cand, axis=0, keepdims=True), axis=1, keepdims=True)
        fe = jnp.min(jnp.min(jnp.where(cand == m, ei, e), axis=0, keepdims=True), axis=1, keepdims=True)
        hit = ei == fe
        sc = jnp.sum(jnp.sum(jnp.where(hit, scores3, 0.0), axis=0, keepdims=True), axis=1, keepdims=True)
        cand = jnp.where(hit, neg, cand)
        idx_rows.append(fe.reshape(1, tm))
        sc_rows.append(sc.reshape(1, tm))
    idx = jnp.concatenate(idx_rows, axis=0)
    sc = jnp.concatenate(sc_rows, axis=0)
    idx_ref[...] = idx
    gate_ref[...] = sc / jnp.sum(sc, axis=0, keepdims=True) * ROUTED_SCALE


def moe_route(o, x, gt1, np1, np2, sc2, sh2, w_router, router_bias):
    t, d = x.shape
    tm = ROUTE_TILE
    wr_t = w_router.T.astype(F32)
    rb = router_bias.reshape(N_EXPERTS, 1).astype(F32)

    def row(i):
        return (i, 0)

    def fixed(i):
        return (0, 0)

    vec = pl.BlockSpec((1, d), fixed)
    return pl.pallas_call(
        _route_kernel,
        grid=(t // tm,),
        in_specs=[pl.BlockSpec((tm, d), row), pl.BlockSpec((tm, d), row), vec, vec, vec, vec, vec,
                  pl.BlockSpec((N_EXPERTS, d), fixed), pl.BlockSpec((N_EXPERTS, 1), fixed)],
        out_specs=[pl.BlockSpec((tm, d), row), pl.BlockSpec((tm, d), row), pl.BlockSpec((tm, d // 2), row),
                   pl.BlockSpec((TOP_K, tm), lambda i: (0, i)), pl.BlockSpec((TOP_K, tm), lambda i: (0, i))],
        out_shape=[jax.ShapeDtypeStruct((t, d), F32), jax.ShapeDtypeStruct((t, d), BF16),
                   jax.ShapeDtypeStruct((t, d // 2), U32),
                   jax.ShapeDtypeStruct((TOP_K, t), I32), jax.ShapeDtypeStruct((TOP_K, t), F32)],
        compiler_params=_compiler_params(("parallel",)),
        name="moe_route",
    )(o, x, gt1, np1, np2, sc2, sh2, wr_t, rb)


def _plan_kernel(idx_ref, dest_ref, blk_e_ref, pad_end_ref, nused_ref, counts_ref, carry_ref, pstart_ref):
    p = pl.program_id(0)
    i = pl.program_id(1)
    e = N_EXPERTS
    tn = idx_ref.shape[1]
    idx = idx_ref[...]
    ei = lax.broadcasted_iota(I32, (e, tn), 0)
    member = jnp.zeros((e, tn), F32)
    for k in range(TOP_K):
        member = member + jnp.where(idx[k:k + 1, :] == ei, 1.0, 0.0)
    tile_counts = jnp.sum(member, axis=1, keepdims=True)

    @pl.when((p == 0) & (i == 0))
    def _():
        counts_ref[...] = jnp.zeros_like(counts_ref)

    @pl.when(p == 0)
    def _():
        counts_ref[...] += tile_counts

    @pl.when((p == 1) & (i == 0))
    def _():
        counts = counts_ref[...]
        padded = jnp.ceil(counts * (1.0 / MOE_BLOCK)) * MOE_BLOCK
        sub = lax.broadcasted_iota(I32, (e, e), 0)
        lan = lax.broadcasted_iota(I32, (e, e), 1)
        start_row = jnp.sum(jnp.where(sub < lan, padded, 0.0), axis=0, keepdims=True)
        end_row = jnp.sum(jnp.where(sub <= lan, padded, 0.0), axis=0, keepdims=True)
        pstart_ref[...] = jnp.sum(jnp.where(sub == lan, start_row, 0.0), axis=1, keepdims=True)
        carry_ref[...] = jnp.zeros_like(carry_ref)
        pad_end_ref[...] = end_row.astype(I32)
        nb = blk_e_ref.shape[0]
        bstart = (lax.broadcasted_iota(I32, (nb, e), 0) * MOE_BLOCK).astype(F32)
        below = jnp.sum(jnp.where(end_row <= bstart, 1.0, 0.0), axis=1, keepdims=True)
        blk_e_ref[...] = jnp.minimum(below, e - 1.0).astype(I32)
        nused_ref[...] = (end_row[:, e - 1:e] * (1.0 / MOE_BLOCK)).astype(I32)

    @pl.when(p == 1)
    def _():
        tt = lax.broadcasted_iota(I32, (tn, tn), 0)
        tc = lax.broadcasted_iota(I32, (tn, tn), 1)
        before = jnp.where(tt < tc, 1.0, 0.0).astype(BF16)
        prefix = jnp.dot(member.astype(BF16), before, preferred_element_type=F32)
        base = prefix + carry_ref[...] + pstart_ref[...]
        rows = []
        for k in range(TOP_K):
            rows.append(jnp.sum(jnp.where(idx[k:k + 1, :] == ei, base, 0.0), axis=0, keepdims=True))
        dest_ref[...] = jnp.concatenate(rows, axis=0).astype(I32)
        carry_ref[...] += tile_counts


def moe_plan(idx, nb):
    k, t = idx.shape
    tn = PLAN_TILE
    e = N_EXPERTS
    return pl.pallas_call(
        _plan_kernel,
        grid=(2, t // tn),
        in_specs=[pl.BlockSpec((k, tn), lambda p, i: (0, i))],
        out_specs=[pl.BlockSpec((k, tn), lambda p, i: (0, i * p)),
                   pl.BlockSpec((nb, 1), lambda p, i: (0, 0)),
                   pl.BlockSpec((1, e), lambda p, i: (0, 0)),
                   pl.BlockSpec((1, 1), lambda p, i: (0, 0))],
        out_shape=[jax.ShapeDtypeStruct((k, t), I32), jax.ShapeDtypeStruct((nb, 1), I32),
                   jax.ShapeDtypeStruct((1, e), I32), jax.ShapeDtypeStruct((1, 1), I32)],
        scratch_shapes=[pltpu.VMEM((e, 1), F32), pltpu.VMEM((e, 1), F32), pltpu.VMEM((e, 1), F32)],
        compiler_params=_compiler_params(("arbitrary", "arbitrary")),
        name="moe_plan",
    )(idx)


def _dispatch_kernel(dest_ref, pend_ref, h_ref, xs_hbm, zero_ref, sem):
    i = pl.program_id(0)
    tm = h_ref.shape[0]

    def block_clear(b):
        start = pl.multiple_of(b * MOE_BLOCK, MOE_BLOCK)
        return pltpu.make_async_copy(zero_ref, xs_hbm.at[pl.ds(start, MOE_BLOCK), :], sem.at[1])

    @pl.when(i == 0)
    def _():
        zero_ref[...] = jnp.zeros_like(zero_ref)
        shift = int(math.log2(MOE_BLOCK))
        n_blocks = xs_hbm.shape[0] // MOE_BLOCK
        first_unused = lax.shift_right_logical(pend_ref[N_EXPERTS - 1], shift)

        def last_block(e):
            return lax.shift_right_logical(jnp.maximum(pend_ref[e] - MOE_BLOCK, 0), shift)

        def zstart(e, c):
            block_clear(last_block(e)).start()
            return c
        lax.fori_loop(0, N_EXPERTS, zstart, 0)

        def tstart(b, c):
            block_clear(b).start()
            return c
        lax.fori_loop(first_unused, n_blocks, tstart, 0)

        def zwait(e, c):
            block_clear(0).wait()
            return c
        lax.fori_loop(0, N_EXPERTS, zwait, 0)
        lax.fori_loop(first_unused, n_blocks, zwait, 0)

    def row_copy(t, row):
        return pltpu.make_async_copy(h_ref.at[pl.ds(t, 1), :], xs_hbm.at[pl.ds(row, 1), :], sem.at[0])

    def start(t, c):
        for k in range(TOP_K):
            row_copy(t, dest_ref[t * TOP_K + k]).start()
        return c
    lax.fori_loop(0, tm, start, 0)

    def wait(t, c):
        for k in range(TOP_K):
            row_copy(0, 0).wait()
        return c
    lax.fori_loop(0, tm, wait, 0)


def moe_dispatch(h_packed, dest_tk, pad_end, rows):
    t, w = h_packed.shape
    tm = DISPATCH_TILE
    dflat = dest_tk.reshape(t * TOP_K)
    return pl.pallas_call(
        _dispatch_kernel,
        grid=(t // tm,),
        in_specs=[pl.BlockSpec((tm * TOP_K,), lambda i: (i,), memory_space=pltpu.SMEM),
                  pl.BlockSpec((N_EXPERTS,), lambda i: (0,), memory_space=pltpu.SMEM),
                  pl.BlockSpec((tm, w), lambda i: (i, 0))],
        out_specs=pl.BlockSpec(memory_space=pl.ANY),
        out_shape=jax.ShapeDtypeStruct((rows, w), U32),
        scratch_shapes=[pltpu.VMEM((MOE_BLOCK, w), U32), pltpu.SemaphoreType.DMA((2,))],
        compiler_params=_compiler_params(("arbitrary",)),
        name="moe_dispatch",
    )(dflat, pad_end.reshape(N_EXPERTS), h_packed)


def _expert_changed(blk_e_ref, i):
    prev = blk_e_ref[jnp.maximum(i - 1, 0)]
    return (i == 0) | (blk_e_ref[i] != prev)


def _gmm1_kernel(blk_e_ref, next_e_ref, nused_ref, x_ref, w1_hbm, w3_hbm, h_ref, st1_ref, st3_ref, w1b_ref, w3b_ref,
                 sem):
    i = pl.program_id(0)

    def weight_copies(e):
        return (pltpu.make_async_copy(w1_hbm.at[e], st1_ref, sem.at[0]),
                pltpu.make_async_copy(w3_hbm.at[e], st3_ref, sem.at[1]))

    @pl.when(i == 0)
    def _():
        for cp in weight_copies(blk_e_ref[0]):
            cp.start()

    @pl.when(i < nused_ref[0])
    def _():
        @pl.when(_expert_changed(blk_e_ref, i))
        def _():
            for cp in weight_copies(blk_e_ref[i]):
                cp.wait()
            w1b_ref[...] = st1_ref[...].astype(BF16)
            w3b_ref[...] = st3_ref[...].astype(BF16)

            @pl.when(next_e_ref[i] >= 0)
            def _():
                for cp in weight_copies(next_e_ref[i]):
                    cp.start()

        lo, hi = _unpack_bf16_pair(x_ref[...])
        lo = lo.astype(BF16)
        hi = hi.astype(BF16)
        half = lo.shape[1]
        g = (jnp.dot(lo, w1b_ref[:half, :], preferred_element_type=F32)
             + jnp.dot(hi, w1b_ref[half:, :], preferred_element_type=F32))
        u = (jnp.dot(lo, w3b_ref[:half, :], preferred_element_type=F32)
             + jnp.dot(hi, w3b_ref[half:, :], preferred_element_type=F32))
        h_ref[...] = (g * jax.nn.sigmoid(g) * u).astype(h_ref.dtype)

    @pl.when(i >= nused_ref[0])
    def _():
        h_ref[...] = jnp.zeros_like(h_ref)


def _next_expert_table(blk_e, n_used):
    nb = blk_e.shape[0]
    pos = jnp.arange(nb, dtype=I32)
    prev = jnp.concatenate([blk_e[:1], blk_e[:-1]])
    starts = (blk_e != prev) & (pos < n_used[0]) & (pos > 0)
    start_pos = jnp.where(starts, pos, nb)
    shifted = jnp.concatenate([start_pos[1:], jnp.full((1,), nb, I32)])
    nxt = lax.cummin(shifted, axis=0, reverse=True)
    return jnp.where(nxt < nb, blk_e[jnp.minimum(nxt, nb - 1)], -1).astype(I32)


def grouped_gate_up(xs, blk_e, n_used, w1, w3):
    rows, half = xs.shape
    d = 2 * half
    nb = rows // MOE_BLOCK
    f = w1.shape[2]
    next_e = _next_expert_table(blk_e, n_used)

    def row_blk(i, nu):
        return jnp.minimum(i, nu[0] - 1)

    grid_spec = pltpu.PrefetchScalarGridSpec(
        num_scalar_prefetch=3,
        grid=(nb,),
        in_specs=[pl.BlockSpec((MOE_BLOCK, half), lambda i, e, ne, nu: (row_blk(i, nu), 0)),
                  pl.BlockSpec(memory_space=pl.ANY),
                  pl.BlockSpec(memory_space=pl.ANY)],
        out_specs=pl.BlockSpec((MOE_BLOCK, f), lambda i, e, ne, nu: (i, 0)),
        scratch_shapes=[pltpu.VMEM((d, f), F32), pltpu.VMEM((d, f), F32),
                        pltpu.VMEM((d, f), BF16), pltpu.VMEM((d, f), BF16),
                        pltpu.SemaphoreType.DMA((2,))],
    )
    return pl.pallas_call(
        _gmm1_kernel,
        grid_spec=grid_spec,
        out_shape=jax.ShapeDtypeStruct((rows, f), BF16),
        compiler_params=_compiler_params(("arbitrary",)),
        name="moe_gate_up",
    )(blk_e, next_e, n_used, xs, w1, w3)


def _gmm2_kernel(blk_e_ref, nused_ref, h_ref, w2_ref, y_ref, w2b_ref):
    i = pl.program_id(0)

    @pl.when(i < nused_ref[0])
    def _():
        @pl.when(_expert_changed(blk_e_ref, i))
        def _():
            w2b_ref[...] = w2_ref[0].astype(BF16)

        y = jnp.dot(h_ref[...], w2b_ref[...], preferred_element_type=F32)
        half = y.shape[1] // 2
        y_ref[...] = _pack_bf16_pair(y[:, :half], y[:, half:])

    @pl.when(i >= nused_ref[0])
    def _():
        y_ref[...] = jnp.zeros_like(y_ref)


def grouped_down(hs, blk_e, n_used, w2):
    rows, f = hs.shape
    nb = rows // MOE_BLOCK
    d = w2.shape[2]

    def row_blk(i, nu):
        return jnp.minimum(i, nu[0] - 1)

    grid_spec = pltpu.PrefetchScalarGridSpec(
        num_scalar_prefetch=2,
        grid=(nb,),
        in_specs=[pl.BlockSpec((MOE_BLOCK, f), lambda i, e, nu: (row_blk(i, nu), 0)),
                  pl.BlockSpec((1, f, d), lambda i, e, nu: (e[row_blk(i, nu)], 0, 0))],
        out_specs=pl.BlockSpec((MOE_BLOCK, d // 2), lambda i, e, nu: (i, 0)),
        scratch_shapes=[pltpu.VMEM((f, d), BF16)],
    )
    return pl.pallas_call(
        _gmm2_kernel,
        grid_spec=grid_spec,
        out_shape=jax.ShapeDtypeStruct((rows, d // 2), U32),
        compiler_params=_compiler_params(("arbitrary",)),
        name="moe_down",
    )(blk_e, n_used, hs, w2)


def _combine_kernel(dcur_ref, dnext_ref, y_hbm, gate_ref, shared_ref, x_ref, gt_ref, nw_ref, o_ref, gbuf, sem):
    i = pl.program_id(0)
    n = pl.num_programs(0)
    tm = x_ref.shape[0]
    slot = lax.rem(i, 2)

    def row_copy(s, src_row, dst_row):
        return pltpu.make_async_copy(y_hbm.at[pl.ds(src_row, 1), :], gbuf.at[s, pl.ds(dst_row, 1), :], sem.at[s])

    def start_tile(d_ref, s):
        def body(t, carry):
            for k in range(TOP_K):
                row_copy(s, d_ref[t * TOP_K + k], k * tm + t).start()
            return carry
        lax.fori_loop(0, tm, body, 0)

    @pl.when(i == 0)
    def _():
        start_tile(dcur_ref, 0)

    @pl.when(i + 1 < n)
    def _():
        start_tile(dnext_ref, 1 - slot)

    def wait_body(t, carry):
        for k in range(TOP_K):
            row_copy(slot, 0, 0).wait()
        return carry
    lax.fori_loop(0, tm, wait_body, 0)

    gate = gate_ref[...]
    half = gbuf.shape[2]
    acc_lo = shared_ref[:, :half].astype(F32)
    acc_hi = shared_ref[:, half:].astype(F32)
    for k in range(TOP_K):
        lo, hi = _unpack_bf16_pair(gbuf[slot, k * tm:(k + 1) * tm, :])
        gk = gate[:, k:k + 1]
        acc_lo = acc_lo + gk * lo
        acc_hi = acc_hi + gk * hi
    ms = (jnp.sum(acc_lo * acc_lo, axis=-1, keepdims=True) + jnp.sum(acc_hi * acc_hi, axis=-1, keepdims=True)) \
        * (1.0 / (2 * half))
    r = lax.rsqrt(ms + EPS)
    o_ref[:, :half] = x_ref[:, :half] + gt_ref[:, :half] * (acc_lo * r * nw_ref[:, :half])
    o_ref[:, half:] = x_ref[:, half:] + gt_ref[:, half:] * (acc_hi * r * nw_ref[:, half:])


def moe_combine(ys, dest_tk, gate_tk, shared, x, gt, nw):
    t, d = x.shape
    tm = COMBINE_TILE
    nt = t // tm
    dflat = dest_tk.reshape(t * TOP_K)
    return pl.pallas_call(
        _combine_kernel,
        grid=(nt,),
        in_specs=[pl.BlockSpec((tm * TOP_K,), lambda i: (i,), memory_space=pltpu.SMEM),
                  pl.BlockSpec((tm * TOP_K,), lambda i: (jnp.minimum(i + 1, nt - 1),), memory_space=pltpu.SMEM),
                  pl.BlockSpec(memory_space=pl.ANY),
                  pl.BlockSpec((tm, TOP_K), lambda i: (i, 0)),
                  pl.BlockSpec((tm, d), lambda i: (i, 0)),
                  pl.BlockSpec((tm, d), lambda i: (i, 0)),
                  pl.BlockSpec((1, d), lambda i: (0, 0)),
                  pl.BlockSpec((1, d), lambda i: (0, 0))],
        out_specs=pl.BlockSpec((tm, d), lambda i: (i, 0)),
        out_shape=jax.ShapeDtypeStruct((t, d), F32),
        scratch_shapes=[pltpu.VMEM((2, TOP_K * tm, d // 2), U32), pltpu.SemaphoreType.DMA((2,))],
        compiler_params=_compiler_params(("arbitrary",)),
        name="moe_combine",
    )(dflat, dflat, ys, gate_tk, shared, x, gt, nw)


def kernel(x, c, w_mod, b_mod, norm_pre1, norm_post1, w_in, conv_w, a_log, dt_bias, dn_norm_w, w_up_dn,
           w_up_swa, w_out, rel_bias, norm_pre2, norm_post2, w_router, router_bias, w1, w3, w2, ws1, ws3, ws2):
    B, S, D = x.shape
    assert B == 1
    T = B * S
    x = x.reshape(T, D)
    depth = w_mod.shape[0]
    n_swa_cols = 3 * SWA_W
    nb = -(-(T * TOP_K) // MOE_BLOCK) + N_EXPERTS
    for layer in range(depth):
        sc = jnp.broadcast_to(jax.nn.silu(c), (8, D))
        mod = dense_matmul(sc, w_mod[layer], tm=8, tn=1024, out_dtype=F32, name="adaln_mod")[:B] + b_mod[layer]
        sh1, sc1, gt1, sh2, sc2, gt2 = jnp.split(mod, 6, axis=-1)

        def vec(v):
            return v[layer].reshape(1, D)

        h = prenorm_modulate(x, vec(norm_pre1), sc1, sh1)
        wl = w_in[layer]
        proj_dn = dense_matmul(h, wl, tm=1024, tn=512, out_dtype=F32, name="in_proj_dn", col_lo=0, col_hi=COL_BA)
        ba = dense_matmul(h, wl, tm=1024, tn=128, out_dtype=F32, name="in_proj_ba", col_lo=COL_BA,
                          col_hi=COL_BA + 128)
        w_rest = repack_columns(wl, COL_SWA, COL_END)
        gates = dense_matmul(h, w_rest, tm=1024, tn=1024, out_dtype=BF16, name="in_proj_gates",
                             col_lo=COL_GATES - COL_SWA)
        o_dn = gated_deltanet(proj_dn, ba, conv_w[layer], a_log[layer], dt_bias[layer], dn_norm_w[layer])
        swa_os, swa_ls = [], []
        for gi, (_, dilation) in enumerate(SWA_GROUPS):
            qkv = project_residue_major(h, w_rest, dilation=dilation, col_lo=gi * n_swa_cols,
                                        col_hi=(gi + 1) * n_swa_cols, tm=1024, tn=1024,
                                        name=f"in_proj_swa_d{dilation}")
            heads = slice(gi * SWA_HEADS_PER_GROUP, (gi + 1) * SWA_HEADS_PER_GROUP)
            o_g, lse_g = swa_group(qkv, rel_bias[:, heads], dilation)
            swa_os.append(o_g)
            swa_ls.append(lse_g)
        mixed = mix_up(o_dn, swa_os, swa_ls, gates, w_up_dn[layer], w_up_swa[layer])
        o = dense_matmul(mixed, w_out[layer], tm=1024, tn=512, out_dtype=F32, name="out_proj")
        x1, h2b, h2p, idx, gate = moe_route(o, x, gt1, vec(norm_post1), vec(norm_pre2), sc2, sh2,
                                            w_router[layer], router_bias[layer])
        dest, blk_e, pad_end, n_used = moe_plan(idx, nb)
        dest_tk = dest.T
        gate_tk = gate.T
        xs = moe_dispatch(h2p, dest_tk, pad_end, nb * MOE_BLOCK)
        blk_e = blk_e.reshape(nb)
        n_used = n_used.reshape(1)
        hs = grouped_gate_up(xs, blk_e, n_used, w1[layer], w3[layer])
        ys = grouped_down(hs, blk_e, n_used, w2[layer])
        hsh = swiglu_up(h2b, ws1[layer], ws3[layer], tm=1024, tn=D_EXPERT // 2, name="shared_gate_up")
        shared = dense_matmul(hsh, ws2[layer], tm=1024, tn=512, out_dtype=F32, name="shared_down")
        x = moe_combine(ys, dest_tk, gate_tk, shared, x1, gt2, vec(norm_post2))
    return x.reshape(B, S, D)
```

```python
import math

import jax
import jax.numpy as jnp
from jax import lax
from jax.experimental import pallas as pl
from jax.experimental.pallas import tpu as pltpu

D_MODEL = 4096
HEAD_DIM = 128
EPS = 1e-6
DN_HEADS = 16
DN_CONV = 4
DN_QK = DN_HEADS * HEAD_DIM
DN_VW = DN_HEADS * HEAD_DIM
SWA_GROUPS = ((128, 1), (512, 4), (2048, 16))
SWA_HEADS_PER_GROUP = 8
SWA_BLOCK = 128
SWA_W = SWA_HEADS_PER_GROUP * HEAD_DIM
REL_BUCKETS = 32
REL_MAX_DIST = 2048
N_EXPERTS = 64
N_EXPERT_GROUPS = 8
TOPK_GROUPS = 4
TOP_K = 8
D_EXPERT = 768
ROUTED_SCALE = 2.5
MOE_BLOCK = 256

COL_BA = 4 * DN_QK
COL_SWA = COL_BA + 2 * DN_HEADS
COL_GATES = COL_SWA + 3 * len(SWA_GROUPS) * SWA_W
COL_END = COL_GATES + 2 * D_MODEL

LANES = 128
SUBLANES = 8
VMEM_LIMIT_BYTES = 56 * 1024 * 1024

DN_TILE = 256
DN_HEADS_PER_STEP = 4
DN_INV_BASE = 16
CONV_PAD = 8
ROUTE_TILE = 256
PLAN_TILE = 512
DISPATCH_TILE = 128
COMBINE_TILE = 64

F32 = jnp.float32
BF16 = jnp.bfloat16
U32 = jnp.uint32
I32 = jnp.int32
HI_MASK = 0xFFFF0000

_NT = (((1,), (1,)), ((), ()))


def _compiler_params(semantics):
    return pltpu.CompilerParams(dimension_semantics=semantics, vmem_limit_bytes=VMEM_LIMIT_BYTES)


def _pack_bf16_pair(lo, hi):
    lo_bits = pltpu.bitcast(lo.astype(BF16).astype(F32), U32) >> jnp.uint32(16)
    hi_bits = pltpu.bitcast(hi.astype(BF16).astype(F32), U32) & jnp.uint32(HI_MASK)
    return lo_bits | hi_bits


def _unpack_bf16_pair(w):
    lo = pltpu.bitcast(w << jnp.uint32(16), F32)
    hi = pltpu.bitcast(w & jnp.uint32(HI_MASK), F32)
    return lo, hi


def _mm_kernel(a_ref, b_ref, o_ref):
    a = a_ref[...].astype(BF16)
    b = b_ref[...].astype(BF16)
    o_ref[...] = jnp.dot(a, b, preferred_element_type=F32).astype(o_ref.dtype)


def dense_matmul(a, b, *, tm, tn, out_dtype, name, col_lo=0, col_hi=None):
    m, k = a.shape
    col_hi = b.shape[1] if col_hi is None else col_hi
    n = col_hi - col_lo
    assert m % tm == 0 and col_lo % tn == 0
    off = col_lo // tn
    return pl.pallas_call(
        _mm_kernel,
        grid=(m // tm, pl.cdiv(n, tn)),
        in_specs=[pl.BlockSpec((tm, k), lambda i, j: (i, 0)),
                  pl.BlockSpec((k, tn), lambda i, j: (0, j + off))],
        out_specs=pl.BlockSpec((tm, tn), lambda i, j: (i, j)),
        out_shape=jax.ShapeDtypeStruct((m, n), out_dtype),
        compiler_params=_compiler_params(("parallel", "arbitrary")),
        name=name,
    )(a, b)


def _swiglu_up_kernel(a_ref, wg_ref, wu_ref, o_ref):
    a = a_ref[...]
    g = jnp.dot(a, wg_ref[...].astype(BF16), preferred_element_type=F32)
    u = jnp.dot(a, wu_ref[...].astype(BF16), preferred_element_type=F32)
    o_ref[...] = (g * jax.nn.sigmoid(g) * u).astype(o_ref.dtype)


def swiglu_up(a, wg, wu, *, tm, tn, name):
    m, k = a.shape
    n = wg.shape[1]
    assert m % tm == 0 and n % tn == 0
    return pl.pallas_call(
        _swiglu_up_kernel,
        grid=(m // tm, n // tn),
        in_specs=[pl.BlockSpec((tm, k), lambda i, j: (i, 0)),
                  pl.BlockSpec((k, tn), lambda i, j: (0, j)),
                  pl.BlockSpec((k, tn), lambda i, j: (0, j))],
        out_specs=pl.BlockSpec((tm, tn), lambda i, j: (i, j)),
        out_shape=jax.ShapeDtypeStruct((m, n), BF16),
        compiler_params=_compiler_params(("parallel", "arbitrary")),
        name=name,
    )(a, wg, wu)


def _weight_rows_spec(k, tn, row_lo):
    assert row_lo % SUBLANES == 0 and tn % SUBLANES == 0
    return pl.BlockSpec((pl.Element(tn), pl.Element(k)),
                        lambda i, j: ((row_lo // SUBLANES + j * (tn // SUBLANES)) * SUBLANES, 0))


def _mm_nt_kernel(a_ref, w_ref, o_ref):
    a = a_ref[...].astype(BF16)
    w = w_ref[...].astype(BF16)
    o_ref[...] = lax.dot_general(a, w, _NT, preferred_element_type=F32).astype(o_ref.dtype)


def dense_matmul_nt(a, w_t, *, row_lo, row_hi, tm, tn, out_dtype, name):
    m, k = a.shape
    n = row_hi - row_lo
    assert m % tm == 0 and n % tn == 0
    return pl.pallas_call(
        _mm_nt_kernel,
        grid=(m // tm, n // tn),
        in_specs=[pl.BlockSpec((tm, k), lambda i, j: (i, 0)), _weight_rows_spec(k, tn, row_lo)],
        out_specs=pl.BlockSpec((tm, tn), lambda i, j: (i, j)),
        out_shape=jax.ShapeDtypeStruct((m, n), out_dtype),
        compiler_params=_compiler_params(("parallel", "arbitrary")),
        name=name,
    )(a, w_t)


def _prenorm_kernel(x_ref, nw_ref, sc_ref, sh_ref, h_ref):
    x = x_ref[...]
    y = x * lax.rsqrt(jnp.mean(x * x, axis=-1, keepdims=True) + EPS) * nw_ref[...]
    h_ref[...] = (y * (1.0 + sc_ref[...]) + sh_ref[...]).astype(h_ref.dtype)


def prenorm_modulate(x, nw, sc, sh, *, tm=512):
    t, d = x.shape
    vec = pl.BlockSpec((1, d), lambda i: (0, 0))
    return pl.pallas_call(
        _prenorm_kernel,
        grid=(t // tm,),
        in_specs=[pl.BlockSpec((tm, d), lambda i: (i, 0)), vec, vec, vec],
        out_specs=pl.BlockSpec((tm, d), lambda i: (i, 0)),
        out_shape=jax.ShapeDtypeStruct((t, d), BF16),
        compiler_params=_compiler_params(("parallel",)),
        name="prenorm_modulate",
    )(x, nw, sc, sh)


def _mm_residue_kernel(a_ref, w_ref, o_ref, res_ref):
    d = o_ref.shape[0]
    rows = o_ref.shape[1]
    res = lax.dot_general(a_ref[...], w_ref[...].astype(BF16), _NT, preferred_element_type=F32)
    if d == 1:
        o_ref[0] = res.astype(o_ref.dtype)
        return
    nslab = res.shape[1] // LANES
    for s in range(nslab):
        res_ref[s] = res[:, s * LANES:(s + 1) * LANES]
    for r in range(d):
        for s in range(nslab):
            o_ref[r, :, s * LANES:(s + 1) * LANES] = res_ref[s, pl.ds(r, rows, stride=d), :].astype(o_ref.dtype)


def project_residue_major(a, w_t, *, dilation, row_lo, row_hi, tm, tn, name):
    m, k = a.shape
    n = row_hi - row_lo
    assert m % tm == 0 and n % tn == 0 and tm % (dilation * 16) == 0
    rows = tm // dilation
    return pl.pallas_call(
        _mm_residue_kernel,
        grid=(m // tm, n // tn),
        in_specs=[pl.BlockSpec((tm, k), lambda i, j: (i, 0)), _weight_rows_spec(k, tn, row_lo)],
        out_specs=pl.BlockSpec((dilation, rows, tn), lambda i, j: (0, i, j)),
        out_shape=jax.ShapeDtypeStruct((dilation, m // dilation, n), BF16),
        scratch_shapes=[pltpu.VMEM((tn // LANES, tm, LANES), F32)],
        compiler_params=_compiler_params(("parallel", "arbitrary")),
        name=name,
    )(a, w_t)


_HEAD_BATCH = ((0,), (0,))
_BMM = (((2,), (1,)), _HEAD_BATCH)
_BMM_NT = (((2,), (2,)), _HEAD_BATCH)
_BMM_TN = (((1,), (1,)), _HEAD_BATCH)


def _bdot(a, b, dims=_BMM):
    return lax.dot_general(a.astype(BF16), b.astype(BF16), dims, preferred_element_type=F32)


def _unit_lower_inverse(low, row, col):
    c = low.shape[-1]
    s = DN_INV_BASE
    same = (row // s) == (col // s)
    ld = jnp.where(same, low, 0.0)
    x = jnp.where(row == col, 1.0, 0.0) - ld
    p = _bdot(ld, ld)
    steps = int(math.log2(s)) - 1
    for it in range(steps):
        x = x + _bdot(x, p)
        if it + 1 < steps:
            p = _bdot(p, p)
    while s < c:
        off = ((row // (2 * s)) == (col // (2 * s))) & ((row // s) != (col // s))
        b = jnp.where(off, low, 0.0)
        x = x - _bdot(_bdot(x, b), x)
        s *= 2
    return x


def _dn_kernel(q_ref, k_ref, v_ref, z_ref, cwq_ref, cwk_ref, cwv_ref, bcol_ref, gcol_ref, grow_ref, nw_ref,
               o_ref, state_ref, xe_ref):
    i = pl.program_id(1)
    c = q_ref.shape[0]
    gw = q_ref.shape[1]

    @pl.when(i == 0)
    def _():
        state_ref[...] = jnp.zeros_like(state_ref)
        xe_ref[0:CONV_PAD, :] = jnp.zeros((CONV_PAD, xe_ref.shape[1]), F32)

    xe_ref[CONV_PAD:CONV_PAD + c, 0:gw] = q_ref[...]
    xe_ref[CONV_PAD:CONV_PAD + c, gw:2 * gw] = k_ref[...]
    xe_ref[CONV_PAD:CONV_PAD + c, 2 * gw:3 * gw] = v_ref[...]

    def conv_silu(lo, w_ref):
        acc = w_ref[DN_CONV - 1:DN_CONV, :] * xe_ref[CONV_PAD:CONV_PAD + c, lo:lo + gw]
        for j in range(DN_CONV - 1):
            r0 = CONV_PAD - (DN_CONV - 1) + j
            acc = acc + w_ref[j:j + 1, :] * xe_ref[r0:r0 + c, lo:lo + gw]
        return acc * jax.nn.sigmoid(acc)

    hps = gw // HEAD_DIM

    def heads(x):
        return jnp.stack([x[:, j * HEAD_DIM:(j + 1) * HEAD_DIM] for j in range(hps)])

    qh = heads(conv_silu(0, cwq_ref))
    kh = heads(conv_silu(gw, cwk_ref))
    vh = heads(conv_silu(2 * gw, cwv_ref))
    xe_ref[0:CONV_PAD, :] = xe_ref[c:c + CONV_PAD, :]

    row = lax.broadcasted_iota(I32, (c, c), 0)
    col = lax.broadcasted_iota(I32, (c, c), 1)
    ge = row >= col
    gt = row > col
    bcols, gcols, grows = bcol_ref[0], gcol_ref[0], grow_ref[0]
    bcol = jnp.stack([bcols[:, j:j + 1] for j in range(hps)])
    gcol = jnp.stack([gcols[:, j:j + 1] for j in range(hps)])
    grow = jnp.stack([grows[j:j + 1, :] for j in range(hps)])

    qn = qh * (lax.rsqrt(jnp.sum(qh * qh, axis=-1, keepdims=True) + EPS) * (HEAD_DIM ** -0.5))
    kn = kh * lax.rsqrt(jnp.sum(kh * kh, axis=-1, keepdims=True) + EPS)
    diff = gcol - grow
    decay = jnp.where(ge, jnp.exp(jnp.where(ge, diff, 0.0)), 0.0)
    kb = kn * bcol
    kf = kn.astype(BF16)
    low = jnp.where(gt, _bdot(kb, kf, _BMM_NT) * decay, 0.0)
    attn = _bdot(qn, kf, _BMM_NT) * decay
    tinv = _unit_lower_inverse(low, row, col)
    eg = jnp.exp(gcol)
    rhs = jnp.concatenate([vh * bcol, kb * eg], axis=2)
    uw = _bdot(tinv, rhs)
    u, w = uw[:, :, :HEAD_DIM], uw[:, :, HEAD_DIM:]
    s = state_ref[...]
    sb = s.astype(BF16)
    v_new = u - _bdot(w, sb)
    vnb = v_new.astype(BF16)
    o = _bdot(qn * eg, sb) + _bdot(attn, vnb)
    glast = grow[:, :, c - 1:c]
    kdec = kn * jnp.exp(glast - gcol)
    state_ref[...] = s * jnp.exp(glast) + _bdot(kdec, vnb, _BMM_TN)
    on = o * lax.rsqrt(jnp.mean(o * o, axis=-1, keepdims=True) + EPS) * nw_ref[...]
    for j in range(hps):
        hs = slice(j * HEAD_DIM, (j + 1) * HEAD_DIM)
        zf = z_ref[:, hs]
        o_ref[:, hs] = (on[j] * (zf * jax.nn.sigmoid(zf))).astype(o_ref.dtype)


def _dn_gates(ba, a_log, dt_bias):
    t = ba.shape[0]
    hps = DN_HEADS_PER_STEP
    ng = DN_HEADS // hps
    beta = jax.nn.sigmoid(ba[:, :DN_HEADS])
    g = -jnp.exp(a_log) * jax.nn.softplus(ba[:, DN_HEADS:2 * DN_HEADS] + dt_bias)
    gc = jnp.cumsum(g.reshape(t // DN_TILE, DN_TILE, DN_HEADS), axis=1).reshape(t, DN_HEADS)
    bcol = beta.reshape(t, ng, hps).transpose(1, 0, 2)
    gcol = gc.reshape(t, ng, hps).transpose(1, 0, 2)
    grow = gc.reshape(t, ng, hps).transpose(1, 2, 0)
    return bcol, gcol, grow


def gated_deltanet(proj_dn, ba, conv_w, a_log, dt_bias, norm_w):
    t = proj_dn.shape[0]
    c = DN_TILE
    hps = DN_HEADS_PER_STEP
    gw = hps * HEAD_DIM
    ng = DN_HEADS // hps
    bcol, gcol, grow = _dn_gates(ba, a_log, dt_bias)
    nw = norm_w.reshape(1, HEAD_DIM).astype(F32)
    return pl.pallas_call(
        _dn_kernel,
        grid=(ng, t // c),
        in_specs=[pl.BlockSpec((c, gw), lambda g, i: (i, g)),
                  pl.BlockSpec((c, gw), lambda g, i: (i, ng + g)),
                  pl.BlockSpec((c, gw), lambda g, i: (i, 2 * ng + g)),
                  pl.BlockSpec((c, gw), lambda g, i: (i, 3 * ng + g)),
                  pl.BlockSpec((DN_CONV, gw), lambda g, i: (0, g)),
                  pl.BlockSpec((DN_CONV, gw), lambda g, i: (0, ng + g)),
                  pl.BlockSpec((DN_CONV, gw), lambda g, i: (0, 2 * ng + g)),
                  pl.BlockSpec((1, c, hps), lambda g, i: (g, i, 0)),
                  pl.BlockSpec((1, c, hps), lambda g, i: (g, i, 0)),
                  pl.BlockSpec((1, hps, c), lambda g, i: (g, 0, i)),
                  pl.BlockSpec((1, HEAD_DIM), lambda g, i: (0, 0))],
        out_specs=pl.BlockSpec((c, gw), lambda g, i: (i, g)),
        out_shape=jax.ShapeDtypeStruct((t, DN_VW), BF16),
        scratch_shapes=[pltpu.VMEM((hps, HEAD_DIM, HEAD_DIM), F32),
                        pltpu.VMEM((c + CONV_PAD, 3 * gw), F32)],
        compiler_params=_compiler_params(("parallel", "arbitrary")),
        name="gated_deltanet",
    )(proj_dn, proj_dn, proj_dn, proj_dn, conv_w, conv_w, conv_w, bcol, gcol, grow, nw)


def _swa_kernel(q_ref, kp_ref, kc_ref, vp_ref, vc_ref, bp_ref, bc_ref, o_ref, lse_ref):
    n = pl.program_id(1)
    blk = SWA_BLOCK
    qi = lax.broadcasted_iota(I32, (blk, blk), 0)
    kj = lax.broadcasted_iota(I32, (blk, blk), 1)
    valid_prev = (kj >= qi) & (n > 0)
    valid_cur = kj <= qi
    scale = HEAD_DIM ** -0.5
    nh = SWA_HEADS_PER_GROUP

    def heads(ref):
        return jnp.stack([ref[:, h * HEAD_DIM:(h + 1) * HEAD_DIM] for h in range(nh)])

    q = heads(q_ref)
    sp = lax.dot_general(q, heads(kp_ref), _BMM_NT, preferred_element_type=F32) * scale + bp_ref[...]
    sc = lax.dot_general(q, heads(kc_ref), _BMM_NT, preferred_element_type=F32) * scale + bc_ref[...]
    sp = jnp.where(valid_prev, sp, -jnp.inf)
    sc = jnp.where(valid_cur, sc, -jnp.inf)
    m = jnp.maximum(jnp.max(sp, axis=-1, keepdims=True), jnp.max(sc, axis=-1, keepdims=True))
    pp = jnp.exp(sp - m)
    pc = jnp.exp(sc - m)
    den = jnp.sum(pp, axis=-1, keepdims=True) + jnp.sum(pc, axis=-1, keepdims=True)
    o = (lax.dot_general(pp.astype(BF16), heads(vp_ref), _BMM, preferred_element_type=F32)
         + lax.dot_general(pc.astype(BF16), heads(vc_ref), _BMM, preferred_element_type=F32))
    o = o / den
    lse = m + jnp.log(den)
    lane = lax.broadcasted_iota(I32, (blk, HEAD_DIM), 1)
    lse_all = jnp.zeros((blk, HEAD_DIM), F32)
    for h in range(nh):
        o_ref[:, h * HEAD_DIM:(h + 1) * HEAD_DIM] = o[h].astype(o_ref.dtype)
        lse_all = jnp.where(lane == h, lse[h], lse_all)
    lse_ref[...] = lse_all


def _t5_causal_bucket(dist):
    max_exact = REL_BUCKETS // 2
    d = jnp.maximum(dist, 0)
    log_ratio = jnp.log(jnp.maximum(d, 1).astype(F32) / max_exact) / math.log(REL_MAX_DIST / max_exact)
    large = jnp.minimum(max_exact + (log_ratio * (REL_BUCKETS - max_exact)).astype(I32), REL_BUCKETS - 1)
    return jnp.where(d < max_exact, d, large)


def _swa_bias_blocks(rel_bias_g, dilation):
    blk = SWA_BLOCK
    qi = jnp.arange(blk, dtype=I32)[:, None]
    kj = jnp.arange(blk, dtype=I32)[None, :]
    table = rel_bias_g.astype(F32)

    def lookup(dist):
        onehot = jax.nn.one_hot(_t5_causal_bucket(dist), REL_BUCKETS, dtype=F32)
        return jnp.einsum('qkb,bh->hqk', onehot, table, precision=lax.Precision.HIGHEST)

    return lookup((qi + blk - kj) * dilation), lookup((qi - kj) * dilation)


def swa_group(qkv, rel_bias_g, dilation):
    d, length, _ = qkv.shape
    blk = SWA_BLOCK
    nb = length // blk
    bp, bc = _swa_bias_blocks(rel_bias_g, dilation)

    def prev(n):
        return jnp.maximum(n - 1, 0)

    def blk3(f):
        return pl.BlockSpec((None, blk, SWA_W), f)

    return pl.pallas_call(
        _swa_kernel,
        grid=(d, nb),
        in_specs=[blk3(lambda r, n: (r, n, 0)),
                  blk3(lambda r, n: (r, prev(n), 1)), blk3(lambda r, n: (r, n, 1)),
                  blk3(lambda r, n: (r, prev(n), 2)), blk3(lambda r, n: (r, n, 2)),
                  pl.BlockSpec((SWA_HEADS_PER_GROUP, blk, blk), lambda r, n: (0, 0, 0)),
                  pl.BlockSpec((SWA_HEADS_PER_GROUP, blk, blk), lambda r, n: (0, 0, 0))],
        out_specs=[blk3(lambda r, n: (r, n, 0)),
                   pl.BlockSpec((None, blk, HEAD_DIM), lambda r, n: (r, n, 0))],
        out_shape=[jax.ShapeDtypeStruct((d, length, SWA_W), BF16),
                   jax.ShapeDtypeStruct((d, length, HEAD_DIM), F32)],
        compiler_params=_compiler_params(("parallel", "arbitrary")),
        name=f"swa_d{dilation}",
    )(qkv, qkv, qkv, qkv, qkv, bp, bc)


def _mix_kernel(odn_ref, o1_ref, o4_ref, o16_ref, l1_ref, l4_ref, l16_ref, gd_ref, gs_ref, wd_ref, ws_ref,
                out_ref, a_ref, oscr_ref, lscr_ref):
    j = pl.program_id(1)
    tm = odn_ref.shape[0]
    groups = ((o1_ref, l1_ref), (o4_ref, l4_ref), (o16_ref, l16_ref))

    @pl.when(j == 0)
    def _():
        for g, (o_ref, l_ref) in enumerate(groups):
            d = o_ref.shape[0]
            rows = tm // d
            for r in range(d):
                if d == 1:
                    lscr_ref[g] = l_ref[0]
                else:
                    lscr_ref[g, pl.ds(r, rows, stride=d), :] = l_ref[r]
                for s in range(SWA_HEADS_PER_GROUP):
                    blk = o_ref[r, :, s * HEAD_DIM:(s + 1) * HEAD_DIM].astype(F32)
                    if d == 1:
                        oscr_ref[g, s] = blk
                    else:
                        oscr_ref[g, s, pl.ds(r, rows, stride=d), :] = blk
        ls = [lscr_ref[g] for g in range(len(groups))]
        m = jnp.maximum(jnp.maximum(ls[0], ls[1]), ls[2])
        es = [jnp.exp(l - m) for l in ls]
        inv = 1.0 / (es[0] + es[1] + es[2])
        for s in range(SWA_HEADS_PER_GROUP):
            acc = jnp.zeros((tm, HEAD_DIM), F32)
            for g in range(len(groups)):
                acc = acc + (es[g] * inv)[:, s:s + 1] * oscr_ref[g, s]
            a_ref[:, s * HEAD_DIM:(s + 1) * HEAD_DIM] = acc.astype(a_ref.dtype)

    y_dn = jnp.dot(odn_ref[...], wd_ref[...].astype(BF16), preferred_element_type=F32)
    y_swa = jnp.dot(a_ref[...], ws_ref[...].astype(BF16), preferred_element_type=F32)
    mixed = jax.nn.sigmoid(gd_ref[...].astype(F32)) * y_dn + jax.nn.sigmoid(gs_ref[...].astype(F32)) * y_swa
    out_ref[...] = mixed.astype(out_ref.dtype)


def mix_up(o_dn, swa_os, swa_ls, gates, w_up_dn, w_up_swa, *, tm=512, tn=512):
    t = o_dn.shape[0]
    dm = w_up_dn.shape[1]
    nj = dm // tn
    in_specs = [pl.BlockSpec((tm, o_dn.shape[1]), lambda i, j: (i, 0))]
    for o in swa_os:
        d = o.shape[0]
        in_specs.append(pl.BlockSpec((d, tm // d, SWA_W), lambda i, j: (0, i, 0)))
    for l in swa_ls:
        d = l.shape[0]
        in_specs.append(pl.BlockSpec((d, tm // d, HEAD_DIM), lambda i, j: (0, i, 0)))
    in_specs += [pl.BlockSpec((tm, tn), lambda i, j: (i, j)),
                 pl.BlockSpec((tm, tn), lambda i, j: (i, nj + j)),
                 pl.BlockSpec((w_up_dn.shape[0], tn), lambda i, j: (0, j)),
                 pl.BlockSpec((w_up_swa.shape[0], tn), lambda i, j: (0, j))]
    ng = len(swa_os)
    return pl.pallas_call(
        _mix_kernel,
        grid=(t // tm, nj),
        in_specs=in_specs,
        out_specs=pl.BlockSpec((tm, tn), lambda i, j: (i, j)),
        out_shape=jax.ShapeDtypeStruct((t, dm), BF16),
        scratch_shapes=[pltpu.VMEM((tm, SWA_W), BF16),
                        pltpu.VMEM((ng, SWA_HEADS_PER_GROUP, tm, HEAD_DIM), F32),
                        pltpu.VMEM((ng, tm, HEAD_DIM), F32)],
        compiler_params=_compiler_params(("parallel", "arbitrary")),
        name="mix_up",
    )(o_dn, *swa_os, *swa_ls, gates, gates, w_up_dn, w_up_swa)


def _route_kernel(o_ref, x_ref, gt1_ref, np1_ref, np2_ref, sc2_ref, sh2_ref, wr_ref, rb_ref,
                  x1_ref, h2b_ref, h2p_ref, idx_ref, gate_ref):
    o = o_ref[...]
    x1 = x_ref[...] + gt1_ref[...] * (o * lax.rsqrt(jnp.mean(o * o, axis=-1, keepdims=True) + EPS) * np1_ref[...])
    x1_ref[...] = x1
    h2 = (x1 * lax.rsqrt(jnp.mean(x1 * x1, axis=-1, keepdims=True) + EPS) * np2_ref[...]) * (1.0 + sc2_ref[...]) \
        + sh2_ref[...]
    h2b_ref[...] = h2.astype(BF16)
    half = h2.shape[1] // 2
    h2p_ref[...] = _pack_bf16_pair(h2[:, :half], h2[:, half:])

    tm = h2.shape[0]
    e, g, w = N_EXPERTS, N_EXPERT_GROUPS, N_EXPERTS // N_EXPERT_GROUPS
    logits = lax.dot_general(wr_ref[...], h2, _NT, precision=lax.Precision.HIGHEST,
                             preferred_element_type=F32)
    scores = jax.nn.sigmoid(logits)
    sel = (scores + rb_ref[...]).reshape(g, w, tm)
    scores3 = scores.reshape(g, w, tm)
    neg = -jnp.inf
    wi = lax.broadcasted_iota(I32, (g, w, tm), 1)
    gi = lax.broadcasted_iota(I32, (g, w, tm), 0)
    m1 = jnp.max(sel, axis=1, keepdims=True)
    first = jnp.min(jnp.where(sel == m1, wi, w), axis=1, keepdims=True)
    m2 = jnp.max(jnp.where(wi == first, neg, sel), axis=1, keepdims=True)
    grp = m1 + m2
    gi1 = lax.broadcasted_iota(I32, (g, 1, tm), 0)
    chosen = jnp.zeros((g, 1, tm), F32)
    for _ in range(TOPK_GROUPS):
        m = jnp.max(grp, axis=0, keepdims=True)
        fg = jnp.min(jnp.where(grp == m, gi1, g), axis=0, keepdims=True)
        hit = gi1 == fg
        chosen = jnp.where(hit, 1.0, chosen)
        grp = jnp.where(hit, neg, grp)
    cand = jnp.where(chosen > 0.5, sel, neg)
    ei = gi * w + wi
    idx_rows, sc_rows = [], []
    for _ in range(TOP_K):
        m = jnp.max(jnp.max(cand, axis=0, keepdims=True), axis=1, keepdims=True)
        fe = jnp.min(jnp.min(jnp.where(cand == m, ei, e), axis=0, keepdims=True), axis=1, keepdims=True)
        hit = ei == fe
        sc = jnp.sum(jnp.sum(jnp.where(hit, scores3, 0.0), axis=0, keepdims=True), axis=1, keepdims=True)
        cand = jnp.where(hit, neg, cand)
        idx_rows.append(fe.reshape(1, tm))
        sc_rows.append(sc.reshape(1, tm))
    idx = jnp.concatenate(idx_rows, axis=0)
    sc = jnp.concatenate(sc_rows, axis=0)
    idx_ref[...] = idx
    gate_ref[...] = sc / jnp.sum(sc, axis=0, keepdims=True) * ROUTED_SCALE


def moe_route(o, x, gt1, np1, np2, sc2, sh2, w_router, router_bias):
    t, d = x.shape
    tm = ROUTE_TILE
    wr_t = w_router.T.astype(F32)
    rb = router_bias.reshape(N_EXPERTS, 1).astype(F32)

    def row(i):
        return (i, 0)

    def fixed(i):
        return (0, 0)

    vec = pl.BlockSpec((1, d), fixed)
    return pl.pallas_call(
        _route_kernel,
        grid=(t // tm,),
        in_specs=[pl.BlockSpec((tm, d), row), pl.BlockSpec((tm, d), row), vec, vec, vec, vec, vec,
                  pl.BlockSpec((N_EXPERTS, d), fixed), pl.BlockSpec((N_EXPERTS, 1), fixed)],
        out_specs=[pl.BlockSpec((tm, d), row), pl.BlockSpec((tm, d), row), pl.BlockSpec((tm, d // 2), row),
                   pl.BlockSpec((TOP_K, tm), lambda i: (0, i)), pl.BlockSpec((TOP_K, tm), lambda i: (0, i))],
        out_shape=[jax.ShapeDtypeStruct((t, d), F32), jax.ShapeDtypeStruct((t, d), BF16),
                   jax.ShapeDtypeStruct((t, d // 2), U32),
                   jax.ShapeDtypeStruct((TOP_K, t), I32), jax.ShapeDtypeStruct((TOP_K, t), F32)],
        compiler_params=_compiler_params(("parallel",)),
        name="moe_route",
    )(o, x, gt1, np1, np2, sc2, sh2, wr_t, rb)


def _plan_kernel(idx_ref, dest_ref, blk_e_ref, pad_end_ref, nused_ref, counts_ref, carry_ref, pstart_ref):
    p = pl.program_id(0)
    i = pl.program_id(1)
    e = N_EXPERTS
    tn = idx_ref.shape[1]
    idx = idx_ref[...]
    ei = lax.broadcasted_iota(I32, (e, tn), 0)
    member = jnp.zeros((e, tn), F32)
    for k in range(TOP_K):
        member = member + jnp.where(idx[k:k + 1, :] == ei, 1.0, 0.0)
    tile_counts = jnp.sum(member, axis=1, keepdims=True)

    @pl.when((p == 0) & (i == 0))
    def _():
        counts_ref[...] = jnp.zeros_like(counts_ref)

    @pl.when(p == 0)
    def _():
        counts_ref[...] += tile_counts

    @pl.when((p == 1) & (i == 0))
    def _():
        counts = counts_ref[...]
        padded = jnp.ceil(counts * (1.0 / MOE_BLOCK)) * MOE_BLOCK
        sub = lax.broadcasted_iota(I32, (e, e), 0)
        lan = lax.broadcasted_iota(I32, (e, e), 1)
        start_row = jnp.sum(jnp.where(sub < lan, padded, 0.0), axis=0, keepdims=True)
        end_row = jnp.sum(jnp.where(sub <= lan, padded, 0.0), axis=0, keepdims=True)
        pstart_ref[...] = jnp.sum(jnp.where(sub == lan, start_row, 0.0), axis=1, keepdims=True)
        carry_ref[...] = jnp.zeros_like(carry_ref)
        pad_end_ref[...] = end_row.astype(I32)
        nb = blk_e_ref.shape[0]
        bstart = (lax.broadcasted_iota(I32, (nb, e), 0) * MOE_BLOCK).astype(F32)
        below = jnp.sum(jnp.where(end_row <= bstart, 1.0, 0.0), axis=1, keepdims=True)
        blk_e_ref[...] = jnp.minimum(below, e - 1.0).astype(I32)
        nused_ref[...] = (end_row[:, e - 1:e] * (1.0 / MOE_BLOCK)).astype(I32)

    @pl.when(p == 1)
    def _():
        tt = lax.broadcasted_iota(I32, (tn, tn), 0)
        tc = lax.broadcasted_iota(I32, (tn, tn), 1)
        before = jnp.where(tt < tc, 1.0, 0.0).astype(BF16)
        prefix = jnp.dot(member.astype(BF16), before, preferred_element_type=F32)
        base = prefix + carry_ref[...] + pstart_ref[...]
        rows = []
        for k in range(TOP_K):
            rows.append(jnp.sum(jnp.where(idx[k:k + 1, :] == ei, base, 0.0), axis=0, keepdims=True))
        dest_ref[...] = jnp.concatenate(rows, axis=0).astype(I32)
        carry_ref[...] += tile_counts


def moe_plan(idx, nb):
    k, t = idx.shape
    tn = PLAN_TILE
    e = N_EXPERTS
    return pl.pallas_call(
        _plan_kernel,
        grid=(2, t // tn),
        in_specs=[pl.BlockSpec((k, tn), lambda p, i: (0, i))],
        out_specs=[pl.BlockSpec((k, tn), lambda p, i: (0, i * p)),
                   pl.BlockSpec((nb, 1), lambda p, i: (0, 0)),
                   pl.BlockSpec((1, e), lambda p, i: (0, 0)),
                   pl.BlockSpec((1, 1), lambda p, i: (0, 0))],
        out_shape=[jax.ShapeDtypeStruct((k, t), I32), jax.ShapeDtypeStruct((nb, 1), I32),
                   jax.ShapeDtypeStruct((1, e), I32), jax.ShapeDtypeStruct((1, 1), I32)],
        scratch_shapes=[pltpu.VMEM((e, 1), F32), pltpu.VMEM((e, 1), F32), pltpu.VMEM((e, 1), F32)],
        compiler_params=_compiler_params(("arbitrary", "arbitrary")),
        name="moe_plan",
    )(idx)


def _dispatch_kernel(dest_ref, pend_ref, h_ref, xs_hbm, zero_ref, sem):
    i = pl.program_id(0)
    tm = h_ref.shape[0]

    def block_clear(b):
        start = pl.multiple_of(b * MOE_BLOCK, MOE_BLOCK)
        return pltpu.make_async_copy(zero_ref, xs_hbm.at[pl.ds(start, MOE_BLOCK), :], sem.at[1])

    @pl.when(i == 0)
    def _():
        zero_ref[...] = jnp.zeros_like(zero_ref)
        shift = int(math.log2(MOE_BLOCK))
        n_blocks = xs_hbm.shape[0] // MOE_BLOCK
        first_unused = lax.shift_right_logical(pend_ref[N_EXPERTS - 1], shift)

        def last_block(e):
            return lax.shift_right_logical(jnp.maximum(pend_ref[e] - MOE_BLOCK, 0), shift)

        def zstart(e, c):
            block_clear(last_block(e)).start()
            return c
        lax.fori_loop(0, N_EXPERTS, zstart, 0)

        def tstart(b, c):
            block_clear(b).start()
            return c
        lax.fori_loop(first_unused, n_blocks, tstart, 0)

        def zwait(e, c):
            block_clear(0).wait()
            return c
        lax.fori_loop(0, N_EXPERTS, zwait, 0)
        lax.fori_loop(first_unused, n_blocks, zwait, 0)

    def row_copy(t, row):
        return pltpu.make_async_copy(h_ref.at[pl.ds(t, 1), :], xs_hbm.at[pl.ds(row, 1), :], sem.at[0])

    def start(t, c):
        for k in range(TOP_K):
            row_copy(t, dest_ref[t * TOP_K + k]).start()
        return c
    lax.fori_loop(0, tm, start, 0)

    tile_rows = xs_hbm.at[pl.ds(0, TOP_K * tm), :]
    pltpu.make_async_copy(tile_rows, tile_rows, sem.at[0]).wait()


def moe_dispatch(h_packed, dest_tk, pad_end, rows):
    t, w = h_packed.shape
    tm = DISPATCH_TILE
    dflat = dest_tk.reshape(t * TOP_K)
    return pl.pallas_call(
        _dispatch_kernel,
        grid=(t // tm,),
        in_specs=[pl.BlockSpec((tm * TOP_K,), lambda i: (i,), memory_space=pltpu.SMEM),
                  pl.BlockSpec((N_EXPERTS,), lambda i: (0,), memory_space=pltpu.SMEM),
                  pl.BlockSpec((tm, w), lambda i: (i, 0))],
        out_specs=pl.BlockSpec(memory_space=pl.ANY),
        out_shape=jax.ShapeDtypeStruct((rows, w), U32),
        scratch_shapes=[pltpu.VMEM((MOE_BLOCK, w), U32), pltpu.SemaphoreType.DMA((2,))],
        compiler_params=_compiler_params(("arbitrary",)),
        name="moe_dispatch",
    )(dflat, pad_end.reshape(N_EXPERTS), h_packed)


def _expert_changed(blk_e_ref, i):
    prev = blk_e_ref[jnp.maximum(i - 1, 0)]
    return (i == 0) | (blk_e_ref[i] != prev)


def _gmm1_kernel(blk_e_ref, next_e_ref, nused_ref, x_ref, w1_hbm, w3_hbm, h_ref, st1_ref, st3_ref, w1b_ref, w3b_ref,
                 sem):
    i = pl.program_id(0)

    def weight_copies(e):
        return (pltpu.make_async_copy(w1_hbm.at[e], st1_ref, sem.at[0]),
                pltpu.make_async_copy(w3_hbm.at[e], st3_ref, sem.at[1]))

    @pl.when(i == 0)
    def _():
        for cp in weight_copies(blk_e_ref[0]):
            cp.start()

    @pl.when(i < nused_ref[0])
    def _():
        @pl.when(_expert_changed(blk_e_ref, i))
        def _():
            for cp in weight_copies(blk_e_ref[i]):
                cp.wait()
            w1b_ref[...] = st1_ref[...].astype(BF16)
            w3b_ref[...] = st3_ref[...].astype(BF16)

            @pl.when(next_e_ref[i] >= 0)
            def _():
                for cp in weight_copies(next_e_ref[i]):
                    cp.start()

        lo, hi = _unpack_bf16_pair(x_ref[...])
        lo = lo.astype(BF16)
        hi = hi.astype(BF16)
        half = lo.shape[1]
        g = (jnp.dot(lo, w1b_ref[:half, :], preferred_element_type=F32)
             + jnp.dot(hi, w1b_ref[half:, :], preferred_element_type=F32))
        u = (jnp.dot(lo, w3b_ref[:half, :], preferred_element_type=F32)
             + jnp.dot(hi, w3b_ref[half:, :], preferred_element_type=F32))
        h_ref[...] = (g * jax.nn.sigmoid(g) * u).astype(h_ref.dtype)

    @pl.when(i >= nused_ref[0])
    def _():
        h_ref[...] = jnp.zeros_like(h_ref)


def _next_expert_table(blk_e, n_used):
    nb = blk_e.shape[0]
    pos = jnp.arange(nb, dtype=I32)
    prev = jnp.concatenate([blk_e[:1], blk_e[:-1]])
    starts = (blk_e != prev) & (pos < n_used[0]) & (pos > 0)
    start_pos = jnp.where(starts, pos, nb)
    shifted = jnp.concatenate([start_pos[1:], jnp.full((1,), nb, I32)])
    nxt = lax.cummin(shifted, axis=0, reverse=True)
    return jnp.where(nxt < nb, blk_e[jnp.minimum(nxt, nb - 1)], -1).astype(I32)


def grouped_gate_up(xs, blk_e, next_e, n_used, w1, w3):
    rows, half = xs.shape
    d = 2 * half
    nb = rows // MOE_BLOCK
    f = w1.shape[2]

    def row_blk(i, nu):
        return jnp.minimum(i, nu[0] - 1)

    grid_spec = pltpu.PrefetchScalarGridSpec(
        num_scalar_prefetch=3,
        grid=(nb,),
        in_specs=[pl.BlockSpec((MOE_BLOCK, half), lambda i, e, ne, nu: (row_blk(i, nu), 0)),
                  pl.BlockSpec(memory_space=pl.ANY),
                  pl.BlockSpec(memory_space=pl.ANY)],
        out_specs=pl.BlockSpec((MOE_BLOCK, f), lambda i, e, ne, nu: (i, 0)),
        scratch_shapes=[pltpu.VMEM((d, f), F32), pltpu.VMEM((d, f), F32),
                        pltpu.VMEM((d, f), BF16), pltpu.VMEM((d, f), BF16),
                        pltpu.SemaphoreType.DMA((2,))],
    )
    return pl.pallas_call(
        _gmm1_kernel,
        grid_spec=grid_spec,
        out_shape=jax.ShapeDtypeStruct((rows, f), BF16),
        compiler_params=_compiler_params(("arbitrary",)),
        name="moe_gate_up",
    )(blk_e, next_e, n_used, xs, w1, w3)


def _gmm2_kernel(blk_e_ref, next_e_ref, nused_ref, h_ref, w2_hbm, y_ref, st2_ref, w2b_ref, sem):
    i = pl.program_id(0)

    def weight_copy(e):
        return pltpu.make_async_copy(w2_hbm.at[e], st2_ref, sem.at[0])

    @pl.when(i == 0)
    def _():
        weight_copy(blk_e_ref[0]).start()

    @pl.when(i < nused_ref[0])
    def _():
        @pl.when(_expert_changed(blk_e_ref, i))
        def _():
            weight_copy(blk_e_ref[i]).wait()
            w2b_ref[...] = st2_ref[...].astype(BF16)

            @pl.when(next_e_ref[i] >= 0)
            def _():
                weight_copy(next_e_ref[i]).start()

        y = jnp.dot(h_ref[...], w2b_ref[...], preferred_element_type=F32)
        half = y.shape[1] // 2
        y_ref[...] = _pack_bf16_pair(y[:, :half], y[:, half:])

    @pl.when(i >= nused_ref[0])
    def _():
        y_ref[...] = jnp.zeros_like(y_ref)


def grouped_down(hs, blk_e, next_e, n_used, w2):
    rows, f = hs.shape
    nb = rows // MOE_BLOCK
    d = w2.shape[2]

    def row_blk(i, nu):
        return jnp.minimum(i, nu[0] - 1)

    grid_spec = pltpu.PrefetchScalarGridSpec(
        num_scalar_prefetch=3,
        grid=(nb,),
        in_specs=[pl.BlockSpec((MOE_BLOCK, f), lambda i, e, ne, nu: (row_blk(i, nu), 0)),
                  pl.BlockSpec(memory_space=pl.ANY)],
        out_specs=pl.BlockSpec((MOE_BLOCK, d // 2), lambda i, e, ne, nu: (i, 0)),
        scratch_shapes=[pltpu.VMEM((f, d), F32), pltpu.VMEM((f, d), BF16), pltpu.SemaphoreType.DMA((1,))],
    )
    return pl.pallas_call(
        _gmm2_kernel,
        grid_spec=grid_spec,
        out_shape=jax.ShapeDtypeStruct((rows, d // 2), U32),
        compiler_params=_compiler_params(("arbitrary",)),
        name="moe_down",
    )(blk_e, next_e, n_used, hs, w2)


def _combine_kernel(dcur_ref, dnext_ref, y_hbm, gate_ref, shared_ref, x_ref, gt_ref, nw_ref, o_ref, gbuf, sem):
    i = pl.program_id(0)
    n = pl.num_programs(0)
    tm = x_ref.shape[0]
    slot = lax.rem(i, 2)

    def row_copy(s, src_row, dst_row):
        return pltpu.make_async_copy(y_hbm.at[pl.ds(src_row, 1), :], gbuf.at[s, pl.ds(dst_row, 1), :], sem.at[s])

    def start_tile(d_ref, s):
        def body(t, carry):
            for k in range(TOP_K):
                row_copy(s, d_ref[t * TOP_K + k], k * tm + t).start()
            return carry
        lax.fori_loop(0, tm, body, 0)

    @pl.when(i == 0)
    def _():
        start_tile(dcur_ref, 0)

    @pl.when(i + 1 < n)
    def _():
        start_tile(dnext_ref, 1 - slot)

    pltpu.make_async_copy(y_hbm.at[pl.ds(0, TOP_K * tm), :], gbuf.at[slot], sem.at[slot]).wait()

    gate = gate_ref[...]
    half = gbuf.shape[2]
    acc_lo = shared_ref[:, :half].astype(F32)
    acc_hi = shared_ref[:, half:].astype(F32)
    for k in range(TOP_K):
        lo, hi = _unpack_bf16_pair(gbuf[slot, k * tm:(k + 1) * tm, :])
        gk = gate[:, k:k + 1]
        acc_lo = acc_lo + gk * lo
        acc_hi = acc_hi + gk * hi
    ms = (jnp.sum(acc_lo * acc_lo, axis=-1, keepdims=True) + jnp.sum(acc_hi * acc_hi, axis=-1, keepdims=True)) \
        * (1.0 / (2 * half))
    r = lax.rsqrt(ms + EPS)
    o_ref[:, :half] = x_ref[:, :half] + gt_ref[:, :half] * (acc_lo * r * nw_ref[:, :half])
    o_ref[:, half:] = x_ref[:, half:] + gt_ref[:, half:] * (acc_hi * r * nw_ref[:, half:])


def moe_combine(ys, dest_tk, gate_tk, shared, x, gt, nw):
    t, d = x.shape
    tm = COMBINE_TILE
    nt = t // tm
    dflat = dest_tk.reshape(t * TOP_K)
    return pl.pallas_call(
        _combine_kernel,
        grid=(nt,),
        in_specs=[pl.BlockSpec((tm * TOP_K,), lambda i: (i,), memory_space=pltpu.SMEM),
                  pl.BlockSpec((tm * TOP_K,), lambda i: (jnp.minimum(i + 1, nt - 1),), memory_space=pltpu.SMEM),
                  pl.BlockSpec(memory_space=pl.ANY),
                  pl.BlockSpec((tm, TOP_K), lambda i: (i, 0)),
                  pl.BlockSpec((tm, d), lambda i: (i, 0)),
                  pl.BlockSpec((tm, d), lambda i: (i, 0)),
                  pl.BlockSpec((1, d), lambda i: (0, 0)),
                  pl.BlockSpec((1, d), lambda i: (0, 0))],
        out_specs=pl.BlockSpec((tm, d), lambda i: (i, 0)),
        out_shape=jax.ShapeDtypeStruct((t, d), F32),
        scratch_shapes=[pltpu.VMEM((2, TOP_K * tm, d // 2), U32), pltpu.SemaphoreType.DMA((2,))],
        compiler_params=_compiler_params(("arbitrary",)),
        name="moe_combine",
    )(dflat, dflat, ys, gate_tk, shared, x, gt, nw)


def kernel(x, c, w_mod, b_mod, norm_pre1, norm_post1, w_in, conv_w, a_log, dt_bias, dn_norm_w, w_up_dn,
           w_up_swa, w_out, rel_bias, norm_pre2, norm_post2, w_router, router_bias, w1, w3, w2, ws1, ws3, ws2):
    B, S, D = x.shape
    assert B == 1
    T = B * S
    x = x.reshape(T, D)
    depth = w_mod.shape[0]
    n_swa_cols = 3 * SWA_W
    nb = -(-(T * TOP_K) // MOE_BLOCK) + N_EXPERTS
    for layer in range(depth):
        sc = jnp.broadcast_to(jax.nn.silu(c), (8, D))
        mod = dense_matmul(sc, w_mod[layer], tm=8, tn=1024, out_dtype=F32, name="adaln_mod")[:B] + b_mod[layer]
        sh1, sc1, gt1, sh2, sc2, gt2 = jnp.split(mod, 6, axis=-1)

        def vec(v):
            return v[layer].reshape(1, D)

        h = prenorm_modulate(x, vec(norm_pre1), sc1, sh1)
        w_t = jnp.swapaxes(w_in[layer], 0, 1)
        proj_dn = dense_matmul_nt(h, w_t, row_lo=0, row_hi=COL_BA, tm=1024, tn=512, out_dtype=F32,
                                  name="in_proj_dn")
        ba = dense_matmul_nt(h, w_t, row_lo=COL_BA, row_hi=COL_BA + LANES, tm=1024, tn=LANES, out_dtype=F32,
                             name="in_proj_ba")
        gates = dense_matmul_nt(h, w_t, row_lo=COL_GATES, row_hi=COL_END, tm=1024, tn=512, out_dtype=BF16,
                                name="in_proj_gates")
        o_dn = gated_deltanet(proj_dn, ba, conv_w[layer], a_log[layer], dt_bias[layer], dn_norm_w[layer])
        swa_os, swa_ls = [], []
        for gi, (_, dilation) in enumerate(SWA_GROUPS):
            qkv = project_residue_major(h, w_t, dilation=dilation, row_lo=COL_SWA + gi * n_swa_cols,
                                        row_hi=COL_SWA + (gi + 1) * n_swa_cols, tm=1024, tn=512,
                                        name=f"in_proj_swa_d{dilation}")
            heads = slice(gi * SWA_HEADS_PER_GROUP, (gi + 1) * SWA_HEADS_PER_GROUP)
            o_g, lse_g = swa_group(qkv, rel_bias[:, heads], dilation)
            swa_os.append(o_g)
            swa_ls.append(lse_g)
        mixed = mix_up(o_dn, swa_os, swa_ls, gates, w_up_dn[layer], w_up_swa[layer])
        o = dense_matmul(mixed, w_out[layer], tm=1024, tn=512, out_dtype=F32, name="out_proj")
        x1, h2b, h2p, idx, gate = moe_route(o, x, gt1, vec(norm_post1), vec(norm_pre2), sc2, sh2,
                                            w_router[layer], router_bias[layer])
        dest, blk_e, pad_end, n_used = moe_plan(idx, nb)
        dest_tk = dest.T
        gate_tk = gate.T
        xs = moe_dispatch(h2p, dest_tk, pad_end, nb * MOE_BLOCK)
        blk_e = blk_e.reshape(nb)
        n_used = n_used.reshape(1)
        next_e = _next_expert_table(blk_e, n_used)
        hs = grouped_gate_up(xs, blk_e, next_e, n_used, w1[layer], w3[layer])
        ys = grouped_down(hs, blk_e, next_e, n_used, w2[layer])
        hsh = swiglu_up(h2b, ws1[layer], ws3[layer], tm=1024, tn=D_EXPERT // 2, name="shared_gate_up")
        shared = dense_matmul(hsh, ws2[layer], tm=1024, tn=512, out_dtype=F32, name="shared_down")
        x = moe_combine(ys, dest_tk, gate_tk, shared, x1, gt2, vec(norm_post2))
    return x.reshape(B, S, D)
```

```python
import math

import jax
import jax.numpy as jnp
from jax import lax
from jax.experimental import pallas as pl
from jax.experimental.pallas import tpu as pltpu

D_MODEL = 4096
HEAD_DIM = 128
EPS = 1e-6
DN_HEADS = 16
DN_CONV = 4
DN_QK = DN_HEADS * HEAD_DIM
DN_VW = DN_HEADS * HEAD_DIM
SWA_GROUPS = ((128, 1), (512, 4), (2048, 16))
SWA_HEADS_PER_GROUP = 8
SWA_BLOCK = 128
SWA_W = SWA_HEADS_PER_GROUP * HEAD_DIM
REL_BUCKETS = 32
REL_MAX_DIST = 2048
N_EXPERTS = 64
N_EXPERT_GROUPS = 8
TOPK_GROUPS = 4
TOP_K = 8
D_EXPERT = 768
ROUTED_SCALE = 2.5
MOE_BLOCK = 256

COL_BA = 4 * DN_QK
COL_SWA = COL_BA + 2 * DN_HEADS
COL_GATES = COL_SWA + 3 * len(SWA_GROUPS) * SWA_W
COL_END = COL_GATES + 2 * D_MODEL

LANES = 128
SUBLANES = 8
DMA_PRIORITIES = 2
VMEM_LIMIT_BYTES = 56 * 1024 * 1024

DN_TILE = 256
DN_HEADS_PER_STEP = 4
DN_INV_BASE = 16
CONV_PAD = 8
ROUTE_TILE = 256
PLAN_TILE = 512
DISPATCH_TILE = 128
COMBINE_TILE = 64

F32 = jnp.float32
BF16 = jnp.bfloat16
U32 = jnp.uint32
I32 = jnp.int32
HI_MASK = 0xFFFF0000

_NT = (((1,), (1,)), ((), ()))


def _compiler_params(semantics):
    return pltpu.CompilerParams(dimension_semantics=semantics, vmem_limit_bytes=VMEM_LIMIT_BYTES)


def _pack_bf16_pair(lo, hi):
    lo_bits = pltpu.bitcast(lo.astype(BF16).astype(F32), U32) >> jnp.uint32(16)
    hi_bits = pltpu.bitcast(hi.astype(BF16).astype(F32), U32)
    return lo_bits | hi_bits


def _unpack_bf16_pair(w):
    lo = pltpu.bitcast(w << jnp.uint32(16), F32)
    hi = pltpu.bitcast(w & jnp.uint32(HI_MASK), F32)
    return lo, hi


def _mm_kernel(a_ref, b_ref, o_ref):
    a = a_ref[...].astype(BF16)
    b = b_ref[...].astype(BF16)
    o_ref[...] = jnp.dot(a, b, preferred_element_type=F32).astype(o_ref.dtype)


def dense_matmul(a, b, *, tm, tn, out_dtype, name, col_lo=0, col_hi=None):
    m, k = a.shape
    col_hi = b.shape[1] if col_hi is None else col_hi
    n = col_hi - col_lo
    assert m % tm == 0 and col_lo % tn == 0
    off = col_lo // tn
    return pl.pallas_call(
        _mm_kernel,
        grid=(m // tm, pl.cdiv(n, tn)),
        in_specs=[pl.BlockSpec((tm, k), lambda i, j: (i, 0)),
                  pl.BlockSpec((k, tn), lambda i, j: (0, j + off))],
        out_specs=pl.BlockSpec((tm, tn), lambda i, j: (i, j)),
        out_shape=jax.ShapeDtypeStruct((m, n), out_dtype),
        compiler_params=_compiler_params(("parallel", "arbitrary")),
        name=name,
    )(a, b)


def _swiglu_up_kernel(a_ref, wg_ref, wu_ref, o_ref):
    a = a_ref[...]
    g = jnp.dot(a, wg_ref[...].astype(BF16), preferred_element_type=F32)
    u = jnp.dot(a, wu_ref[...].astype(BF16), preferred_element_type=F32)
    o_ref[...] = (g * jax.nn.sigmoid(g) * u).astype(o_ref.dtype)


def swiglu_up(a, wg, wu, *, tm, tn, name):
    m, k = a.shape
    n = wg.shape[1]
    assert m % tm == 0 and n % tn == 0
    return pl.pallas_call(
        _swiglu_up_kernel,
        grid=(m // tm, n // tn),
        in_specs=[pl.BlockSpec((tm, k), lambda i, j: (i, 0)),
                  pl.BlockSpec((k, tn), lambda i, j: (0, j)),
                  pl.BlockSpec((k, tn), lambda i, j: (0, j))],
        out_specs=pl.BlockSpec((tm, tn), lambda i, j: (i, j)),
        out_shape=jax.ShapeDtypeStruct((m, n), BF16),
        compiler_params=_compiler_params(("parallel", "arbitrary")),
        name=name,
    )(a, wg, wu)


def _weight_rows_spec(k, tn, row_lo):
    assert row_lo % SUBLANES == 0 and tn % SUBLANES == 0
    return pl.BlockSpec((pl.Element(tn), pl.Element(k)),
                        lambda i, j: ((row_lo // SUBLANES + j * (tn // SUBLANES)) * SUBLANES, 0))


def _mm_nt_kernel(a_ref, w_ref, o_ref):
    a = a_ref[...].astype(BF16)
    w = w_ref[...].astype(BF16)
    o_ref[...] = lax.dot_general(a, w, _NT, preferred_element_type=F32).astype(o_ref.dtype)


def dense_matmul_nt(a, w_t, *, row_lo, row_hi, tm, tn, out_dtype, name):
    m, k = a.shape
    n = row_hi - row_lo
    assert m % tm == 0 and n % tn == 0
    return pl.pallas_call(
        _mm_nt_kernel,
        grid=(m // tm, n // tn),
        in_specs=[pl.BlockSpec((tm, k), lambda i, j: (i, 0)), _weight_rows_spec(k, tn, row_lo)],
        out_specs=pl.BlockSpec((tm, tn), lambda i, j: (i, j)),
        out_shape=jax.ShapeDtypeStruct((m, n), out_dtype),
        compiler_params=_compiler_params(("parallel", "arbitrary")),
        name=name,
    )(a, w_t)


def _prenorm_kernel(x_ref, nw_ref, sc_ref, sh_ref, h_ref):
    x = x_ref[...]
    y = x * lax.rsqrt(jnp.mean(x * x, axis=-1, keepdims=True) + EPS) * nw_ref[...]
    h_ref[...] = (y * (1.0 + sc_ref[...]) + sh_ref[...]).astype(h_ref.dtype)


def prenorm_modulate(x, nw, sc, sh, *, tm=512):
    t, d = x.shape
    vec = pl.BlockSpec((1, d), lambda i: (0, 0))
    return pl.pallas_call(
        _prenorm_kernel,
        grid=(t // tm,),
        in_specs=[pl.BlockSpec((tm, d), lambda i: (i, 0)), vec, vec, vec],
        out_specs=pl.BlockSpec((tm, d), lambda i: (i, 0)),
        out_shape=jax.ShapeDtypeStruct((t, d), BF16),
        compiler_params=_compiler_params(("parallel",)),
        name="prenorm_modulate",
    )(x, nw, sc, sh)


def _mm_residue_kernel(a_ref, w_ref, o_ref, res_ref):
    d = o_ref.shape[0]
    rows = o_ref.shape[1]
    res = lax.dot_general(a_ref[...], w_ref[...].astype(BF16), _NT, preferred_element_type=F32)
    if d == 1:
        o_ref[0] = res.astype(o_ref.dtype)
        return
    nslab = res.shape[1] // LANES
    for s in range(nslab):
        res_ref[s] = res[:, s * LANES:(s + 1) * LANES]
    for r in range(d):
        for s in range(nslab):
            o_ref[r, :, s * LANES:(s + 1) * LANES] = res_ref[s, pl.ds(r, rows, stride=d), :].astype(o_ref.dtype)


def project_residue_major(a, w_t, *, dilation, row_lo, row_hi, tm, tn, name):
    m, k = a.shape
    n = row_hi - row_lo
    assert m % tm == 0 and n % tn == 0 and tm % (dilation * 16) == 0
    rows = tm // dilation
    return pl.pallas_call(
        _mm_residue_kernel,
        grid=(m // tm, n // tn),
        in_specs=[pl.BlockSpec((tm, k), lambda i, j: (i, 0)), _weight_rows_spec(k, tn, row_lo)],
        out_specs=pl.BlockSpec((dilation, rows, tn), lambda i, j: (0, i, j)),
        out_shape=jax.ShapeDtypeStruct((dilation, m // dilation, n), BF16),
        scratch_shapes=[pltpu.VMEM((tn // LANES, tm, LANES), F32)],
        compiler_params=_compiler_params(("parallel", "arbitrary")),
        name=name,
    )(a, w_t)


_HEAD_BATCH = ((0,), (0,))
_BMM = (((2,), (1,)), _HEAD_BATCH)
_BMM_NT = (((2,), (2,)), _HEAD_BATCH)
_BMM_TN = (((1,), (1,)), _HEAD_BATCH)


def _bdot(a, b, dims=_BMM):
    return lax.dot_general(a.astype(BF16), b.astype(BF16), dims, preferred_element_type=F32)


def _unit_lower_inverse(low, row, col):
    c = low.shape[-1]
    s = DN_INV_BASE
    same = (row // s) == (col // s)
    ld = jnp.where(same, low, 0.0)
    x = jnp.where(row == col, 1.0, 0.0) - ld
    p = _bdot(ld, ld)
    steps = int(math.log2(s)) - 1
    for it in range(steps):
        x = x + _bdot(x, p)
        if it + 1 < steps:
            p = _bdot(p, p)
    while s < c:
        off = ((row // (2 * s)) == (col // (2 * s))) & ((row // s) != (col // s))
        b = jnp.where(off, low, 0.0)
        x = x - _bdot(_bdot(x, b), x)
        s *= 2
    return x


def _dn_kernel(q_ref, k_ref, v_ref, z_ref, cwq_ref, cwk_ref, cwv_ref, bcol_ref, gcol_ref, grow_ref, nw_ref,
               o_ref, state_ref, xe_ref):
    i = pl.program_id(1)
    c = q_ref.shape[0]
    gw = q_ref.shape[1]

    @pl.when(i == 0)
    def _():
        state_ref[...] = jnp.zeros_like(state_ref)
        xe_ref[0:CONV_PAD, :] = jnp.zeros((CONV_PAD, xe_ref.shape[1]), F32)

    xe_ref[CONV_PAD:CONV_PAD + c, 0:gw] = q_ref[...]
    xe_ref[CONV_PAD:CONV_PAD + c, gw:2 * gw] = k_ref[...]
    xe_ref[CONV_PAD:CONV_PAD + c, 2 * gw:3 * gw] = v_ref[...]

    def conv_silu(lo, w_ref):
        acc = w_ref[DN_CONV - 1:DN_CONV, :] * xe_ref[CONV_PAD:CONV_PAD + c, lo:lo + gw]
        for j in range(DN_CONV - 1):
            r0 = CONV_PAD - (DN_CONV - 1) + j
            acc = acc + w_ref[j:j + 1, :] * xe_ref[r0:r0 + c, lo:lo + gw]
        return acc * jax.nn.sigmoid(acc)

    hps = gw // HEAD_DIM

    def heads(x):
        return jnp.stack([x[:, j * HEAD_DIM:(j + 1) * HEAD_DIM] for j in range(hps)])

    qh = heads(conv_silu(0, cwq_ref))
    kh = heads(conv_silu(gw, cwk_ref))
    vh = heads(conv_silu(2 * gw, cwv_ref))
    xe_ref[0:CONV_PAD, :] = xe_ref[c:c + CONV_PAD, :]

    row = lax.broadcasted_iota(I32, (c, c), 0)
    col = lax.broadcasted_iota(I32, (c, c), 1)
    ge = row >= col
    gt = row > col
    bcols, gcols, grows = bcol_ref[0], gcol_ref[0], grow_ref[0]
    bcol = jnp.stack([bcols[:, j:j + 1] for j in range(hps)])
    gcol = jnp.stack([gcols[:, j:j + 1] for j in range(hps)])
    grow = jnp.stack([grows[j:j + 1, :] for j in range(hps)])

    qn = qh * (lax.rsqrt(jnp.sum(qh * qh, axis=-1, keepdims=True) + EPS) * (HEAD_DIM ** -0.5))
    kn = kh * lax.rsqrt(jnp.sum(kh * kh, axis=-1, keepdims=True) + EPS)
    diff = gcol - grow
    decay = jnp.where(ge, jnp.exp(jnp.where(ge, diff, 0.0)), 0.0)
    kb = kn * bcol
    kf = kn.astype(BF16)
    low = jnp.where(gt, _bdot(kb, kf, _BMM_NT) * decay, 0.0)
    attn = _bdot(qn, kf, _BMM_NT) * decay
    tinv = _unit_lower_inverse(low, row, col)
    eg = jnp.exp(gcol)
    rhs = jnp.concatenate([vh * bcol, kb * eg], axis=2)
    uw = _bdot(tinv, rhs)
    u, w = uw[:, :, :HEAD_DIM], uw[:, :, HEAD_DIM:]
    s = state_ref[...]
    sb = s.astype(BF16)
    v_new = u - _bdot(w, sb)
    vnb = v_new.astype(BF16)
    o = _bdot(qn * eg, sb) + _bdot(attn, vnb)
    glast = grow[:, :, c - 1:c]
    kdec = kn * jnp.exp(glast - gcol)
    state_ref[...] = s * jnp.exp(glast) + _bdot(kdec, vnb, _BMM_TN)
    on = o * lax.rsqrt(jnp.mean(o * o, axis=-1, keepdims=True) + EPS) * nw_ref[...]
    for j in range(hps):
        hs = slice(j * HEAD_DIM, (j + 1) * HEAD_DIM)
        zf = z_ref[:, hs]
        o_ref[:, hs] = (on[j] * (zf * jax.nn.sigmoid(zf))).astype(o_ref.dtype)


def _dn_gates(ba, a_log, dt_bias):
    t = ba.shape[0]
    hps = DN_HEADS_PER_STEP
    ng = DN_HEADS // hps
    beta = jax.nn.sigmoid(ba[:, :DN_HEADS])
    g = -jnp.exp(a_log) * jax.nn.softplus(ba[:, DN_HEADS:2 * DN_HEADS] + dt_bias)
    gc = jnp.cumsum(g.reshape(t // DN_TILE, DN_TILE, DN_HEADS), axis=1).reshape(t, DN_HEADS)
    bcol = beta.reshape(t, ng, hps).transpose(1, 0, 2)
    gcol = gc.reshape(t, ng, hps).transpose(1, 0, 2)
    grow = gc.reshape(t, ng, hps).transpose(1, 2, 0)
    return bcol, gcol, grow


def gated_deltanet(proj_dn, ba, conv_w, a_log, dt_bias, norm_w):
    t = proj_dn.shape[0]
    c = DN_TILE
    hps = DN_HEADS_PER_STEP
    gw = hps * HEAD_DIM
    ng = DN_HEADS // hps
    bcol, gcol, grow = _dn_gates(ba, a_log, dt_bias)
    nw = norm_w.reshape(1, HEAD_DIM).astype(F32)
    return pl.pallas_call(
        _dn_kernel,
        grid=(ng, t // c),
        in_specs=[pl.BlockSpec((c, gw), lambda g, i: (i, g)),
                  pl.BlockSpec((c, gw), lambda g, i: (i, ng + g)),
                  pl.BlockSpec((c, gw), lambda g, i: (i, 2 * ng + g)),
                  pl.BlockSpec((c, gw), lambda g, i: (i, 3 * ng + g)),
                  pl.BlockSpec((DN_CONV, gw), lambda g, i: (0, g)),
                  pl.BlockSpec((DN_CONV, gw), lambda g, i: (0, ng + g)),
                  pl.BlockSpec((DN_CONV, gw), lambda g, i: (0, 2 * ng + g)),
                  pl.BlockSpec((1, c, hps), lambda g, i: (g, i, 0)),
                  pl.BlockSpec((1, c, hps), lambda g, i: (g, i, 0)),
                  pl.BlockSpec((1, hps, c), lambda g, i: (g, 0, i)),
                  pl.BlockSpec((1, HEAD_DIM), lambda g, i: (0, 0))],
        out_specs=pl.BlockSpec((c, gw), lambda g, i: (i, g)),
        out_shape=jax.ShapeDtypeStruct((t, DN_VW), BF16),
        scratch_shapes=[pltpu.VMEM((hps, HEAD_DIM, HEAD_DIM), F32),
                        pltpu.VMEM((c + CONV_PAD, 3 * gw), F32)],
        compiler_params=_compiler_params(("parallel", "arbitrary")),
        name="gated_deltanet",
    )(proj_dn, proj_dn, proj_dn, proj_dn, conv_w, conv_w, conv_w, bcol, gcol, grow, nw)


def _swa_kernel(q_ref, kp_ref, kc_ref, vp_ref, vc_ref, bp_ref, bc_ref, o_ref, lse_ref):
    n = pl.program_id(1)
    blk = SWA_BLOCK
    qi = lax.broadcasted_iota(I32, (blk, blk), 0)
    kj = lax.broadcasted_iota(I32, (blk, blk), 1)
    valid_prev = (kj >= qi) & (n > 0)
    valid_cur = kj <= qi
    scale = HEAD_DIM ** -0.5
    nh = SWA_HEADS_PER_GROUP

    def heads(ref):
        return jnp.stack([ref[:, h * HEAD_DIM:(h + 1) * HEAD_DIM] for h in range(nh)])

    q = heads(q_ref)
    sp = lax.dot_general(q, heads(kp_ref), _BMM_NT, preferred_element_type=F32) * scale + bp_ref[...]
    sc = lax.dot_general(q, heads(kc_ref), _BMM_NT, preferred_element_type=F32) * scale + bc_ref[...]
    sp = jnp.where(valid_prev, sp, -jnp.inf)
    sc = jnp.where(valid_cur, sc, -jnp.inf)
    m = jnp.maximum(jnp.max(sp, axis=-1, keepdims=True), jnp.max(sc, axis=-1, keepdims=True))
    pp = jnp.exp(sp - m)
    pc = jnp.exp(sc - m)
    den = jnp.sum(pp, axis=-1, keepdims=True) + jnp.sum(pc, axis=-1, keepdims=True)
    o = (lax.dot_general(pp.astype(BF16), heads(vp_ref), _BMM, preferred_element_type=F32)
         + lax.dot_general(pc.astype(BF16), heads(vc_ref), _BMM, preferred_element_type=F32))
    o = o / den
    lse = m + jnp.log(den)
    lane = lax.broadcasted_iota(I32, (blk, HEAD_DIM), 1)
    lse_all = jnp.zeros((blk, HEAD_DIM), F32)
    for h in range(nh):
        o_ref[:, h * HEAD_DIM:(h + 1) * HEAD_DIM] = o[h].astype(o_ref.dtype)
        lse_all = jnp.where(lane == h, lse[h], lse_all)
    lse_ref[...] = lse_all


def _t5_causal_bucket(dist):
    max_exact = REL_BUCKETS // 2
    d = jnp.maximum(dist, 0)
    log_ratio = jnp.log(jnp.maximum(d, 1).astype(F32) / max_exact) / math.log(REL_MAX_DIST / max_exact)
    large = jnp.minimum(max_exact + (log_ratio * (REL_BUCKETS - max_exact)).astype(I32), REL_BUCKETS - 1)
    return jnp.where(d < max_exact, d, large)


def _swa_bias_blocks(rel_bias_g, dilation):
    blk = SWA_BLOCK
    qi = jnp.arange(blk, dtype=I32)[:, None]
    kj = jnp.arange(blk, dtype=I32)[None, :]
    table = rel_bias_g.astype(F32)

    def lookup(dist):
        onehot = jax.nn.one_hot(_t5_causal_bucket(dist), REL_BUCKETS, dtype=F32)
        return jnp.einsum('qkb,bh->hqk', onehot, table, precision=lax.Precision.HIGHEST)

    return lookup((qi + blk - kj) * dilation), lookup((qi - kj) * dilation)


def swa_group(qkv, rel_bias_g, dilation):
    d, length, _ = qkv.shape
    blk = SWA_BLOCK
    nb = length // blk
    bp, bc = _swa_bias_blocks(rel_bias_g, dilation)

    def prev(n):
        return jnp.maximum(n - 1, 0)

    def blk3(f):
        return pl.BlockSpec((None, blk, SWA_W), f)

    return pl.pallas_call(
        _swa_kernel,
        grid=(d, nb),
        in_specs=[blk3(lambda r, n: (r, n, 0)),
                  blk3(lambda r, n: (r, prev(n), 1)), blk3(lambda r, n: (r, n, 1)),
                  blk3(lambda r, n: (r, prev(n), 2)), blk3(lambda r, n: (r, n, 2)),
                  pl.BlockSpec((SWA_HEADS_PER_GROUP, blk, blk), lambda r, n: (0, 0, 0)),
                  pl.BlockSpec((SWA_HEADS_PER_GROUP, blk, blk), lambda r, n: (0, 0, 0))],
        out_specs=[blk3(lambda r, n: (r, n, 0)),
                   pl.BlockSpec((None, blk, HEAD_DIM), lambda r, n: (r, n, 0))],
        out_shape=[jax.ShapeDtypeStruct((d, length, SWA_W), BF16),
                   jax.ShapeDtypeStruct((d, length, HEAD_DIM), F32)],
        compiler_params=_compiler_params(("parallel", "arbitrary")),
        name=f"swa_d{dilation}",
    )(qkv, qkv, qkv, qkv, qkv, bp, bc)


def _mix_kernel(odn_ref, o1_ref, o4_ref, o16_ref, l1_ref, l4_ref, l16_ref, gd_ref, gs_ref, wd_ref, ws_ref,
                out_ref, a_ref, oscr_ref, lscr_ref):
    j = pl.program_id(1)
    tm = odn_ref.shape[0]
    groups = ((o1_ref, l1_ref), (o4_ref, l4_ref), (o16_ref, l16_ref))

    @pl.when(j == 0)
    def _():
        for g, (o_ref, l_ref) in enumerate(groups):
            d = o_ref.shape[0]
            rows = tm // d
            for r in range(d):
                if d == 1:
                    lscr_ref[g] = l_ref[0]
                else:
                    lscr_ref[g, pl.ds(r, rows, stride=d), :] = l_ref[r]
                for s in range(SWA_HEADS_PER_GROUP):
                    blk = o_ref[r, :, s * HEAD_DIM:(s + 1) * HEAD_DIM].astype(F32)
                    if d == 1:
                        oscr_ref[g, s] = blk
                    else:
                        oscr_ref[g, s, pl.ds(r, rows, stride=d), :] = blk
        ls = [lscr_ref[g] for g in range(len(groups))]
        m = jnp.maximum(jnp.maximum(ls[0], ls[1]), ls[2])
        es = [jnp.exp(l - m) for l in ls]
        inv = 1.0 / (es[0] + es[1] + es[2])
        for s in range(SWA_HEADS_PER_GROUP):
            acc = jnp.zeros((tm, HEAD_DIM), F32)
            for g in range(len(groups)):
                acc = acc + (es[g] * inv)[:, s:s + 1] * oscr_ref[g, s]
            a_ref[:, s * HEAD_DIM:(s + 1) * HEAD_DIM] = acc.astype(a_ref.dtype)

    y_dn = jnp.dot(odn_ref[...], wd_ref[...].astype(BF16), preferred_element_type=F32)
    y_swa = jnp.dot(a_ref[...], ws_ref[...].astype(BF16), preferred_element_type=F32)
    mixed = jax.nn.sigmoid(gd_ref[...].astype(F32)) * y_dn + jax.nn.sigmoid(gs_ref[...].astype(F32)) * y_swa
    out_ref[...] = mixed.astype(out_ref.dtype)


def mix_up(o_dn, swa_os, swa_ls, gates, w_up_dn, w_up_swa, *, tm=1024, tn=256):
    t = o_dn.shape[0]
    dm = w_up_dn.shape[1]
    nj = dm // tn
    in_specs = [pl.BlockSpec((tm, o_dn.shape[1]), lambda i, j: (i, 0))]
    for o in swa_os:
        d = o.shape[0]
        in_specs.append(pl.BlockSpec((d, tm // d, SWA_W), lambda i, j: (0, i, 0)))
    for l in swa_ls:
        d = l.shape[0]
        in_specs.append(pl.BlockSpec((d, tm // d, HEAD_DIM), lambda i, j: (0, i, 0)))
    in_specs += [pl.BlockSpec((tm, tn), lambda i, j: (i, j)),
                 pl.BlockSpec((tm, tn), lambda i, j: (i, nj + j)),
                 pl.BlockSpec((w_up_dn.shape[0], tn), lambda i, j: (0, j)),
                 pl.BlockSpec((w_up_swa.shape[0], tn), lambda i, j: (0, j))]
    ng = len(swa_os)
    return pl.pallas_call(
        _mix_kernel,
        grid=(t // tm, nj),
        in_specs=in_specs,
        out_specs=pl.BlockSpec((tm, tn), lambda i, j: (i, j)),
        out_shape=jax.ShapeDtypeStruct((t, dm), BF16),
        scratch_shapes=[pltpu.VMEM((tm, SWA_W), BF16),
                        pltpu.VMEM((ng, SWA_HEADS_PER_GROUP, tm, HEAD_DIM), F32),
                        pltpu.VMEM((ng, tm, HEAD_DIM), F32)],
        compiler_params=_compiler_params(("parallel", "arbitrary")),
        name="mix_up",
    )(o_dn, *swa_os, *swa_ls, gates, gates, w_up_dn, w_up_swa)


def _route_kernel(o_ref, x_ref, gt1_ref, np1_ref, np2_ref, sc2_ref, sh2_ref, wr_ref, rb_ref,
                  x1_ref, h2b_ref, h2p_ref, idx_ref, gate_ref):
    o = o_ref[...]
    x1 = x_ref[...] + gt1_ref[...] * (o * lax.rsqrt(jnp.mean(o * o, axis=-1, keepdims=True) + EPS) * np1_ref[...])
    x1_ref[...] = x1
    h2 = (x1 * lax.rsqrt(jnp.mean(x1 * x1, axis=-1, keepdims=True) + EPS) * np2_ref[...]) * (1.0 + sc2_ref[...]) \
        + sh2_ref[...]
    h2b_ref[...] = h2.astype(BF16)
    half = h2.shape[1] // 2
    h2p_ref[...] = _pack_bf16_pair(h2[:, :half], h2[:, half:])

    tm = h2.shape[0]
    e, g, w = N_EXPERTS, N_EXPERT_GROUPS, N_EXPERTS // N_EXPERT_GROUPS
    logits = lax.dot_general(wr_ref[...], h2, _NT, precision=lax.Precision.HIGHEST,
                             preferred_element_type=F32)
    scores = jax.nn.sigmoid(logits)
    sel = (scores + rb_ref[...]).reshape(g, w, tm)
    scores3 = scores.reshape(g, w, tm)
    neg = -jnp.inf
    wi = lax.broadcasted_iota(I32, (g, w, tm), 1)
    gi = lax.broadcasted_iota(I32, (g, w, tm), 0)
    m1 = jnp.max(sel, axis=1, keepdims=True)
    first = jnp.min(jnp.where(sel == m1, wi, w), axis=1, keepdims=True)
    m2 = jnp.max(jnp.where(wi == first, neg, sel), axis=1, keepdims=True)
    grp = m1 + m2
    gi1 = lax.broadcasted_iota(I32, (g, 1, tm), 0)
    chosen = jnp.zeros((g, 1, tm), F32)
    for _ in range(TOPK_GROUPS):
        m = jnp.max(grp, axis=0, keepdims=True)
        fg = jnp.min(jnp.where(grp == m, gi1, g), axis=0, keepdims=True)
        hit = gi1 == fg
        chosen = jnp.where(hit, 1.0, chosen)
        grp = jnp.where(hit, neg, grp)
    cand = jnp.where(chosen > 0.5, sel, neg)
    ei = gi * w + wi
    idx_rows, sc_rows = [], []
    for _ in range(TOP_K):
        m = jnp.max(jnp.max(cand, axis=0, keepdims=True), axis=1, keepdims=True)
        fe = jnp.min(jnp.min(jnp.where(cand == m, ei, e), axis=0, keepdims=True), axis=1, keepdims=True)
        hit = ei == fe
        sc = jnp.sum(jnp.sum(jnp.where(hit, scores3, 0.0), axis=0, keepdims=True), axis=1, keepdims=True)
        cand = jnp.where(hit, neg, cand)
        idx_rows.append(fe.reshape(1, tm))
        sc_rows.append(sc.reshape(1, tm))
    idx = jnp.concatenate(idx_rows, axis=0)
    sc = jnp.concatenate(sc_rows, axis=0)
    idx_ref[...] = idx
    gate_ref[...] = sc / jnp.sum(sc, axis=0, keepdims=True) * ROUTED_SCALE


def moe_route(o, x, gt1, np1, np2, sc2, sh2, w_router, router_bias):
    t, d = x.shape
    tm = ROUTE_TILE
    wr_t = w_router.T.astype(F32)
    rb = router_bias.reshape(N_EXPERTS, 1).astype(F32)

    def row(i):
        return (i, 0)

    def fixed(i):
        return (0, 0)

    vec = pl.BlockSpec((1, d), fixed)
    return pl.pallas_call(
        _route_kernel,
        grid=(t // tm,),
        in_specs=[pl.BlockSpec((tm, d), row), pl.BlockSpec((tm, d), row), vec, vec, vec, vec, vec,
                  pl.BlockSpec((N_EXPERTS, d), fixed), pl.BlockSpec((N_EXPERTS, 1), fixed)],
        out_specs=[pl.BlockSpec((tm, d), row), pl.BlockSpec((tm, d), row), pl.BlockSpec((tm, d // 2), row),
                   pl.BlockSpec((TOP_K, tm), lambda i: (0, i)), pl.BlockSpec((TOP_K, tm), lambda i: (0, i))],
        out_shape=[jax.ShapeDtypeStruct((t, d), F32), jax.ShapeDtypeStruct((t, d), BF16),
                   jax.ShapeDtypeStruct((t, d // 2), U32),
                   jax.ShapeDtypeStruct((TOP_K, t), I32), jax.ShapeDtypeStruct((TOP_K, t), F32)],
        compiler_params=_compiler_params(("parallel",)),
        name="moe_route",
    )(o, x, gt1, np1, np2, sc2, sh2, wr_t, rb)


def _plan_kernel(idx_ref, dest_ref, blk_e_ref, pad_end_ref, nused_ref, counts_ref, carry_ref, pstart_ref):
    p = pl.program_id(0)
    i = pl.program_id(1)
    e = N_EXPERTS
    tn = idx_ref.shape[1]
    idx = idx_ref[...]
    ei = lax.broadcasted_iota(I32, (e, tn), 0)
    member = jnp.zeros((e, tn), F32)
    for k in range(TOP_K):
        member = member + jnp.where(idx[k:k + 1, :] == ei, 1.0, 0.0)
    tile_counts = jnp.sum(member, axis=1, keepdims=True)

    @pl.when((p == 0) & (i == 0))
    def _():
        counts_ref[...] = jnp.zeros_like(counts_ref)

    @pl.when(p == 0)
    def _():
        counts_ref[...] += tile_counts

    @pl.when((p == 1) & (i == 0))
    def _():
        counts = counts_ref[...]
        padded = jnp.ceil(counts * (1.0 / MOE_BLOCK)) * MOE_BLOCK
        sub = lax.broadcasted_iota(I32, (e, e), 0)
        lan = lax.broadcasted_iota(I32, (e, e), 1)
        start_row = jnp.sum(jnp.where(sub < lan, padded, 0.0), axis=0, keepdims=True)
        end_row = jnp.sum(jnp.where(sub <= lan, padded, 0.0), axis=0, keepdims=True)
        pstart_ref[...] = jnp.sum(jnp.where(sub == lan, start_row, 0.0), axis=1, keepdims=True)
        carry_ref[...] = jnp.zeros_like(carry_ref)
        pad_end_ref[...] = end_row.astype(I32)
        nb = blk_e_ref.shape[0]
        bstart = (lax.broadcasted_iota(I32, (nb, e), 0) * MOE_BLOCK).astype(F32)
        below = jnp.sum(jnp.where(end_row <= bstart, 1.0, 0.0), axis=1, keepdims=True)
        blk_e_ref[...] = jnp.minimum(below, e - 1.0).astype(I32)
        nused_ref[...] = (end_row[:, e - 1:e] * (1.0 / MOE_BLOCK)).astype(I32)

    @pl.when(p == 1)
    def _():
        tt = lax.broadcasted_iota(I32, (tn, tn), 0)
        tc = lax.broadcasted_iota(I32, (tn, tn), 1)
        before = jnp.where(tt < tc, 1.0, 0.0).astype(BF16)
        prefix = jnp.dot(member.astype(BF16), before, preferred_element_type=F32)
        base = prefix + carry_ref[...] + pstart_ref[...]
        rows = []
        for k in range(TOP_K):
            rows.append(jnp.sum(jnp.where(idx[k:k + 1, :] == ei, base, 0.0), axis=0, keepdims=True))
        dest_ref[...] = jnp.concatenate(rows, axis=0).astype(I32)
        carry_ref[...] += tile_counts


def moe_plan(idx, nb):
    k, t = idx.shape
    tn = PLAN_TILE
    e = N_EXPERTS
    return pl.pallas_call(
        _plan_kernel,
        grid=(2, t // tn),
        in_specs=[pl.BlockSpec((k, tn), lambda p, i: (0, i))],
        out_specs=[pl.BlockSpec((k, tn), lambda p, i: (0, i * p)),
                   pl.BlockSpec((nb, 1), lambda p, i: (0, 0)),
                   pl.BlockSpec((1, e), lambda p, i: (0, 0)),
                   pl.BlockSpec((1, 1), lambda p, i: (0, 0))],
        out_shape=[jax.ShapeDtypeStruct((k, t), I32), jax.ShapeDtypeStruct((nb, 1), I32),
                   jax.ShapeDtypeStruct((1, e), I32), jax.ShapeDtypeStruct((1, 1), I32)],
        scratch_shapes=[pltpu.VMEM((e, 1), F32), pltpu.VMEM((e, 1), F32), pltpu.VMEM((e, 1), F32)],
        compiler_params=_compiler_params(("arbitrary", "arbitrary")),
        name="moe_plan",
    )(idx)


def _dispatch_kernel(dest_ref, pend_ref, h_ref, xs_hbm, zero_ref, sem):
    i = pl.program_id(0)
    tm = h_ref.shape[0]

    def block_clear(b):
        start = pl.multiple_of(b * MOE_BLOCK, MOE_BLOCK)
        return pltpu.make_async_copy(zero_ref, xs_hbm.at[pl.ds(start, MOE_BLOCK), :], sem.at[1])

    @pl.when(i == 0)
    def _():
        zero_ref[...] = jnp.zeros_like(zero_ref)
        shift = int(math.log2(MOE_BLOCK))
        n_blocks = xs_hbm.shape[0] // MOE_BLOCK
        first_unused = lax.shift_right_logical(pend_ref[N_EXPERTS - 1], shift)

        def last_block(e):
            return lax.shift_right_logical(jnp.maximum(pend_ref[e] - MOE_BLOCK, 0), shift)

        def zstart(e, c):
            block_clear(last_block(e)).start()
            return c
        lax.fori_loop(0, N_EXPERTS, zstart, 0)

        def tstart(b, c):
            block_clear(b).start()
            return c
        lax.fori_loop(first_unused, n_blocks, tstart, 0)

        def zwait(e, c):
            block_clear(0).wait()
            return c
        lax.fori_loop(0, N_EXPERTS, zwait, 0)
        lax.fori_loop(first_unused, n_blocks, zwait, 0)

    def row_copy(t, row):
        return pltpu.make_async_copy(h_ref.at[pl.ds(t, 1), :], xs_hbm.at[pl.ds(row, 1), :], sem.at[0])

    def start(t, c):
        for k in range(TOP_K):
            row_copy(t, dest_ref[t * TOP_K + k]).start(priority=k % DMA_PRIORITIES)
        return c
    lax.fori_loop(0, tm, start, 0)

    tile_rows = xs_hbm.at[pl.ds(0, TOP_K * tm), :]
    pltpu.make_async_copy(tile_rows, tile_rows, sem.at[0]).wait()


def moe_dispatch(h_packed, dest_tk, pad_end, rows):
    t, w = h_packed.shape
    tm = DISPATCH_TILE
    dflat = dest_tk.reshape(t * TOP_K)
    return pl.pallas_call(
        _dispatch_kernel,
        grid=(t // tm,),
        in_specs=[pl.BlockSpec((tm * TOP_K,), lambda i: (i,), memory_space=pltpu.SMEM),
                  pl.BlockSpec((N_EXPERTS,), lambda i: (0,), memory_space=pltpu.SMEM),
                  pl.BlockSpec((tm, w), lambda i: (i, 0))],
        out_specs=pl.BlockSpec(memory_space=pl.ANY),
        out_shape=jax.ShapeDtypeStruct((rows, w), U32),
        scratch_shapes=[pltpu.VMEM((MOE_BLOCK, w), U32), pltpu.SemaphoreType.DMA((2,))],
        compiler_params=_compiler_params(("arbitrary",)),
        name="moe_dispatch",
    )(dflat, pad_end.reshape(N_EXPERTS), h_packed)


def _expert_changed(blk_e_ref, i):
    prev = blk_e_ref[jnp.maximum(i - 1, 0)]
    return (i == 0) | (blk_e_ref[i] != prev)


def _gmm1_kernel(blk_e_ref, next_e_ref, nused_ref, x_ref, w1_hbm, w3_hbm, h_ref, st1_ref, st3_ref, w1b_ref, w3b_ref,
                 sem):
    i = pl.program_id(0)

    def weight_copies(e):
        return (pltpu.make_async_copy(w1_hbm.at[e], st1_ref, sem.at[0]),
                pltpu.make_async_copy(w3_hbm.at[e], st3_ref, sem.at[1]))

    @pl.when(i == 0)
    def _():
        for cp in weight_copies(blk_e_ref[0]):
            cp.start()

    @pl.when(i < nused_ref[0])
    def _():
        @pl.when(_expert_changed(blk_e_ref, i))
        def _():
            for cp in weight_copies(blk_e_ref[i]):
                cp.wait()
            w1b_ref[...] = st1_ref[...].astype(BF16)
            w3b_ref[...] = st3_ref[...].astype(BF16)

            @pl.when(next_e_ref[i] >= 0)
            def _():
                for cp in weight_copies(next_e_ref[i]):
                    cp.start()

        lo, hi = _unpack_bf16_pair(x_ref[...])
        lo = lo.astype(BF16)
        hi = hi.astype(BF16)
        half = lo.shape[1]
        g = (jnp.dot(lo, w1b_ref[:half, :], preferred_element_type=F32)
             + jnp.dot(hi, w1b_ref[half:, :], preferred_element_type=F32))
        u = (jnp.dot(lo, w3b_ref[:half, :], preferred_element_type=F32)
             + jnp.dot(hi, w3b_ref[half:, :], preferred_element_type=F32))
        h_ref[...] = (g * jax.nn.sigmoid(g) * u).astype(h_ref.dtype)

    @pl.when(i >= nused_ref[0])
    def _():
        h_ref[...] = jnp.zeros_like(h_ref)


def _next_expert_table(blk_e, n_used):
    nb = blk_e.shape[0]
    pos = jnp.arange(nb, dtype=I32)
    prev = jnp.concatenate([blk_e[:1], blk_e[:-1]])
    starts = (blk_e != prev) & (pos < n_used[0]) & (pos > 0)
    start_pos = jnp.where(starts, pos, nb)
    shifted = jnp.concatenate([start_pos[1:], jnp.full((1,), nb, I32)])
    nxt = lax.cummin(shifted, axis=0, reverse=True)
    return jnp.where(nxt < nb, blk_e[jnp.minimum(nxt, nb - 1)], -1).astype(I32)


def grouped_gate_up(xs, blk_e, next_e, n_used, w1, w3):
    rows, half = xs.shape
    d = 2 * half
    nb = rows // MOE_BLOCK
    f = w1.shape[2]

    def row_blk(i, nu):
        return jnp.minimum(i, nu[0] - 1)

    grid_spec = pltpu.PrefetchScalarGridSpec(
        num_scalar_prefetch=3,
        grid=(nb,),
        in_specs=[pl.BlockSpec((MOE_BLOCK, half), lambda i, e, ne, nu: (row_blk(i, nu), 0)),
                  pl.BlockSpec(memory_space=pl.ANY),
                  pl.BlockSpec(memory_space=pl.ANY)],
        out_specs=pl.BlockSpec((MOE_BLOCK, f), lambda i, e, ne, nu: (i, 0)),
        scratch_shapes=[pltpu.VMEM((d, f), F32), pltpu.VMEM((d, f), F32),
                        pltpu.VMEM((d, f), BF16), pltpu.VMEM((d, f), BF16),
                        pltpu.SemaphoreType.DMA((2,))],
    )
    return pl.pallas_call(
        _gmm1_kernel,
        grid_spec=grid_spec,
        out_shape=jax.ShapeDtypeStruct((rows, f), BF16),
        compiler_params=_compiler_params(("arbitrary",)),
        name="moe_gate_up",
    )(blk_e, next_e, n_used, xs, w1, w3)


def _gmm2_kernel(blk_e_ref, next_e_ref, nused_ref, h_ref, w2_hbm, y_ref, st2_ref, w2b_ref, sem):
    i = pl.program_id(0)

    def weight_copy(e):
        return pltpu.make_async_copy(w2_hbm.at[e], st2_ref, sem.at[0])

    @pl.when(i == 0)
    def _():
        weight_copy(blk_e_ref[0]).start()

    @pl.when(i < nused_ref[0])
    def _():
        @pl.when(_expert_changed(blk_e_ref, i))
        def _():
            weight_copy(blk_e_ref[i]).wait()
            w2b_ref[...] = st2_ref[...].astype(BF16)

            @pl.when(next_e_ref[i] >= 0)
            def _():
                weight_copy(next_e_ref[i]).start()

        y = jnp.dot(h_ref[...], w2b_ref[...], preferred_element_type=F32)
        half = y.shape[1] // 2
        y_ref[...] = _pack_bf16_pair(y[:, :half], y[:, half:])

    @pl.when(i >= nused_ref[0])
    def _():
        y_ref[...] = jnp.zeros_like(y_ref)


def grouped_down(hs, blk_e, next_e, n_used, w2):
    rows, f = hs.shape
    nb = rows // MOE_BLOCK
    d = w2.shape[2]

    def row_blk(i, nu):
        return jnp.minimum(i, nu[0] - 1)

    grid_spec = pltpu.PrefetchScalarGridSpec(
        num_scalar_prefetch=3,
        grid=(nb,),
        in_specs=[pl.BlockSpec((MOE_BLOCK, f), lambda i, e, ne, nu: (row_blk(i, nu), 0)),
                  pl.BlockSpec(memory_space=pl.ANY)],
        out_specs=pl.BlockSpec((MOE_BLOCK, d // 2), lambda i, e, ne, nu: (i, 0)),
        scratch_shapes=[pltpu.VMEM((f, d), F32), pltpu.VMEM((f, d), BF16), pltpu.SemaphoreType.DMA((1,))],
    )
    return pl.pallas_call(
        _gmm2_kernel,
        grid_spec=grid_spec,
        out_shape=jax.ShapeDtypeStruct((rows, d // 2), U32),
        compiler_params=_compiler_params(("arbitrary",)),
        name="moe_down",
    )(blk_e, next_e, n_used, hs, w2)


def _combine_kernel(dcur_ref, dnext_ref, y_hbm, gate_ref, shared_ref, x_ref, gt_ref, nw_ref, o_ref, gbuf, sem):
    i = pl.program_id(0)
    n = pl.num_programs(0)
    tm = x_ref.shape[0]
    slot = lax.rem(i, 2)

    def row_copy(s, src_row, dst_row):
        return pltpu.make_async_copy(y_hbm.at[pl.ds(src_row, 1), :], gbuf.at[s, pl.ds(dst_row, 1), :], sem.at[s])

    def start_tile(d_ref, s):
        def body(t, carry):
            for k in range(TOP_K):
                row_copy(s, d_ref[t * TOP_K + k], k * tm + t).start(priority=k % DMA_PRIORITIES)
            return carry
        lax.fori_loop(0, tm, body, 0)

    @pl.when(i == 0)
    def _():
        start_tile(dcur_ref, 0)

    @pl.when(i + 1 < n)
    def _():
        start_tile(dnext_ref, 1 - slot)

    pltpu.make_async_copy(y_hbm.at[pl.ds(0, TOP_K * tm), :], gbuf.at[slot], sem.at[slot]).wait()

    gate = gate_ref[...]
    half = gbuf.shape[2]
    acc_lo = shared_ref[:, :half].astype(F32)
    acc_hi = shared_ref[:, half:].astype(F32)
    for k in range(TOP_K):
        lo, hi = _unpack_bf16_pair(gbuf[slot, k * tm:(k + 1) * tm, :])
        gk = gate[:, k:k + 1]
        acc_lo = acc_lo + gk * lo
        acc_hi = acc_hi + gk * hi
    ms = (jnp.sum(acc_lo * acc_lo, axis=-1, keepdims=True) + jnp.sum(acc_hi * acc_hi, axis=-1, keepdims=True)) \
        * (1.0 / (2 * half))
    r = lax.rsqrt(ms + EPS)
    o_ref[:, :half] = x_ref[:, :half] + gt_ref[:, :half] * (acc_lo * r * nw_ref[:, :half])
    o_ref[:, half:] = x_ref[:, half:] + gt_ref[:, half:] * (acc_hi * r * nw_ref[:, half:])


def moe_combine(ys, dest_tk, gate_tk, shared, x, gt, nw):
    t, d = x.shape
    tm = COMBINE_TILE
    nt = t // tm
    dflat = dest_tk.reshape(t * TOP_K)
    return pl.pallas_call(
        _combine_kernel,
        grid=(nt,),
        in_specs=[pl.BlockSpec((tm * TOP_K,), lambda i: (i,), memory_space=pltpu.SMEM),
                  pl.BlockSpec((tm * TOP_K,), lambda i: (jnp.minimum(i + 1, nt - 1),), memory_space=pltpu.SMEM),
                  pl.BlockSpec(memory_space=pl.ANY),
                  pl.BlockSpec((tm, TOP_K), lambda i: (i, 0)),
                  pl.BlockSpec((tm, d), lambda i: (i, 0)),
                  pl.BlockSpec((tm, d), lambda i: (i, 0)),
                  pl.BlockSpec((1, d), lambda i: (0, 0)),
                  pl.BlockSpec((1, d), lambda i: (0, 0))],
        out_specs=pl.BlockSpec((tm, d), lambda i: (i, 0)),
        out_shape=jax.ShapeDtypeStruct((t, d), F32),
        scratch_shapes=[pltpu.VMEM((2, TOP_K * tm, d // 2), U32), pltpu.SemaphoreType.DMA((2,))],
        compiler_params=_compiler_params(("arbitrary",)),
        name="moe_combine",
    )(dflat, dflat, ys, gate_tk, shared, x, gt, nw)


def kernel(x, c, w_mod, b_mod, norm_pre1, norm_post1, w_in, conv_w, a_log, dt_bias, dn_norm_w, w_up_dn,
           w_up_swa, w_out, rel_bias, norm_pre2, norm_post2, w_router, router_bias, w1, w3, w2, ws1, ws3, ws2):
    B, S, D = x.shape
    assert B == 1
    T = B * S
    x = x.reshape(T, D)
    depth = w_mod.shape[0]
    n_swa_cols = 3 * SWA_W
    nb = -(-(T * TOP_K) // MOE_BLOCK) + N_EXPERTS
    for layer in range(depth):
        sc = jnp.broadcast_to(jax.nn.silu(c), (8, D))
        mod = dense_matmul(sc, w_mod[layer], tm=8, tn=1024, out_dtype=F32, name="adaln_mod")[:B] + b_mod[layer]
        sh1, sc1, gt1, sh2, sc2, gt2 = jnp.split(mod, 6, axis=-1)

        def vec(v):
            return v[layer].reshape(1, D)

        h = prenorm_modulate(x, vec(norm_pre1), sc1, sh1)
        w_t = jnp.swapaxes(w_in[layer], 0, 1)
        proj_dn = dense_matmul_nt(h, w_t, row_lo=0, row_hi=COL_BA, tm=1024, tn=512, out_dtype=F32,
                                  name="in_proj_dn")
        ba = dense_matmul_nt(h, w_t, row_lo=COL_BA, row_hi=COL_BA + LANES, tm=1024, tn=LANES, out_dtype=F32,
                             name="in_proj_ba")
        gates = dense_matmul_nt(h, w_t, row_lo=COL_GATES, row_hi=COL_END, tm=1024, tn=512, out_dtype=BF16,
                                name="in_proj_gates")
        o_dn = gated_deltanet(proj_dn, ba, conv_w[layer], a_log[layer], dt_bias[layer], dn_norm_w[layer])
        swa_os, swa_ls = [], []
        for gi, (_, dilation) in enumerate(SWA_GROUPS):
            qkv = project_residue_major(h, w_t, dilation=dilation, row_lo=COL_SWA + gi * n_swa_cols,
                                        row_hi=COL_SWA + (gi + 1) * n_swa_cols, tm=1024, tn=512,
                                        name=f"in_proj_swa_d{dilation}")
            heads = slice(gi * SWA_HEADS_PER_GROUP, (gi + 1) * SWA_HEADS_PER_GROUP)
            o_g, lse_g = swa_group(qkv, rel_bias[:, heads], dilation)
            swa_os.append(o_g)
            swa_ls.append(lse_g)
        mixed = mix_up(o_dn, swa_os, swa_ls, gates, w_up_dn[layer], w_up_swa[layer])
        o = dense_matmul(mixed, w_out[layer], tm=1024, tn=512, out_dtype=F32, name="out_proj")
        x1, h2b, h2p, idx, gate = moe_route(o, x, gt1, vec(norm_post1), vec(norm_pre2), sc2, sh2,
                                            w_router[layer], router_bias[layer])
        dest, blk_e, pad_end, n_used = moe_plan(idx, nb)
        dest_tk = dest.T
        gate_tk = gate.T
        xs = moe_dispatch(h2p, dest_tk, pad_end, nb * MOE_BLOCK)
        blk_e = blk_e.reshape(nb)
        n_used = n_used.reshape(1)
        next_e = _next_expert_table(blk_e, n_used)
        hs = grouped_gate_up(xs, blk_e, next_e, n_used, w1[layer], w3[layer])
        ys = grouped_down(hs, blk_e, next_e, n_used, w2[layer])
        hsh = swiglu_up(h2b, ws1[layer], ws3[layer], tm=1024, tn=D_EXPERT // 3, name="shared_gate_up")
        shared = dense_matmul(hsh, ws2[layer], tm=1024, tn=512, out_dtype=F32, name="shared_down")
        x = moe_combine(ys, dest_tk, gate_tk, shared, x1, gt2, vec(norm_post2))
    return x.reshape(B, S, D)
```

```python
import math

import jax
import jax.numpy as jnp
from jax import lax
from jax.experimental import pallas as pl
from jax.experimental.pallas import tpu as pltpu

D_MODEL = 4096
HEAD_DIM = 128
EPS = 1e-6
DN_HEADS = 16
DN_CONV = 4
DN_QK = DN_HEADS * HEAD_DIM
DN_VW = DN_HEADS * HEAD_DIM
SWA_GROUPS = ((128, 1), (512, 4), (2048, 16))
SWA_HEADS_PER_GROUP = 8
SWA_BLOCK = 128
SWA_W = SWA_HEADS_PER_GROUP * HEAD_DIM
REL_BUCKETS = 32
REL_MAX_DIST = 2048
N_EXPERTS = 64
N_EXPERT_GROUPS = 8
TOPK_GROUPS = 4
TOP_K = 8
D_EXPERT = 768
ROUTED_SCALE = 2.5
MOE_BLOCK = 256

COL_BA = 4 * DN_QK
COL_SWA = COL_BA + 2 * DN_HEADS
COL_GATES = COL_SWA + 3 * len(SWA_GROUPS) * SWA_W
COL_END = COL_GATES + 2 * D_MODEL

LANES = 128
SUBLANES = 8
DMA_PRIORITIES = 2
VMEM_LIMIT_BYTES = 56 * 1024 * 1024

DN_TILE = 256
DN_HEADS_PER_STEP = 8
DN_INV_BASE = 16
CONV_PAD = 8
ROUTE_TILE = 256
PLAN_TILE = 512
DISPATCH_TILE = 128
COMBINE_TILE = 64

F32 = jnp.float32
BF16 = jnp.bfloat16
U32 = jnp.uint32
I32 = jnp.int32
HI_MASK = 0xFFFF0000

_NT = (((1,), (1,)), ((), ()))


def _compiler_params(semantics):
    return pltpu.CompilerParams(dimension_semantics=semantics, vmem_limit_bytes=VMEM_LIMIT_BYTES)


def _pack_bf16_pair(lo, hi):
    lo_bits = pltpu.bitcast(lo.astype(BF16).astype(F32), U32) >> jnp.uint32(16)
    hi_bits = pltpu.bitcast(hi.astype(BF16).astype(F32), U32)
    return lo_bits | hi_bits


def _unpack_bf16_pair(w):
    lo = pltpu.bitcast(w << jnp.uint32(16), F32)
    hi = pltpu.bitcast(w & jnp.uint32(HI_MASK), F32)
    return lo, hi


def _mm_kernel(a_ref, b_ref, o_ref):
    a = a_ref[...].astype(BF16)
    b = b_ref[...].astype(BF16)
    o_ref[...] = jnp.dot(a, b, preferred_element_type=F32).astype(o_ref.dtype)


def dense_matmul(a, b, *, tm, tn, out_dtype, name, col_lo=0, col_hi=None):
    m, k = a.shape
    col_hi = b.shape[1] if col_hi is None else col_hi
    n = col_hi - col_lo
    assert m % tm == 0 and col_lo % tn == 0
    off = col_lo // tn
    return pl.pallas_call(
        _mm_kernel,
        grid=(m // tm, pl.cdiv(n, tn)),
        in_specs=[pl.BlockSpec((tm, k), lambda i, j: (i, 0)),
                  pl.BlockSpec((k, tn), lambda i, j: (0, j + off))],
        out_specs=pl.BlockSpec((tm, tn), lambda i, j: (i, j)),
        out_shape=jax.ShapeDtypeStruct((m, n), out_dtype),
        compiler_params=_compiler_params(("parallel", "arbitrary")),
        name=name,
    )(a, b)


def _swiglu_up_kernel(a_ref, wg_ref, wu_ref, o_ref):
    a = a_ref[...]
    g = jnp.dot(a, wg_ref[...].astype(BF16), preferred_element_type=F32)
    u = jnp.dot(a, wu_ref[...].astype(BF16), preferred_element_type=F32)
    o_ref[...] = (g * jax.nn.sigmoid(g) * u).astype(o_ref.dtype)


def swiglu_up(a, wg, wu, *, tm, tn, name):
    m, k = a.shape
    n = wg.shape[1]
    assert m % tm == 0 and n % tn == 0
    return pl.pallas_call(
        _swiglu_up_kernel,
        grid=(m // tm, n // tn),
        in_specs=[pl.BlockSpec((tm, k), lambda i, j: (i, 0)),
                  pl.BlockSpec((k, tn), lambda i, j: (0, j)),
                  pl.BlockSpec((k, tn), lambda i, j: (0, j))],
        out_specs=pl.BlockSpec((tm, tn), lambda i, j: (i, j)),
        out_shape=jax.ShapeDtypeStruct((m, n), BF16),
        compiler_params=_compiler_params(("parallel", "arbitrary")),
        name=name,
    )(a, wg, wu)


def _weight_rows_spec(k, tn, row_lo):
    assert row_lo % SUBLANES == 0 and tn % SUBLANES == 0
    return pl.BlockSpec((pl.Element(tn), pl.Element(k)),
                        lambda i, j: ((row_lo // SUBLANES + j * (tn // SUBLANES)) * SUBLANES, 0))


def _mm_nt_kernel(a_ref, w_ref, o_ref):
    a = a_ref[...].astype(BF16)
    w = w_ref[...].astype(BF16)
    o_ref[...] = lax.dot_general(a, w, _NT, preferred_element_type=F32).astype(o_ref.dtype)


def dense_matmul_nt(a, w_t, *, row_lo, row_hi, tm, tn, out_dtype, name):
    m, k = a.shape
    n = row_hi - row_lo
    assert m % tm == 0 and n % tn == 0
    return pl.pallas_call(
        _mm_nt_kernel,
        grid=(m // tm, n // tn),
        in_specs=[pl.BlockSpec((tm, k), lambda i, j: (i, 0)), _weight_rows_spec(k, tn, row_lo)],
        out_specs=pl.BlockSpec((tm, tn), lambda i, j: (i, j)),
        out_shape=jax.ShapeDtypeStruct((m, n), out_dtype),
        compiler_params=_compiler_params(("parallel", "arbitrary")),
        name=name,
    )(a, w_t)


def _prenorm_kernel(x_ref, nw_ref, sc_ref, sh_ref, h_ref):
    x = x_ref[...]
    y = x * lax.rsqrt(jnp.mean(x * x, axis=-1, keepdims=True) + EPS) * nw_ref[...]
    h_ref[...] = (y * (1.0 + sc_ref[...]) + sh_ref[...]).astype(h_ref.dtype)


def prenorm_modulate(x, nw, sc, sh, *, tm=512):
    t, d = x.shape
    vec = pl.BlockSpec((1, d), lambda i: (0, 0))
    return pl.pallas_call(
        _prenorm_kernel,
        grid=(t // tm,),
        in_specs=[pl.BlockSpec((tm, d), lambda i: (i, 0)), vec, vec, vec],
        out_specs=pl.BlockSpec((tm, d), lambda i: (i, 0)),
        out_shape=jax.ShapeDtypeStruct((t, d), BF16),
        compiler_params=_compiler_params(("parallel",)),
        name="prenorm_modulate",
    )(x, nw, sc, sh)


def _mm_residue_kernel(a_ref, w_ref, o_ref, res_ref):
    d = o_ref.shape[0]
    rows = o_ref.shape[1]
    res = lax.dot_general(a_ref[...], w_ref[...].astype(BF16), _NT, preferred_element_type=F32)
    if d == 1:
        o_ref[0] = res.astype(o_ref.dtype)
        return
    nslab = res.shape[1] // LANES
    for s in range(nslab):
        res_ref[s] = res[:, s * LANES:(s + 1) * LANES]
    for r in range(d):
        for s in range(nslab):
            o_ref[r, :, s * LANES:(s + 1) * LANES] = res_ref[s, pl.ds(r, rows, stride=d), :].astype(o_ref.dtype)


def project_residue_major(a, w_t, *, dilation, row_lo, row_hi, tm, tn, name):
    m, k = a.shape
    n = row_hi - row_lo
    assert m % tm == 0 and n % tn == 0 and tm % (dilation * 16) == 0
    rows = tm // dilation
    return pl.pallas_call(
        _mm_residue_kernel,
        grid=(m // tm, n // tn),
        in_specs=[pl.BlockSpec((tm, k), lambda i, j: (i, 0)), _weight_rows_spec(k, tn, row_lo)],
        out_specs=pl.BlockSpec((dilation, rows, tn), lambda i, j: (0, i, j)),
        out_shape=jax.ShapeDtypeStruct((dilation, m // dilation, n), BF16),
        scratch_shapes=[pltpu.VMEM((tn // LANES, tm, LANES), F32)],
        compiler_params=_compiler_params(("parallel", "arbitrary")),
        name=name,
    )(a, w_t)


_HEAD_BATCH = ((0,), (0,))
_BMM = (((2,), (1,)), _HEAD_BATCH)
_BMM_NT = (((2,), (2,)), _HEAD_BATCH)
_BMM_TN = (((1,), (1,)), _HEAD_BATCH)


def _bdot(a, b, dims=_BMM):
    return lax.dot_general(a.astype(BF16), b.astype(BF16), dims, preferred_element_type=F32)


def _unit_lower_inverse(low, row, col):
    c = low.shape[-1]
    s = DN_INV_BASE
    same = (row // s) == (col // s)
    ld = jnp.where(same, low, 0.0)
    x = jnp.where(row == col, 1.0, 0.0) - ld
    p = _bdot(ld, ld)
    steps = int(math.log2(s)) - 1
    for it in range(steps):
        x = x + _bdot(x, p)
        if it + 1 < steps:
            p = _bdot(p, p)
    while s < c:
        off = ((row // (2 * s)) == (col // (2 * s))) & ((row // s) != (col // s))
        b = jnp.where(off, low, 0.0)
        x = x - _bdot(_bdot(x, b), x)
        s *= 2
    return x


def _dn_kernel(q_ref, k_ref, v_ref, z_ref, cwq_ref, cwk_ref, cwv_ref, bcol_ref, gcol_ref, grow_ref, nw_ref,
               o_ref, state_ref, xe_ref):
    i = pl.program_id(1)
    c = q_ref.shape[0]
    gw = q_ref.shape[1]

    @pl.when(i == 0)
    def _():
        state_ref[...] = jnp.zeros_like(state_ref)
        xe_ref[0:CONV_PAD, :] = jnp.zeros((CONV_PAD, xe_ref.shape[1]), F32)

    xe_ref[CONV_PAD:CONV_PAD + c, 0:gw] = q_ref[...]
    xe_ref[CONV_PAD:CONV_PAD + c, gw:2 * gw] = k_ref[...]
    xe_ref[CONV_PAD:CONV_PAD + c, 2 * gw:3 * gw] = v_ref[...]

    def conv_silu(lo, w_ref):
        acc = w_ref[DN_CONV - 1:DN_CONV, :] * xe_ref[CONV_PAD:CONV_PAD + c, lo:lo + gw]
        for j in range(DN_CONV - 1):
            r0 = CONV_PAD - (DN_CONV - 1) + j
            acc = acc + w_ref[j:j + 1, :] * xe_ref[r0:r0 + c, lo:lo + gw]
        return acc * jax.nn.sigmoid(acc)

    hps = gw // HEAD_DIM

    def heads(x):
        return jnp.stack([x[:, j * HEAD_DIM:(j + 1) * HEAD_DIM] for j in range(hps)])

    qh = heads(conv_silu(0, cwq_ref))
    kh = heads(conv_silu(gw, cwk_ref))
    vh = heads(conv_silu(2 * gw, cwv_ref))
    xe_ref[0:CONV_PAD, :] = xe_ref[c:c + CONV_PAD, :]

    row = lax.broadcasted_iota(I32, (c, c), 0)
    col = lax.broadcasted_iota(I32, (c, c), 1)
    ge = row >= col
    gt = row > col
    bcols, gcols, grows = bcol_ref[0], gcol_ref[0], grow_ref[0]
    bcol = jnp.stack([bcols[:, j:j + 1] for j in range(hps)])
    gcol = jnp.stack([gcols[:, j:j + 1] for j in range(hps)])
    grow = jnp.stack([grows[j:j + 1, :] for j in range(hps)])

    qn = qh * (lax.rsqrt(jnp.sum(qh * qh, axis=-1, keepdims=True) + EPS) * (HEAD_DIM ** -0.5))
    kn = kh * lax.rsqrt(jnp.sum(kh * kh, axis=-1, keepdims=True) + EPS)
    diff = gcol - grow
    decay = jnp.where(ge, jnp.exp(jnp.where(ge, diff, 0.0)), 0.0)
    kb = kn * bcol
    kf = kn.astype(BF16)
    low = jnp.where(gt, _bdot(kb, kf, _BMM_NT) * decay, 0.0)
    attn = _bdot(qn, kf, _BMM_NT) * decay
    tinv = _unit_lower_inverse(low, row, col)
    eg = jnp.exp(gcol)
    rhs = jnp.concatenate([vh * bcol, kb * eg], axis=2)
    uw = _bdot(tinv, rhs)
    u, w = uw[:, :, :HEAD_DIM], uw[:, :, HEAD_DIM:]
    s = state_ref[...]
    sb = s.astype(BF16)
    v_new = u - _bdot(w, sb)
    vnb = v_new.astype(BF16)
    o = _bdot(qn * eg, sb) + _bdot(attn, vnb)
    glast = grow[:, :, c - 1:c]
    kdec = kn * jnp.exp(glast - gcol)
    state_ref[...] = s * jnp.exp(glast) + _bdot(kdec, vnb, _BMM_TN)
    on = o * lax.rsqrt(jnp.mean(o * o, axis=-1, keepdims=True) + EPS) * nw_ref[...]
    for j in range(hps):
        hs = slice(j * HEAD_DIM, (j + 1) * HEAD_DIM)
        zf = z_ref[:, hs]
        o_ref[:, hs] = (on[j] * (zf * jax.nn.sigmoid(zf))).astype(o_ref.dtype)


def _dn_gates(ba, a_log, dt_bias):
    t = ba.shape[0]
    hps = DN_HEADS_PER_STEP
    ng = DN_HEADS // hps
    beta = jax.nn.sigmoid(ba[:, :DN_HEADS])
    g = -jnp.exp(a_log) * jax.nn.softplus(ba[:, DN_HEADS:2 * DN_HEADS] + dt_bias)
    gc = jnp.cumsum(g.reshape(t // DN_TILE, DN_TILE, DN_HEADS), axis=1).reshape(t, DN_HEADS)
    bcol = beta.reshape(t, ng, hps).transpose(1, 0, 2)
    gcol = gc.reshape(t, ng, hps).transpose(1, 0, 2)
    grow = gc.reshape(t, ng, hps).transpose(1, 2, 0)
    return bcol, gcol, grow


def gated_deltanet(proj_dn, ba, conv_w, a_log, dt_bias, norm_w):
    t = proj_dn.shape[0]
    c = DN_TILE
    hps = DN_HEADS_PER_STEP
    gw = hps * HEAD_DIM
    ng = DN_HEADS // hps
    bcol, gcol, grow = _dn_gates(ba, a_log, dt_bias)
    nw = norm_w.reshape(1, HEAD_DIM).astype(F32)
    return pl.pallas_call(
        _dn_kernel,
        grid=(ng, t // c),
        in_specs=[pl.BlockSpec((c, gw), lambda g, i: (i, g)),
                  pl.BlockSpec((c, gw), lambda g, i: (i, ng + g)),
                  pl.BlockSpec((c, gw), lambda g, i: (i, 2 * ng + g)),
                  pl.BlockSpec((c, gw), lambda g, i: (i, 3 * ng + g)),
                  pl.BlockSpec((DN_CONV, gw), lambda g, i: (0, g)),
                  pl.BlockSpec((DN_CONV, gw), lambda g, i: (0, ng + g)),
                  pl.BlockSpec((DN_CONV, gw), lambda g, i: (0, 2 * ng + g)),
                  pl.BlockSpec((1, c, hps), lambda g, i: (g, i, 0)),
                  pl.BlockSpec((1, c, hps), lambda g, i: (g, i, 0)),
                  pl.BlockSpec((1, hps, c), lambda g, i: (g, 0, i)),
                  pl.BlockSpec((1, HEAD_DIM), lambda g, i: (0, 0))],
        out_specs=pl.BlockSpec((c, gw), lambda g, i: (i, g)),
        out_shape=jax.ShapeDtypeStruct((t, DN_VW), BF16),
        scratch_shapes=[pltpu.VMEM((hps, HEAD_DIM, HEAD_DIM), F32),
                        pltpu.VMEM((c + CONV_PAD, 3 * gw), F32)],
        compiler_params=_compiler_params(("parallel", "arbitrary")),
        name="gated_deltanet",
    )(proj_dn, proj_dn, proj_dn, proj_dn, conv_w, conv_w, conv_w, bcol, gcol, grow, nw)


def _swa_kernel(q_ref, kp_ref, kc_ref, vp_ref, vc_ref, bp_ref, bc_ref, o_ref, lse_ref):
    n = pl.program_id(1)
    blk = SWA_BLOCK
    qi = lax.broadcasted_iota(I32, (blk, blk), 0)
    kj = lax.broadcasted_iota(I32, (blk, blk), 1)
    valid_prev = (kj >= qi) & (n > 0)
    valid_cur = kj <= qi
    scale = HEAD_DIM ** -0.5
    nh = SWA_HEADS_PER_GROUP

    def heads(ref):
        return jnp.stack([ref[:, h * HEAD_DIM:(h + 1) * HEAD_DIM] for h in range(nh)])

    q = heads(q_ref)
    sp = lax.dot_general(q, heads(kp_ref), _BMM_NT, preferred_element_type=F32) * scale + bp_ref[...]
    sc = lax.dot_general(q, heads(kc_ref), _BMM_NT, preferred_element_type=F32) * scale + bc_ref[...]
    sp = jnp.where(valid_prev, sp, -jnp.inf)
    sc = jnp.where(valid_cur, sc, -jnp.inf)
    m = jnp.maximum(jnp.max(sp, axis=-1, keepdims=True), jnp.max(sc, axis=-1, keepdims=True))
    pp = jnp.exp(sp - m)
    pc = jnp.exp(sc - m)
    den = jnp.sum(pp, axis=-1, keepdims=True) + jnp.sum(pc, axis=-1, keepdims=True)
    o = (lax.dot_general(pp.astype(BF16), heads(vp_ref), _BMM, preferred_element_type=F32)
         + lax.dot_general(pc.astype(BF16), heads(vc_ref), _BMM, preferred_element_type=F32))
    o = o / den
    lse = m + jnp.log(den)
    lane = lax.broadcasted_iota(I32, (blk, HEAD_DIM), 1)
    lse_all = jnp.zeros((blk, HEAD_DIM), F32)
    for h in range(nh):
        o_ref[:, h * HEAD_DIM:(h + 1) * HEAD_DIM] = o[h].astype(o_ref.dtype)
        lse_all = jnp.where(lane == h, lse[h], lse_all)
    lse_ref[...] = lse_all


def _t5_causal_bucket(dist):
    max_exact = REL_BUCKETS // 2
    d = jnp.maximum(dist, 0)
    log_ratio = jnp.log(jnp.maximum(d, 1).astype(F32) / max_exact) / math.log(REL_MAX_DIST / max_exact)
    large = jnp.minimum(max_exact + (log_ratio * (REL_BUCKETS - max_exact)).astype(I32), REL_BUCKETS - 1)
    return jnp.where(d < max_exact, d, large)


def _swa_bias_blocks(rel_bias_g, dilation):
    blk = SWA_BLOCK
    qi = jnp.arange(blk, dtype=I32)[:, None]
    kj = jnp.arange(blk, dtype=I32)[None, :]
    table = rel_bias_g.astype(F32)

    def lookup(dist):
        onehot = jax.nn.one_hot(_t5_causal_bucket(dist), REL_BUCKETS, dtype=F32)
        return jnp.einsum('qkb,bh->hqk', onehot, table, precision=lax.Precision.HIGHEST)

    return lookup((qi + blk - kj) * dilation), lookup((qi - kj) * dilation)


def swa_group(qkv, rel_bias_g, dilation):
    d, length, _ = qkv.shape
    blk = SWA_BLOCK
    nb = length // blk
    bp, bc = _swa_bias_blocks(rel_bias_g, dilation)

    def prev(n):
        return jnp.maximum(n - 1, 0)

    def blk3(f):
        return pl.BlockSpec((None, blk, SWA_W), f)

    return pl.pallas_call(
        _swa_kernel,
        grid=(d, nb),
        in_specs=[blk3(lambda r, n: (r, n, 0)),
                  blk3(lambda r, n: (r, prev(n), 1)), blk3(lambda r, n: (r, n, 1)),
                  blk3(lambda r, n: (r, prev(n), 2)), blk3(lambda r, n: (r, n, 2)),
                  pl.BlockSpec((SWA_HEADS_PER_GROUP, blk, blk), lambda r, n: (0, 0, 0)),
                  pl.BlockSpec((SWA_HEADS_PER_GROUP, blk, blk), lambda r, n: (0, 0, 0))],
        out_specs=[blk3(lambda r, n: (r, n, 0)),
                   pl.BlockSpec((None, blk, HEAD_DIM), lambda r, n: (r, n, 0))],
        out_shape=[jax.ShapeDtypeStruct((d, length, SWA_W), BF16),
                   jax.ShapeDtypeStruct((d, length, HEAD_DIM), F32)],
        compiler_params=_compiler_params(("parallel", "arbitrary")),
        name=f"swa_d{dilation}",
    )(qkv, qkv, qkv, qkv, qkv, bp, bc)


def _mix_kernel(odn_ref, o1_ref, o4_ref, o16_ref, l1_ref, l4_ref, l16_ref, gd_ref, gs_ref, wd_ref, ws_ref,
                out_ref, a_ref, oscr_ref, lscr_ref):
    j = pl.program_id(1)
    tm = odn_ref.shape[0]
    groups = ((o1_ref, l1_ref), (o4_ref, l4_ref), (o16_ref, l16_ref))

    @pl.when(j == 0)
    def _():
        for g, (o_ref, l_ref) in enumerate(groups):
            d = o_ref.shape[0]
            rows = tm // d
            for r in range(d):
                if d == 1:
                    lscr_ref[g] = l_ref[0]
                else:
                    lscr_ref[g, pl.ds(r, rows, stride=d), :] = l_ref[r]
                for s in range(SWA_HEADS_PER_GROUP):
                    blk = o_ref[r, :, s * HEAD_DIM:(s + 1) * HEAD_DIM].astype(F32)
                    if d == 1:
                        oscr_ref[g, s] = blk
                    else:
                        oscr_ref[g, s, pl.ds(r, rows, stride=d), :] = blk
        ls = [lscr_ref[g] for g in range(len(groups))]
        m = jnp.maximum(jnp.maximum(ls[0], ls[1]), ls[2])
        es = [jnp.exp(l - m) for l in ls]
        inv = 1.0 / (es[0] + es[1] + es[2])
        for s in range(SWA_HEADS_PER_GROUP):
            acc = jnp.zeros((tm, HEAD_DIM), F32)
            for g in range(len(groups)):
                acc = acc + (es[g] * inv)[:, s:s + 1] * oscr_ref[g, s]
            a_ref[:, s * HEAD_DIM:(s + 1) * HEAD_DIM] = acc.astype(a_ref.dtype)

    y_dn = jnp.dot(odn_ref[...], wd_ref[...].astype(BF16), preferred_element_type=F32)
    y_swa = jnp.dot(a_ref[...], ws_ref[...].astype(BF16), preferred_element_type=F32)
    mixed = jax.nn.sigmoid(gd_ref[...].astype(F32)) * y_dn + jax.nn.sigmoid(gs_ref[...].astype(F32)) * y_swa
    out_ref[...] = mixed.astype(out_ref.dtype)


def mix_up(o_dn, swa_os, swa_ls, gates, w_up_dn, w_up_swa, *, tm=1024, tn=256):
    t = o_dn.shape[0]
    dm = w_up_dn.shape[1]
    nj = dm // tn
    in_specs = [pl.BlockSpec((tm, o_dn.shape[1]), lambda i, j: (i, 0))]
    for o in swa_os:
        d = o.shape[0]
        in_specs.append(pl.BlockSpec((d, tm // d, SWA_W), lambda i, j: (0, i, 0)))
    for l in swa_ls:
        d = l.shape[0]
        in_specs.append(pl.BlockSpec((d, tm // d, HEAD_DIM), lambda i, j: (0, i, 0)))
    in_specs += [pl.BlockSpec((tm, tn), lambda i, j: (i, j)),
                 pl.BlockSpec((tm, tn), lambda i, j: (i, nj + j)),
                 pl.BlockSpec((w_up_dn.shape[0], tn), lambda i, j: (0, j)),
                 pl.BlockSpec((w_up_swa.shape[0], tn), lambda i, j: (0, j))]
    ng = len(swa_os)
    return pl.pallas_call(
        _mix_kernel,
        grid=(t // tm, nj),
        in_specs=in_specs,
        out_specs=pl.BlockSpec((tm, tn), lambda i, j: (i, j)),
        out_shape=jax.ShapeDtypeStruct((t, dm), BF16),
        scratch_shapes=[pltpu.VMEM((tm, SWA_W), BF16),
                        pltpu.VMEM((ng, SWA_HEADS_PER_GROUP, tm, HEAD_DIM), F32),
                        pltpu.VMEM((ng, tm, HEAD_DIM), F32)],
        compiler_params=_compiler_params(("parallel", "arbitrary")),
        name="mix_up",
    )(o_dn, *swa_os, *swa_ls, gates, gates, w_up_dn, w_up_swa)


def _route_kernel(o_ref, x_ref, gt1_ref, np1_ref, np2_ref, sc2_ref, sh2_ref, wr_ref, rb_ref,
                  x1_ref, h2b_ref, h2p_ref, idx_ref, gate_ref):
    o = o_ref[...]
    x1 = x_ref[...] + gt1_ref[...] * (o * lax.rsqrt(jnp.mean(o * o, axis=-1, keepdims=True) + EPS) * np1_ref[...])
    x1_ref[...] = x1
    h2 = (x1 * lax.rsqrt(jnp.mean(x1 * x1, axis=-1, keepdims=True) + EPS) * np2_ref[...]) * (1.0 + sc2_ref[...]) \
        + sh2_ref[...]
    h2b_ref[...] = h2.astype(BF16)
    half = h2.shape[1] // 2
    h2p_ref[...] = _pack_bf16_pair(h2[:, :half], h2[:, half:])

    tm = h2.shape[0]
    e, g, w = N_EXPERTS, N_EXPERT_GROUPS, N_EXPERTS // N_EXPERT_GROUPS
    logits = lax.dot_general(wr_ref[...], h2, _NT, precision=lax.Precision.HIGHEST,
                             preferred_element_type=F32)
    scores = jax.nn.sigmoid(logits)
    sel = (scores + rb_ref[...]).reshape(g, w, tm)
    scores3 = scores.reshape(g, w, tm)
    neg = -jnp.inf
    wi = lax.broadcasted_iota(I32, (g, w, tm), 1)
    gi = lax.broadcasted_iota(I32, (g, w, tm), 0)
    m1 = jnp.max(sel, axis=1, keepdims=True)
    first = jnp.min(jnp.where(sel == m1, wi, w), axis=1, keepdims=True)
    m2 = jnp.max(jnp.where(wi == first, neg, sel), axis=1, keepdims=True)
    grp = m1 + m2
    gi1 = lax.broadcasted_iota(I32, (g, 1, tm), 0)
    chosen = jnp.zeros((g, 1, tm), F32)
    for _ in range(TOPK_GROUPS):
        m = jnp.max(grp, axis=0, keepdims=True)
        fg = jnp.min(jnp.where(grp == m, gi1, g), axis=0, keepdims=True)
        hit = gi1 == fg
        chosen = jnp.where(hit, 1.0, chosen)
        grp = jnp.where(hit, neg, grp)
    cand = jnp.where(chosen > 0.5, sel, neg)
    ei = gi * w + wi
    idx_rows, sc_rows = [], []
    for _ in range(TOP_K):
        m = jnp.max(jnp.max(cand, axis=0, keepdims=True), axis=1, keepdims=True)
        fe = jnp.min(jnp.min(jnp.where(cand == m, ei, e), axis=0, keepdims=True), axis=1, keepdims=True)
        hit = ei == fe
        sc = jnp.sum(jnp.sum(jnp.where(hit, scores3, 0.0), axis=0, keepdims=True), axis=1, keepdims=True)
        cand = jnp.where(hit, neg, cand)
        idx_rows.append(fe.reshape(1, tm))
        sc_rows.append(sc.reshape(1, tm))
    idx = jnp.concatenate(idx_rows, axis=0)
    sc = jnp.concatenate(sc_rows, axis=0)
    idx_ref[...] = idx
    gate_ref[...] = sc / jnp.sum(sc, axis=0, keepdims=True) * ROUTED_SCALE


def moe_route(o, x, gt1, np1, np2, sc2, sh2, w_router, router_bias):
    t, d = x.shape
    tm = ROUTE_TILE
    wr_t = w_router.T.astype(F32)
    rb = router_bias.reshape(N_EXPERTS, 1).astype(F32)

    def row(i):
        return (i, 0)

    def fixed(i):
        return (0, 0)

    vec = pl.BlockSpec((1, d), fixed)
    return pl.pallas_call(
        _route_kernel,
        grid=(t // tm,),
        in_specs=[pl.BlockSpec((tm, d), row), pl.BlockSpec((tm, d), row), vec, vec, vec, vec, vec,
                  pl.BlockSpec((N_EXPERTS, d), fixed), pl.BlockSpec((N_EXPERTS, 1), fixed)],
        out_specs=[pl.BlockSpec((tm, d), row), pl.BlockSpec((tm, d), row), pl.BlockSpec((tm, d // 2), row),
                   pl.BlockSpec((TOP_K, tm), lambda i: (0, i)), pl.BlockSpec((TOP_K, tm), lambda i: (0, i))],
        out_shape=[jax.ShapeDtypeStruct((t, d), F32), jax.ShapeDtypeStruct((t, d), BF16),
                   jax.ShapeDtypeStruct((t, d // 2), U32),
                   jax.ShapeDtypeStruct((TOP_K, t), I32), jax.ShapeDtypeStruct((TOP_K, t), F32)],
        compiler_params=_compiler_params(("parallel",)),
        name="moe_route",
    )(o, x, gt1, np1, np2, sc2, sh2, wr_t, rb)


def _plan_kernel(idx_ref, dest_ref, blk_e_ref, pad_end_ref, nused_ref, counts_ref, carry_ref, pstart_ref):
    p = pl.program_id(0)
    i = pl.program_id(1)
    e = N_EXPERTS
    tn = idx_ref.shape[1]
    idx = idx_ref[...]
    ei = lax.broadcasted_iota(I32, (e, tn), 0)
    member = jnp.zeros((e, tn), F32)
    for k in range(TOP_K):
        member = member + jnp.where(idx[k:k + 1, :] == ei, 1.0, 0.0)
    tile_counts = jnp.sum(member, axis=1, keepdims=True)

    @pl.when((p == 0) & (i == 0))
    def _():
        counts_ref[...] = jnp.zeros_like(counts_ref)

    @pl.when(p == 0)
    def _():
        counts_ref[...] += tile_counts

    @pl.when((p == 1) & (i == 0))
    def _():
        counts = counts_ref[...]
        padded = jnp.ceil(counts * (1.0 / MOE_BLOCK)) * MOE_BLOCK
        sub = lax.broadcasted_iota(I32, (e, e), 0)
        lan = lax.broadcasted_iota(I32, (e, e), 1)
        start_row = jnp.sum(jnp.where(sub < lan, padded, 0.0), axis=0, keepdims=True)
        end_row = jnp.sum(jnp.where(sub <= lan, padded, 0.0), axis=0, keepdims=True)
        pstart_ref[...] = jnp.sum(jnp.where(sub == lan, start_row, 0.0), axis=1, keepdims=True)
        carry_ref[...] = jnp.zeros_like(carry_ref)
        pad_end_ref[...] = end_row.astype(I32)
        nb = blk_e_ref.shape[0]
        bstart = (lax.broadcasted_iota(I32, (nb, e), 0) * MOE_BLOCK).astype(F32)
        below = jnp.sum(jnp.where(end_row <= bstart, 1.0, 0.0), axis=1, keepdims=True)
        blk_e_ref[...] = jnp.minimum(below, e - 1.0).astype(I32)
        nused_ref[...] = (end_row[:, e - 1:e] * (1.0 / MOE_BLOCK)).astype(I32)

    @pl.when(p == 1)
    def _():
        tt = lax.broadcasted_iota(I32, (tn, tn), 0)
        tc = lax.broadcasted_iota(I32, (tn, tn), 1)
        before = jnp.where(tt < tc, 1.0, 0.0).astype(BF16)
        prefix = jnp.dot(member.astype(BF16), before, preferred_element_type=F32)
        base = prefix + carry_ref[...] + pstart_ref[...]
        rows = []
        for k in range(TOP_K):
            rows.append(jnp.sum(jnp.where(idx[k:k + 1, :] == ei, base, 0.0), axis=0, keepdims=True))
        dest_ref[...] = jnp.concatenate(rows, axis=0).astype(I32)
        carry_ref[...] += tile_counts


def moe_plan(idx, nb):
    k, t = idx.shape
    tn = PLAN_TILE
    e = N_EXPERTS
    return pl.pallas_call(
        _plan_kernel,
        grid=(2, t // tn),
        in_specs=[pl.BlockSpec((k, tn), lambda p, i: (0, i))],
        out_specs=[pl.BlockSpec((k, tn), lambda p, i: (0, i * p)),
                   pl.BlockSpec((nb, 1), lambda p, i: (0, 0)),
                   pl.BlockSpec((1, e), lambda p, i: (0, 0)),
                   pl.BlockSpec((1, 1), lambda p, i: (0, 0))],
        out_shape=[jax.ShapeDtypeStruct((k, t), I32), jax.ShapeDtypeStruct((nb, 1), I32),
                   jax.ShapeDtypeStruct((1, e), I32), jax.ShapeDtypeStruct((1, 1), I32)],
        scratch_shapes=[pltpu.VMEM((e, 1), F32), pltpu.VMEM((e, 1), F32), pltpu.VMEM((e, 1), F32)],
        compiler_params=_compiler_params(("arbitrary", "arbitrary")),
        name="moe_plan",
    )(idx)


def _dispatch_kernel(dest_ref, pend_ref, h_ref, xs_hbm, zero_ref, sem):
    i = pl.program_id(0)
    tm = h_ref.shape[0]

    def block_clear(b):
        start = pl.multiple_of(b * MOE_BLOCK, MOE_BLOCK)
        return pltpu.make_async_copy(zero_ref, xs_hbm.at[pl.ds(start, MOE_BLOCK), :], sem.at[1])

    @pl.when(i == 0)
    def _():
        zero_ref[...] = jnp.zeros_like(zero_ref)
        shift = int(math.log2(MOE_BLOCK))
        n_blocks = xs_hbm.shape[0] // MOE_BLOCK
        first_unused = lax.shift_right_logical(pend_ref[N_EXPERTS - 1], shift)

        def last_block(e):
            return lax.shift_right_logical(jnp.maximum(pend_ref[e] - MOE_BLOCK, 0), shift)

        def zstart(e, c):
            block_clear(last_block(e)).start()
            return c
        lax.fori_loop(0, N_EXPERTS, zstart, 0)

        def tstart(b, c):
            block_clear(b).start()
            return c
        lax.fori_loop(first_unused, n_blocks, tstart, 0)

        def zwait(e, c):
            block_clear(0).wait()
            return c
        lax.fori_loop(0, N_EXPERTS, zwait, 0)
        lax.fori_loop(first_unused, n_blocks, zwait, 0)

    def row_copy(t, row):
        return pltpu.make_async_copy(h_ref.at[pl.ds(t, 1), :], xs_hbm.at[pl.ds(row, 1), :], sem.at[0])

    def start(t, c):
        for k in range(TOP_K):
            row_copy(t, dest_ref[t * TOP_K + k]).start(priority=k % DMA_PRIORITIES)
        return c
    lax.fori_loop(0, tm, start, 0)

    tile_rows = xs_hbm.at[pl.ds(0, TOP_K * tm), :]
    pltpu.make_async_copy(tile_rows, tile_rows, sem.at[0]).wait()


def moe_dispatch(h_packed, dest_tk, pad_end, rows):
    t, w = h_packed.shape
    tm = DISPATCH_TILE
    dflat = dest_tk.reshape(t * TOP_K)
    return pl.pallas_call(
        _dispatch_kernel,
        grid=(t // tm,),
        in_specs=[pl.BlockSpec((tm * TOP_K,), lambda i: (i,), memory_space=pltpu.SMEM),
                  pl.BlockSpec((N_EXPERTS,), lambda i: (0,), memory_space=pltpu.SMEM),
                  pl.BlockSpec((tm, w), lambda i: (i, 0))],
        out_specs=pl.BlockSpec(memory_space=pl.ANY),
        out_shape=jax.ShapeDtypeStruct((rows, w), U32),
        scratch_shapes=[pltpu.VMEM((MOE_BLOCK, w), U32), pltpu.SemaphoreType.DMA((2,))],
        compiler_params=_compiler_params(("arbitrary",)),
        name="moe_dispatch",
    )(dflat, pad_end.reshape(N_EXPERTS), h_packed)


def _expert_changed(blk_e_ref, i):
    prev = blk_e_ref[jnp.maximum(i - 1, 0)]
    return (i == 0) | (blk_e_ref[i] != prev)


def _gmm1_kernel(blk_e_ref, next_e_ref, nused_ref, x_ref, w1_hbm, w3_hbm, h_ref, st1_ref, st3_ref, w1b_ref, w3b_ref,
                 sem):
    i = pl.program_id(0)

    def weight_copies(e):
        return (pltpu.make_async_copy(w1_hbm.at[e], st1_ref, sem.at[0]),
                pltpu.make_async_copy(w3_hbm.at[e], st3_ref, sem.at[1]))

    @pl.when(i == 0)
    def _():
        for cp in weight_copies(blk_e_ref[0]):
            cp.start()

    @pl.when(i < nused_ref[0])
    def _():
        @pl.when(_expert_changed(blk_e_ref, i))
        def _():
            for cp in weight_copies(blk_e_ref[i]):
                cp.wait()
            w1b_ref[...] = st1_ref[...].astype(BF16)
            w3b_ref[...] = st3_ref[...].astype(BF16)

            @pl.when(next_e_ref[i] >= 0)
            def _():
                for cp in weight_copies(next_e_ref[i]):
                    cp.start()

        lo, hi = _unpack_bf16_pair(x_ref[...])
        lo = lo.astype(BF16)
        hi = hi.astype(BF16)
        half = lo.shape[1]
        g = (jnp.dot(lo, w1b_ref[:half, :], preferred_element_type=F32)
             + jnp.dot(hi, w1b_ref[half:, :], preferred_element_type=F32))
        u = (jnp.dot(lo, w3b_ref[:half, :], preferred_element_type=F32)
             + jnp.dot(hi, w3b_ref[half:, :], preferred_element_type=F32))
        h_ref[...] = (g * jax.nn.sigmoid(g) * u).astype(h_ref.dtype)

    @pl.when(i >= nused_ref[0])
    def _():
        h_ref[...] = jnp.zeros_like(h_ref)


def _next_expert_table(blk_e, n_used):
    nb = blk_e.shape[0]
    pos = jnp.arange(nb, dtype=I32)
    prev = jnp.concatenate([blk_e[:1], blk_e[:-1]])
    starts = (blk_e != prev) & (pos < n_used[0]) & (pos > 0)
    start_pos = jnp.where(starts, pos, nb)
    shifted = jnp.concatenate([start_pos[1:], jnp.full((1,), nb, I32)])
    nxt = lax.cummin(shifted, axis=0, reverse=True)
    return jnp.where(nxt < nb, blk_e[jnp.minimum(nxt, nb - 1)], -1).astype(I32)


def grouped_gate_up(xs, blk_e, next_e, n_used, w1, w3):
    rows, half = xs.shape
    d = 2 * half
    nb = rows // MOE_BLOCK
    f = w1.shape[2]

    def row_blk(i, nu):
        return jnp.minimum(i, nu[0] - 1)

    grid_spec = pltpu.PrefetchScalarGridSpec(
        num_scalar_prefetch=3,
        grid=(nb,),
        in_specs=[pl.BlockSpec((MOE_BLOCK, half), lambda i, e, ne, nu: (row_blk(i, nu), 0)),
                  pl.BlockSpec(memory_space=pl.ANY),
                  pl.BlockSpec(memory_space=pl.ANY)],
        out_specs=pl.BlockSpec((MOE_BLOCK, f), lambda i, e, ne, nu: (i, 0)),
        scratch_shapes=[pltpu.VMEM((d, f), F32), pltpu.VMEM((d, f), F32),
                        pltpu.VMEM((d, f), BF16), pltpu.VMEM((d, f), BF16),
                        pltpu.SemaphoreType.DMA((2,))],
    )
    return pl.pallas_call(
        _gmm1_kernel,
        grid_spec=grid_spec,
        out_shape=jax.ShapeDtypeStruct((rows, f), BF16),
        compiler_params=_compiler_params(("arbitrary",)),
        name="moe_gate_up",
    )(blk_e, next_e, n_used, xs, w1, w3)


def _gmm2_kernel(blk_e_ref, next_e_ref, nused_ref, h_ref, w2_hbm, y_ref, st2_ref, w2b_ref, sem):
    i = pl.program_id(0)

    def weight_copy(e):
        return pltpu.make_async_copy(w2_hbm.at[e], st2_ref, sem.at[0])

    @pl.when(i == 0)
    def _():
        weight_copy(blk_e_ref[0]).start()

    @pl.when(i < nused_ref[0])
    def _():
        @pl.when(_expert_changed(blk_e_ref, i))
        def _():
            weight_copy(blk_e_ref[i]).wait()
            w2b_ref[...] = st2_ref[...].astype(BF16)

            @pl.when(next_e_ref[i] >= 0)
            def _():
                weight_copy(next_e_ref[i]).start()

        y = jnp.dot(h_ref[...], w2b_ref[...], preferred_element_type=F32)
        half = y.shape[1] // 2
        y_ref[...] = _pack_bf16_pair(y[:, :half], y[:, half:])

    @pl.when(i >= nused_ref[0])
    def _():
        y_ref[...] = jnp.zeros_like(y_ref)


def grouped_down(hs, blk_e, next_e, n_used, w2):
    rows, f = hs.shape
    nb = rows // MOE_BLOCK
    d = w2.shape[2]

    def row_blk(i, nu):
        return jnp.minimum(i, nu[0] - 1)

    grid_spec = pltpu.PrefetchScalarGridSpec(
        num_scalar_prefetch=3,
        grid=(nb,),
        in_specs=[pl.BlockSpec((MOE_BLOCK, f), lambda i, e, ne, nu: (row_blk(i, nu), 0)),
                  pl.BlockSpec(memory_space=pl.ANY)],
        out_specs=pl.BlockSpec((MOE_BLOCK, d // 2), lambda i, e, ne, nu: (i, 0)),
        scratch_shapes=[pltpu.VMEM((f, d), F32), pltpu.VMEM((f, d), BF16), pltpu.SemaphoreType.DMA((1,))],
    )
    return pl.pallas_call(
        _gmm2_kernel,
        grid_spec=grid_spec,
        out_shape=jax.ShapeDtypeStruct((rows, d // 2), U32),
        compiler_params=_compiler_params(("arbitrary",)),
        name="moe_down",
    )(blk_e, next_e, n_used, hs, w2)


def _combine_kernel(dcur_ref, dnext_ref, y_hbm, gate_ref, shared_ref, x_ref, gt_ref, nw_ref, o_ref, gbuf, sem):
    i = pl.program_id(0)
    n = pl.num_programs(0)
    tm = x_ref.shape[0]
    slot = lax.rem(i, 2)

    def row_copy(s, src_row, dst_row):
        return pltpu.make_async_copy(y_hbm.at[pl.ds(src_row, 1), :], gbuf.at[s, pl.ds(dst_row, 1), :], sem.at[s])

    def start_tile(d_ref, s):
        def body(t, carry):
            for k in range(TOP_K):
                row_copy(s, d_ref[t * TOP_K + k], k * tm + t).start(priority=k % DMA_PRIORITIES)
            return carry
        lax.fori_loop(0, tm, body, 0)

    @pl.when(i == 0)
    def _():
        start_tile(dcur_ref, 0)

    @pl.when(i + 1 < n)
    def _():
        start_tile(dnext_ref, 1 - slot)

    pltpu.make_async_copy(y_hbm.at[pl.ds(0, TOP_K * tm), :], gbuf.at[slot], sem.at[slot]).wait()

    gate = gate_ref[...]
    half = gbuf.shape[2]
    acc_lo = shared_ref[:, :half].astype(F32)
    acc_hi = shared_ref[:, half:].astype(F32)
    for k in range(TOP_K):
        lo, hi = _unpack_bf16_pair(gbuf[slot, k * tm:(k + 1) * tm, :])
        gk = gate[:, k:k + 1]
        acc_lo = acc_lo + gk * lo
        acc_hi = acc_hi + gk * hi
    ms = (jnp.sum(acc_lo * acc_lo, axis=-1, keepdims=True) + jnp.sum(acc_hi * acc_hi, axis=-1, keepdims=True)) \
        * (1.0 / (2 * half))
    r = lax.rsqrt(ms + EPS)
    o_ref[:, :half] = x_ref[:, :half] + gt_ref[:, :half] * (acc_lo * r * nw_ref[:, :half])
    o_ref[:, half:] = x_ref[:, half:] + gt_ref[:, half:] * (acc_hi * r * nw_ref[:, half:])


def moe_combine(ys, dest_tk, gate_tk, shared, x, gt, nw):
    t, d = x.shape
    tm = COMBINE_TILE
    nt = t // tm
    dflat = dest_tk.reshape(t * TOP_K)
    return pl.pallas_call(
        _combine_kernel,
        grid=(nt,),
        in_specs=[pl.BlockSpec((tm * TOP_K,), lambda i: (i,), memory_space=pltpu.SMEM),
                  pl.BlockSpec((tm * TOP_K,), lambda i: (jnp.minimum(i + 1, nt - 1),), memory_space=pltpu.SMEM),
                  pl.BlockSpec(memory_space=pl.ANY),
                  pl.BlockSpec((tm, TOP_K), lambda i: (i, 0)),
                  pl.BlockSpec((tm, d), lambda i: (i, 0)),
                  pl.BlockSpec((tm, d), lambda i: (i, 0)),
                  pl.BlockSpec((1, d), lambda i: (0, 0)),
                  pl.BlockSpec((1, d), lambda i: (0, 0))],
        out_specs=pl.BlockSpec((tm, d), lambda i: (i, 0)),
        out_shape=jax.ShapeDtypeStruct((t, d), F32),
        scratch_shapes=[pltpu.VMEM((2, TOP_K * tm, d // 2), U32), pltpu.SemaphoreType.DMA((2,))],
        compiler_params=_compiler_params(("arbitrary",)),
        name="moe_combine",
    )(dflat, dflat, ys, gate_tk, shared, x, gt, nw)


def kernel(x, c, w_mod, b_mod, norm_pre1, norm_post1, w_in, conv_w, a_log, dt_bias, dn_norm_w, w_up_dn,
           w_up_swa, w_out, rel_bias, norm_pre2, norm_post2, w_router, router_bias, w1, w3, w2, ws1, ws3, ws2):
    B, S, D = x.shape
    assert B == 1
    T = B * S
    x = x.reshape(T, D)
    depth = w_mod.shape[0]
    n_swa_cols = 3 * SWA_W
    nb = -(-(T * TOP_K) // MOE_BLOCK) + N_EXPERTS
    for layer in range(depth):
        sc = jnp.broadcast_to(jax.nn.silu(c), (8, D))
        mod = dense_matmul(sc, w_mod[layer], tm=8, tn=1024, out_dtype=F32, name="adaln_mod")[:B] + b_mod[layer]
        sh1, sc1, gt1, sh2, sc2, gt2 = jnp.split(mod, 6, axis=-1)

        def vec(v):
            return v[layer].reshape(1, D)

        h = prenorm_modulate(x, vec(norm_pre1), sc1, sh1)
        w_t = jnp.swapaxes(w_in[layer], 0, 1)
        proj_dn = dense_matmul_nt(h, w_t, row_lo=0, row_hi=COL_BA, tm=1024, tn=512, out_dtype=F32,
                                  name="in_proj_dn")
        ba = dense_matmul_nt(h, w_t, row_lo=COL_BA, row_hi=COL_BA + LANES, tm=1024, tn=LANES, out_dtype=F32,
                             name="in_proj_ba")
        gates = dense_matmul_nt(h, w_t, row_lo=COL_GATES, row_hi=COL_END, tm=1024, tn=512, out_dtype=BF16,
                                name="in_proj_gates")
        o_dn = gated_deltanet(proj_dn, ba, conv_w[layer], a_log[layer], dt_bias[layer], dn_norm_w[layer])
        swa_os, swa_ls = [], []
        for gi, (_, dilation) in enumerate(SWA_GROUPS):
            qkv = project_residue_major(h, w_t, dilation=dilation, row_lo=COL_SWA + gi * n_swa_cols,
                                        row_hi=COL_SWA + (gi + 1) * n_swa_cols, tm=1024, tn=512,
                                        name=f"in_proj_swa_d{dilation}")
            heads = slice(gi * SWA_HEADS_PER_GROUP, (gi + 1) * SWA_HEADS_PER_GROUP)
            o_g, lse_g = swa_group(qkv, rel_bias[:, heads], dilation)
            swa_os.append(o_g)
            swa_ls.append(lse_g)
        mixed = mix_up(o_dn, swa_os, swa_ls, gates, w_up_dn[layer], w_up_swa[layer])
        o = dense_matmul(mixed, w_out[layer], tm=1024, tn=512, out_dtype=F32, name="out_proj")
        x1, h2b, h2p, idx, gate = moe_route(o, x, gt1, vec(norm_post1), vec(norm_pre2), sc2, sh2,
                                            w_router[layer], router_bias[layer])
        dest, blk_e, pad_end, n_used = moe_plan(idx, nb)
        dest_tk = dest.T
        gate_tk = gate.T
        xs = moe_dispatch(h2p, dest_tk, pad_end, nb * MOE_BLOCK)
        blk_e = blk_e.reshape(nb)
        n_used = n_used.reshape(1)
        next_e = _next_expert_table(blk_e, n_used)
        hs = grouped_gate_up(xs, blk_e, next_e, n_used, w1[layer], w3[layer])
        ys = grouped_down(hs, blk_e, next_e, n_used, w2[layer])
        hsh = swiglu_up(h2b, ws1[layer], ws3[layer], tm=1024, tn=D_EXPERT // 3, name="shared_gate_up")
        shared = dense_matmul(hsh, ws2[layer], tm=1024, tn=512, out_dtype=F32, name="shared_down")
        x = moe_combine(ys, dest_tk, gate_tk, shared, x1, gt2, vec(norm_post2))
    return x.reshape(B, S, D)
```

```python
import math

import jax
import jax.numpy as jnp
from jax import lax
from jax.experimental import pallas as pl
from jax.experimental.pallas import tpu as pltpu

D_MODEL = 4096
HEAD_DIM = 128
EPS = 1e-6
DN_HEADS = 16
DN_CONV = 4
DN_QK = DN_HEADS * HEAD_DIM
DN_VW = DN_HEADS * HEAD_DIM
SWA_GROUPS = ((128, 1), (512, 4), (2048, 16))
SWA_HEADS_PER_GROUP = 8
SWA_BLOCK = 128
SWA_W = SWA_HEADS_PER_GROUP * HEAD_DIM
REL_BUCKETS = 32
REL_MAX_DIST = 2048
N_EXPERTS = 64
N_EXPERT_GROUPS = 8
TOPK_GROUPS = 4
TOP_K = 8
D_EXPERT = 768
ROUTED_SCALE = 2.5
MOE_BLOCK = 256

COL_BA = 4 * DN_QK
COL_SWA = COL_BA + 2 * DN_HEADS
COL_GATES = COL_SWA + 3 * len(SWA_GROUPS) * SWA_W
COL_END = COL_GATES + 2 * D_MODEL

LANES = 128
SUBLANES = 8
DMA_PRIORITIES = 2
VMEM_LIMIT_BYTES = 56 * 1024 * 1024

DN_TILE = 256
DN_HEADS_PER_STEP = 8
DN_INV_BASE = 16
CONV_PAD = 8
ROUTE_TILE = 256
PLAN_TILE = 512
DISPATCH_TILE = 256
COMBINE_TILE = 128

F32 = jnp.float32
BF16 = jnp.bfloat16
U32 = jnp.uint32
I32 = jnp.int32
HI_MASK = 0xFFFF0000

_NT = (((1,), (1,)), ((), ()))


def _compiler_params(semantics):
    return pltpu.CompilerParams(dimension_semantics=semantics, vmem_limit_bytes=VMEM_LIMIT_BYTES)


def _pack_bf16_pair(lo, hi):
    lo_bits = pltpu.bitcast(lo.astype(BF16).astype(F32), U32) >> jnp.uint32(16)
    hi_bits = pltpu.bitcast(hi.astype(BF16).astype(F32), U32)
    return lo_bits | hi_bits


def _unpack_bf16_pair(w):
    lo = pltpu.bitcast(w << jnp.uint32(16), F32)
    hi = pltpu.bitcast(w & jnp.uint32(HI_MASK), F32)
    return lo, hi


def _mm_kernel(a_ref, b_ref, o_ref):
    a = a_ref[...].astype(BF16)
    b = b_ref[...].astype(BF16)
    o_ref[...] = jnp.dot(a, b, preferred_element_type=F32).astype(o_ref.dtype)


def dense_matmul(a, b, *, tm, tn, out_dtype, name, col_lo=0, col_hi=None):
    m, k = a.shape
    col_hi = b.shape[1] if col_hi is None else col_hi
    n = col_hi - col_lo
    assert m % tm == 0 and col_lo % tn == 0
    off = col_lo // tn
    return pl.pallas_call(
        _mm_kernel,
        grid=(m // tm, pl.cdiv(n, tn)),
        in_specs=[pl.BlockSpec((tm, k), lambda i, j: (i, 0)),
                  pl.BlockSpec((k, tn), lambda i, j: (0, j + off))],
        out_specs=pl.BlockSpec((tm, tn), lambda i, j: (i, j)),
        out_shape=jax.ShapeDtypeStruct((m, n), out_dtype),
        compiler_params=_compiler_params(("parallel", "arbitrary")),
        name=name,
    )(a, b)


def _swiglu_up_kernel(a_ref, wg_ref, wu_ref, o_ref):
    a = a_ref[...]
    g = jnp.dot(a, wg_ref[...].astype(BF16), preferred_element_type=F32)
    u = jnp.dot(a, wu_ref[...].astype(BF16), preferred_element_type=F32)
    o_ref[...] = (g * jax.nn.sigmoid(g) * u).astype(o_ref.dtype)


def swiglu_up(a, wg, wu, *, tm, tn, name):
    m, k = a.shape
    n = wg.shape[1]
    assert m % tm == 0 and n % tn == 0
    return pl.pallas_call(
        _swiglu_up_kernel,
        grid=(m // tm, n // tn),
        in_specs=[pl.BlockSpec((tm, k), lambda i, j: (i, 0)),
                  pl.BlockSpec((k, tn), lambda i, j: (0, j)),
                  pl.BlockSpec((k, tn), lambda i, j: (0, j))],
        out_specs=pl.BlockSpec((tm, tn), lambda i, j: (i, j)),
        out_shape=jax.ShapeDtypeStruct((m, n), BF16),
        compiler_params=_compiler_params(("parallel", "arbitrary")),
        name=name,
    )(a, wg, wu)


def _weight_rows_spec(k, tn, row_lo):
    assert row_lo % SUBLANES == 0 and tn % SUBLANES == 0
    return pl.BlockSpec((pl.Element(tn), pl.Element(k)),
                        lambda i, j: ((row_lo // SUBLANES + j * (tn // SUBLANES)) * SUBLANES, 0))


def _mm_nt_kernel(a_ref, w_ref, o_ref):
    a = a_ref[...].astype(BF16)
    w = w_ref[...].astype(BF16)
    o_ref[...] = lax.dot_general(a, w, _NT, preferred_element_type=F32).astype(o_ref.dtype)


def dense_matmul_nt(a, w_t, *, row_lo, row_hi, tm, tn, out_dtype, name):
    m, k = a.shape
    n = row_hi - row_lo
    assert m % tm == 0 and n % tn == 0
    return pl.pallas_call(
        _mm_nt_kernel,
        grid=(m // tm, n // tn),
        in_specs=[pl.BlockSpec((tm, k), lambda i, j: (i, 0)), _weight_rows_spec(k, tn, row_lo)],
        out_specs=pl.BlockSpec((tm, tn), lambda i, j: (i, j)),
        out_shape=jax.ShapeDtypeStruct((m, n), out_dtype),
        compiler_params=_compiler_params(("parallel", "arbitrary")),
        name=name,
    )(a, w_t)


def _prenorm_kernel(x_ref, nw_ref, sc_ref, sh_ref, h_ref):
    x = x_ref[...]
    y = x * lax.rsqrt(jnp.mean(x * x, axis=-1, keepdims=True) + EPS) * nw_ref[...]
    h_ref[...] = (y * (1.0 + sc_ref[...]) + sh_ref[...]).astype(h_ref.dtype)


def prenorm_modulate(x, nw, sc, sh, *, tm=512):
    t, d = x.shape
    vec = pl.BlockSpec((1, d), lambda i: (0, 0))
    return pl.pallas_call(
        _prenorm_kernel,
        grid=(t // tm,),
        in_specs=[pl.BlockSpec((tm, d), lambda i: (i, 0)), vec, vec, vec],
        out_specs=pl.BlockSpec((tm, d), lambda i: (i, 0)),
        out_shape=jax.ShapeDtypeStruct((t, d), BF16),
        compiler_params=_compiler_params(("parallel",)),
        name="prenorm_modulate",
    )(x, nw, sc, sh)


def _mm_residue_kernel(a_ref, w_ref, o_ref, res_ref):
    d = o_ref.shape[0]
    rows = o_ref.shape[1]
    res = lax.dot_general(a_ref[...], w_ref[...].astype(BF16), _NT, preferred_element_type=F32)
    if d == 1:
        o_ref[0] = res.astype(o_ref.dtype)
        return
    nslab = res.shape[1] // LANES
    for s in range(nslab):
        res_ref[s] = res[:, s * LANES:(s + 1) * LANES]
    for r in range(d):
        for s in range(nslab):
            o_ref[r, :, s * LANES:(s + 1) * LANES] = res_ref[s, pl.ds(r, rows, stride=d), :].astype(o_ref.dtype)


def project_residue_major(a, w_t, *, dilation, row_lo, row_hi, tm, tn, name):
    m, k = a.shape
    n = row_hi - row_lo
    assert m % tm == 0 and n % tn == 0 and tm % (dilation * 16) == 0
    rows = tm // dilation
    return pl.pallas_call(
        _mm_residue_kernel,
        grid=(m // tm, n // tn),
        in_specs=[pl.BlockSpec((tm, k), lambda i, j: (i, 0)), _weight_rows_spec(k, tn, row_lo)],
        out_specs=pl.BlockSpec((dilation, rows, tn), lambda i, j: (0, i, j)),
        out_shape=jax.ShapeDtypeStruct((dilation, m // dilation, n), BF16),
        scratch_shapes=[pltpu.VMEM((tn // LANES, tm, LANES), F32)],
        compiler_params=_compiler_params(("parallel", "arbitrary")),
        name=name,
    )(a, w_t)


_HEAD_BATCH = ((0,), (0,))
_BMM = (((2,), (1,)), _HEAD_BATCH)
_BMM_NT = (((2,), (2,)), _HEAD_BATCH)
_BMM_TN = (((1,), (1,)), _HEAD_BATCH)


def _bdot(a, b, dims=_BMM):
    return lax.dot_general(a.astype(BF16), b.astype(BF16), dims, preferred_element_type=F32)


def _unit_lower_inverse(low, row, col):
    c = low.shape[-1]
    s = DN_INV_BASE
    same = (row // s) == (col // s)
    ld = jnp.where(same, low, 0.0)
    x = jnp.where(row == col, 1.0, 0.0) - ld
    p = _bdot(ld, ld)
    steps = int(math.log2(s)) - 1
    for it in range(steps):
        x = x + _bdot(x, p)
        if it + 1 < steps:
            p = _bdot(p, p)
    while s < c:
        off = ((row // (2 * s)) == (col // (2 * s))) & ((row // s) != (col // s))
        b = jnp.where(off, low, 0.0)
        x = x - _bdot(_bdot(x, b), x)
        s *= 2
    return x


def _dn_kernel(q_ref, k_ref, v_ref, z_ref, cwq_ref, cwk_ref, cwv_ref, bcol_ref, gcol_ref, grow_ref, nw_ref,
               o_ref, state_ref, xe_ref):
    i = pl.program_id(1)
    c = q_ref.shape[0]
    gw = q_ref.shape[1]

    @pl.when(i == 0)
    def _():
        state_ref[...] = jnp.zeros_like(state_ref)
        xe_ref[0:CONV_PAD, :] = jnp.zeros((CONV_PAD, xe_ref.shape[1]), F32)

    xe_ref[CONV_PAD:CONV_PAD + c, 0:gw] = q_ref[...]
    xe_ref[CONV_PAD:CONV_PAD + c, gw:2 * gw] = k_ref[...]
    xe_ref[CONV_PAD:CONV_PAD + c, 2 * gw:3 * gw] = v_ref[...]

    def conv_silu(lo, w_ref):
        acc = w_ref[DN_CONV - 1:DN_CONV, :] * xe_ref[CONV_PAD:CONV_PAD + c, lo:lo + gw]
        for j in range(DN_CONV - 1):
            r0 = CONV_PAD - (DN_CONV - 1) + j
            acc = acc + w_ref[j:j + 1, :] * xe_ref[r0:r0 + c, lo:lo + gw]
        return acc * jax.nn.sigmoid(acc)

    hps = gw // HEAD_DIM

    def heads(x):
        return jnp.stack([x[:, j * HEAD_DIM:(j + 1) * HEAD_DIM] for j in range(hps)])

    qh = heads(conv_silu(0, cwq_ref))
    kh = heads(conv_silu(gw, cwk_ref))
    vh = heads(conv_silu(2 * gw, cwv_ref))
    xe_ref[0:CONV_PAD, :] = xe_ref[c:c + CONV_PAD, :]

    row = lax.broadcasted_iota(I32, (c, c), 0)
    col = lax.broadcasted_iota(I32, (c, c), 1)
    ge = row >= col
    gt = row > col
    bcols, gcols, grows = bcol_ref[0], gcol_ref[0], grow_ref[0]
    bcol = jnp.stack([bcols[:, j:j + 1] for j in range(hps)])
    gcol = jnp.stack([gcols[:, j:j + 1] for j in range(hps)])
    grow = jnp.stack([grows[j:j + 1, :] for j in range(hps)])

    qn = qh * (lax.rsqrt(jnp.sum(qh * qh, axis=-1, keepdims=True) + EPS) * (HEAD_DIM ** -0.5))
    kn = kh * lax.rsqrt(jnp.sum(kh * kh, axis=-1, keepdims=True) + EPS)
    diff = gcol - grow
    decay = jnp.where(ge, jnp.exp(jnp.where(ge, diff, 0.0)), 0.0)
    kb = kn * bcol
    kf = kn.astype(BF16)
    low = jnp.where(gt, _bdot(kb, kf, _BMM_NT) * decay, 0.0)
    attn = _bdot(qn, kf, _BMM_NT) * decay
    tinv = _unit_lower_inverse(low, row, col)
    eg = jnp.exp(gcol)
    rhs = jnp.concatenate([vh * bcol, kb * eg], axis=2)
    uw = _bdot(tinv, rhs)
    u, w = uw[:, :, :HEAD_DIM], uw[:, :, HEAD_DIM:]
    s = state_ref[...]
    sb = s.astype(BF16)
    v_new = u - _bdot(w, sb)
    vnb = v_new.astype(BF16)
    o = _bdot(qn * eg, sb) + _bdot(attn, vnb)
    glast = grow[:, :, c - 1:c]
    kdec = kn * jnp.exp(glast - gcol)
    state_ref[...] = s * jnp.exp(glast) + _bdot(kdec, vnb, _BMM_TN)
    on = o * lax.rsqrt(jnp.mean(o * o, axis=-1, keepdims=True) + EPS) * nw_ref[...]
    for j in range(hps):
        hs = slice(j * HEAD_DIM, (j + 1) * HEAD_DIM)
        zf = z_ref[:, hs]
        o_ref[:, hs] = (on[j] * (zf * jax.nn.sigmoid(zf))).astype(o_ref.dtype)


def _dn_gates(ba, a_log, dt_bias):
    t = ba.shape[0]
    hps = DN_HEADS_PER_STEP
    ng = DN_HEADS // hps
    beta = jax.nn.sigmoid(ba[:, :DN_HEADS])
    g = -jnp.exp(a_log) * jax.nn.softplus(ba[:, DN_HEADS:2 * DN_HEADS] + dt_bias)
    gc = jnp.cumsum(g.reshape(t // DN_TILE, DN_TILE, DN_HEADS), axis=1).reshape(t, DN_HEADS)
    bcol = beta.reshape(t, ng, hps).transpose(1, 0, 2)
    gcol = gc.reshape(t, ng, hps).transpose(1, 0, 2)
    grow = gc.reshape(t, ng, hps).transpose(1, 2, 0)
    return bcol, gcol, grow


def gated_deltanet(proj_dn, ba, conv_w, a_log, dt_bias, norm_w):
    t = proj_dn.shape[0]
    c = DN_TILE
    hps = DN_HEADS_PER_STEP
    gw = hps * HEAD_DIM
    ng = DN_HEADS // hps
    bcol, gcol, grow = _dn_gates(ba, a_log, dt_bias)
    nw = norm_w.reshape(1, HEAD_DIM).astype(F32)
    return pl.pallas_call(
        _dn_kernel,
        grid=(ng, t // c),
        in_specs=[pl.BlockSpec((c, gw), lambda g, i: (i, g)),
                  pl.BlockSpec((c, gw), lambda g, i: (i, ng + g)),
                  pl.BlockSpec((c, gw), lambda g, i: (i, 2 * ng + g)),
                  pl.BlockSpec((c, gw), lambda g, i: (i, 3 * ng + g)),
                  pl.BlockSpec((DN_CONV, gw), lambda g, i: (0, g)),
                  pl.BlockSpec((DN_CONV, gw), lambda g, i: (0, ng + g)),
                  pl.BlockSpec((DN_CONV, gw), lambda g, i: (0, 2 * ng + g)),
                  pl.BlockSpec((1, c, hps), lambda g, i: (g, i, 0)),
                  pl.BlockSpec((1, c, hps), lambda g, i: (g, i, 0)),
                  pl.BlockSpec((1, hps, c), lambda g, i: (g, 0, i)),
                  pl.BlockSpec((1, HEAD_DIM), lambda g, i: (0, 0))],
        out_specs=pl.BlockSpec((c, gw), lambda g, i: (i, g)),
        out_shape=jax.ShapeDtypeStruct((t, DN_VW), BF16),
        scratch_shapes=[pltpu.VMEM((hps, HEAD_DIM, HEAD_DIM), F32),
                        pltpu.VMEM((c + CONV_PAD, 3 * gw), F32)],
        compiler_params=_compiler_params(("parallel", "arbitrary")),
        name="gated_deltanet",
    )(proj_dn, proj_dn, proj_dn, proj_dn, conv_w, conv_w, conv_w, bcol, gcol, grow, nw)


def _swa_kernel(q_ref, kp_ref, kc_ref, vp_ref, vc_ref, bp_ref, bc_ref, o_ref, lse_ref):
    n = pl.program_id(1)
    blk = SWA_BLOCK
    qi = lax.broadcasted_iota(I32, (blk, blk), 0)
    kj = lax.broadcasted_iota(I32, (blk, blk), 1)
    valid_prev = (kj >= qi) & (n > 0)
    valid_cur = kj <= qi
    scale = HEAD_DIM ** -0.5
    nh = SWA_HEADS_PER_GROUP

    def heads(ref):
        return jnp.stack([ref[:, h * HEAD_DIM:(h + 1) * HEAD_DIM] for h in range(nh)])

    q = heads(q_ref)
    sp = lax.dot_general(q, heads(kp_ref), _BMM_NT, preferred_element_type=F32) * scale + bp_ref[...]
    sc = lax.dot_general(q, heads(kc_ref), _BMM_NT, preferred_element_type=F32) * scale + bc_ref[...]
    sp = jnp.where(valid_prev, sp, -jnp.inf)
    sc = jnp.where(valid_cur, sc, -jnp.inf)
    m = jnp.maximum(jnp.max(sp, axis=-1, keepdims=True), jnp.max(sc, axis=-1, keepdims=True))
    pp = jnp.exp(sp - m)
    pc = jnp.exp(sc - m)
    den = jnp.sum(pp, axis=-1, keepdims=True) + jnp.sum(pc, axis=-1, keepdims=True)
    o = (lax.dot_general(pp.astype(BF16), heads(vp_ref), _BMM, preferred_element_type=F32)
         + lax.dot_general(pc.astype(BF16), heads(vc_ref), _BMM, preferred_element_type=F32))
    o = o / den
    lse = m + jnp.log(den)
    lane = lax.broadcasted_iota(I32, (blk, HEAD_DIM), 1)
    lse_all = jnp.zeros((blk, HEAD_DIM), F32)
    for h in range(nh):
        o_ref[:, h * HEAD_DIM:(h + 1) * HEAD_DIM] = o[h].astype(o_ref.dtype)
        lse_all = jnp.where(lane == h, lse[h], lse_all)
    lse_ref[...] = lse_all


def _t5_causal_bucket(dist):
    max_exact = REL_BUCKETS // 2
    d = jnp.maximum(dist, 0)
    log_ratio = jnp.log(jnp.maximum(d, 1).astype(F32) / max_exact) / math.log(REL_MAX_DIST / max_exact)
    large = jnp.minimum(max_exact + (log_ratio * (REL_BUCKETS - max_exact)).astype(I32), REL_BUCKETS - 1)
    return jnp.where(d < max_exact, d, large)


def _swa_bias_blocks(rel_bias_g, dilation):
    blk = SWA_BLOCK
    qi = jnp.arange(blk, dtype=I32)[:, None]
    kj = jnp.arange(blk, dtype=I32)[None, :]
    table = rel_bias_g.astype(F32)

    def lookup(dist):
        onehot = jax.nn.one_hot(_t5_causal_bucket(dist), REL_BUCKETS, dtype=F32)
        return jnp.einsum('qkb,bh->hqk', onehot, table, precision=lax.Precision.HIGHEST)

    return lookup((qi + blk - kj) * dilation), lookup((qi - kj) * dilation)


def swa_group(qkv, rel_bias_g, dilation):
    d, length, _ = qkv.shape
    blk = SWA_BLOCK
    nb = length // blk
    bp, bc = _swa_bias_blocks(rel_bias_g, dilation)

    def prev(n):
        return jnp.maximum(n - 1, 0)

    def blk3(f):
        return pl.BlockSpec((None, blk, SWA_W), f)

    return pl.pallas_call(
        _swa_kernel,
        grid=(d, nb),
        in_specs=[blk3(lambda r, n: (r, n, 0)),
                  blk3(lambda r, n: (r, prev(n), 1)), blk3(lambda r, n: (r, n, 1)),
                  blk3(lambda r, n: (r, prev(n), 2)), blk3(lambda r, n: (r, n, 2)),
                  pl.BlockSpec((SWA_HEADS_PER_GROUP, blk, blk), lambda r, n: (0, 0, 0)),
                  pl.BlockSpec((SWA_HEADS_PER_GROUP, blk, blk), lambda r, n: (0, 0, 0))],
        out_specs=[blk3(lambda r, n: (r, n, 0)),
                   pl.BlockSpec((None, blk, HEAD_DIM), lambda r, n: (r, n, 0))],
        out_shape=[jax.ShapeDtypeStruct((d, length, SWA_W), BF16),
                   jax.ShapeDtypeStruct((d, length, HEAD_DIM), F32)],
        compiler_params=_compiler_params(("parallel", "arbitrary")),
        name=f"swa_d{dilation}",
    )(qkv, qkv, qkv, qkv, qkv, bp, bc)


def _mix_kernel(odn_ref, o1_ref, o4_ref, o16_ref, l1_ref, l4_ref, l16_ref, gd_ref, gs_ref, wd_ref, ws_ref,
                out_ref, a_ref, oscr_ref, lscr_ref):
    j = pl.program_id(1)
    tm = odn_ref.shape[0]
    groups = ((o1_ref, l1_ref), (o4_ref, l4_ref), (o16_ref, l16_ref))

    @pl.when(j == 0)
    def _():
        for g, (o_ref, l_ref) in enumerate(groups):
            d = o_ref.shape[0]
            rows = tm // d
            for r in range(d):
                if d == 1:
                    lscr_ref[g] = l_ref[0]
                else:
                    lscr_ref[g, pl.ds(r, rows, stride=d), :] = l_ref[r]
                for s in range(SWA_HEADS_PER_GROUP):
                    blk = o_ref[r, :, s * HEAD_DIM:(s + 1) * HEAD_DIM].astype(F32)
                    if d == 1:
                        oscr_ref[g, s] = blk
                    else:
                        oscr_ref[g, s, pl.ds(r, rows, stride=d), :] = blk
        ls = [lscr_ref[g] for g in range(len(groups))]
        m = jnp.maximum(jnp.maximum(ls[0], ls[1]), ls[2])
        es = [jnp.exp(l - m) for l in ls]
        inv = 1.0 / (es[0] + es[1] + es[2])
        for s in range(SWA_HEADS_PER_GROUP):
            acc = jnp.zeros((tm, HEAD_DIM), F32)
            for g in range(len(groups)):
                acc = acc + (es[g] * inv)[:, s:s + 1] * oscr_ref[g, s]
            a_ref[:, s * HEAD_DIM:(s + 1) * HEAD_DIM] = acc.astype(a_ref.dtype)

    y_dn = jnp.dot(odn_ref[...], wd_ref[...].astype(BF16), preferred_element_type=F32)
    y_swa = jnp.dot(a_ref[...], ws_ref[...].astype(BF16), preferred_element_type=F32)
    mixed = jax.nn.sigmoid(gd_ref[...].astype(F32)) * y_dn + jax.nn.sigmoid(gs_ref[...].astype(F32)) * y_swa
    out_ref[...] = mixed.astype(out_ref.dtype)


def mix_up(o_dn, swa_os, swa_ls, gates, w_up_dn, w_up_swa, *, tm=1024, tn=256):
    t = o_dn.shape[0]
    dm = w_up_dn.shape[1]
    nj = dm // tn
    in_specs = [pl.BlockSpec((tm, o_dn.shape[1]), lambda i, j: (i, 0))]
    for o in swa_os:
        d = o.shape[0]
        in_specs.append(pl.BlockSpec((d, tm // d, SWA_W), lambda i, j: (0, i, 0)))
    for l in swa_ls:
        d = l.shape[0]
        in_specs.append(pl.BlockSpec((d, tm // d, HEAD_DIM), lambda i, j: (0, i, 0)))
    in_specs += [pl.BlockSpec((tm, tn), lambda i, j: (i, j)),
                 pl.BlockSpec((tm, tn), lambda i, j: (i, nj + j)),
                 pl.BlockSpec((w_up_dn.shape[0], tn), lambda i, j: (0, j)),
                 pl.BlockSpec((w_up_swa.shape[0], tn), lambda i, j: (0, j))]
    ng = len(swa_os)
    return pl.pallas_call(
        _mix_kernel,
        grid=(t // tm, nj),
        in_specs=in_specs,
        out_specs=pl.BlockSpec((tm, tn), lambda i, j: (i, j)),
        out_shape=jax.ShapeDtypeStruct((t, dm), BF16),
        scratch_shapes=[pltpu.VMEM((tm, SWA_W), BF16),
                        pltpu.VMEM((ng, SWA_HEADS_PER_GROUP, tm, HEAD_DIM), F32),
                        pltpu.VMEM((ng, tm, HEAD_DIM), F32)],
        compiler_params=_compiler_params(("parallel", "arbitrary")),
        name="mix_up",
    )(o_dn, *swa_os, *swa_ls, gates, gates, w_up_dn, w_up_swa)


def _route_kernel(o_ref, x_ref, gt1_ref, np1_ref, np2_ref, sc2_ref, sh2_ref, wr_ref, rb_ref,
                  x1_ref, h2b_ref, h2p_ref, idx_ref, gate_ref):
    o = o_ref[...]
    x1 = x_ref[...] + gt1_ref[...] * (o * lax.rsqrt(jnp.mean(o * o, axis=-1, keepdims=True) + EPS) * np1_ref[...])
    x1_ref[...] = x1
    h2 = (x1 * lax.rsqrt(jnp.mean(x1 * x1, axis=-1, keepdims=True) + EPS) * np2_ref[...]) * (1.0 + sc2_ref[...]) \
        + sh2_ref[...]
    h2b_ref[...] = h2.astype(BF16)
    half = h2.shape[1] // 2
    h2p_ref[...] = _pack_bf16_pair(h2[:, :half], h2[:, half:])

    tm = h2.shape[0]
    e, g, w = N_EXPERTS, N_EXPERT_GROUPS, N_EXPERTS // N_EXPERT_GROUPS
    logits = lax.dot_general(wr_ref[...], h2, _NT, precision=lax.Precision.HIGHEST,
                             preferred_element_type=F32)
    scores = jax.nn.sigmoid(logits)
    sel = (scores + rb_ref[...]).reshape(g, w, tm)
    scores3 = scores.reshape(g, w, tm)
    neg = -jnp.inf
    wi = lax.broadcasted_iota(I32, (g, w, tm), 1)
    gi = lax.broadcasted_iota(I32, (g, w, tm), 0)
    m1 = jnp.max(sel, axis=1, keepdims=True)
    first = jnp.min(jnp.where(sel == m1, wi, w), axis=1, keepdims=True)
    m2 = jnp.max(jnp.where(wi == first, neg, sel), axis=1, keepdims=True)
    grp = m1 + m2
    gi1 = lax.broadcasted_iota(I32, (g, 1, tm), 0)
    chosen = jnp.zeros((g, 1, tm), F32)
    for _ in range(TOPK_GROUPS):
        m = jnp.max(grp, axis=0, keepdims=True)
        fg = jnp.min(jnp.where(grp == m, gi1, g), axis=0, keepdims=True)
        hit = gi1 == fg
        chosen = jnp.where(hit, 1.0, chosen)
        grp = jnp.where(hit, neg, grp)
    cand = jnp.where(chosen > 0.5, sel, neg)
    ei = gi * w + wi
    idx_rows, sc_rows = [], []
    for _ in range(TOP_K):
        m = jnp.max(jnp.max(cand, axis=0, keepdims=True), axis=1, keepdims=True)
        fe = jnp.min(jnp.min(jnp.where(cand == m, ei, e), axis=0, keepdims=True), axis=1, keepdims=True)
        hit = ei == fe
        sc = jnp.sum(jnp.sum(jnp.where(hit, scores3, 0.0), axis=0, keepdims=True), axis=1, keepdims=True)
        cand = jnp.where(hit, neg, cand)
        idx_rows.append(fe.reshape(1, tm))
        sc_rows.append(sc.reshape(1, tm))
    idx = jnp.concatenate(idx_rows, axis=0)
    sc = jnp.concatenate(sc_rows, axis=0)
    idx_ref[...] = idx
    gate_ref[...] = sc / jnp.sum(sc, axis=0, keepdims=True) * ROUTED_SCALE


def moe_route(o, x, gt1, np1, np2, sc2, sh2, w_router, router_bias):
    t, d = x.shape
    tm = ROUTE_TILE
    wr_t = w_router.T.astype(F32)
    rb = router_bias.reshape(N_EXPERTS, 1).astype(F32)

    def row(i):
        return (i, 0)

    def fixed(i):
        return (0, 0)

    vec = pl.BlockSpec((1, d), fixed)
    return pl.pallas_call(
        _route_kernel,
        grid=(t // tm,),
        in_specs=[pl.BlockSpec((tm, d), row), pl.BlockSpec((tm, d), row), vec, vec, vec, vec, vec,
                  pl.BlockSpec((N_EXPERTS, d), fixed), pl.BlockSpec((N_EXPERTS, 1), fixed)],
        out_specs=[pl.BlockSpec((tm, d), row), pl.BlockSpec((tm, d), row), pl.BlockSpec((tm, d // 2), row),
                   pl.BlockSpec((TOP_K, tm), lambda i: (0, i)), pl.BlockSpec((TOP_K, tm), lambda i: (0, i))],
        out_shape=[jax.ShapeDtypeStruct((t, d), F32), jax.ShapeDtypeStruct((t, d), BF16),
                   jax.ShapeDtypeStruct((t, d // 2), U32),
                   jax.ShapeDtypeStruct((TOP_K, t), I32), jax.ShapeDtypeStruct((TOP_K, t), F32)],
        compiler_params=_compiler_params(("parallel",)),
        name="moe_route",
    )(o, x, gt1, np1, np2, sc2, sh2, wr_t, rb)


def _plan_kernel(idx_ref, dest_ref, blk_e_ref, pad_end_ref, nused_ref, counts_ref, carry_ref, pstart_ref):
    p = pl.program_id(0)
    i = pl.program_id(1)
    e = N_EXPERTS
    tn = idx_ref.shape[1]
    idx = idx_ref[...]
    ei = lax.broadcasted_iota(I32, (e, tn), 0)
    member = jnp.zeros((e, tn), F32)
    for k in range(TOP_K):
        member = member + jnp.where(idx[k:k + 1, :] == ei, 1.0, 0.0)
    tile_counts = jnp.sum(member, axis=1, keepdims=True)

    @pl.when((p == 0) & (i == 0))
    def _():
        counts_ref[...] = jnp.zeros_like(counts_ref)

    @pl.when(p == 0)
    def _():
        counts_ref[...] += tile_counts

    @pl.when((p == 1) & (i == 0))
    def _():
        counts = counts_ref[...]
        padded = jnp.ceil(counts * (1.0 / MOE_BLOCK)) * MOE_BLOCK
        sub = lax.broadcasted_iota(I32, (e, e), 0)
        lan = lax.broadcasted_iota(I32, (e, e), 1)
        start_row = jnp.sum(jnp.where(sub < lan, padded, 0.0), axis=0, keepdims=True)
        end_row = jnp.sum(jnp.where(sub <= lan, padded, 0.0), axis=0, keepdims=True)
        pstart_ref[...] = jnp.sum(jnp.where(sub == lan, start_row, 0.0), axis=1, keepdims=True)
        carry_ref[...] = jnp.zeros_like(carry_ref)
        pad_end_ref[...] = end_row.astype(I32)
        nb = blk_e_ref.shape[0]
        bstart = (lax.broadcasted_iota(I32, (nb, e), 0) * MOE_BLOCK).astype(F32)
        below = jnp.sum(jnp.where(end_row <= bstart, 1.0, 0.0), axis=1, keepdims=True)
        blk_e_ref[...] = jnp.minimum(below, e - 1.0).astype(I32)
        nused_ref[...] = (end_row[:, e - 1:e] * (1.0 / MOE_BLOCK)).astype(I32)

    @pl.when(p == 1)
    def _():
        tt = lax.broadcasted_iota(I32, (tn, tn), 0)
        tc = lax.broadcasted_iota(I32, (tn, tn), 1)
        before = jnp.where(tt < tc, 1.0, 0.0).astype(BF16)
        prefix = jnp.dot(member.astype(BF16), before, preferred_element_type=F32)
        base = prefix + carry_ref[...] + pstart_ref[...]
        rows = []
        for k in range(TOP_K):
            rows.append(jnp.sum(jnp.where(idx[k:k + 1, :] == ei, base, 0.0), axis=0, keepdims=True))
        dest_ref[...] = jnp.concatenate(rows, axis=0).astype(I32)
        carry_ref[...] += tile_counts


def moe_plan(idx, nb):
    k, t = idx.shape
    tn = PLAN_TILE
    e = N_EXPERTS
    return pl.pallas_call(
        _plan_kernel,
        grid=(2, t // tn),
        in_specs=[pl.BlockSpec((k, tn), lambda p, i: (0, i))],
        out_specs=[pl.BlockSpec((k, tn), lambda p, i: (0, i * p)),
                   pl.BlockSpec((nb, 1), lambda p, i: (0, 0)),
                   pl.BlockSpec((1, e), lambda p, i: (0, 0)),
                   pl.BlockSpec((1, 1), lambda p, i: (0, 0))],
        out_shape=[jax.ShapeDtypeStruct((k, t), I32), jax.ShapeDtypeStruct((nb, 1), I32),
                   jax.ShapeDtypeStruct((1, e), I32), jax.ShapeDtypeStruct((1, 1), I32)],
        scratch_shapes=[pltpu.VMEM((e, 1), F32), pltpu.VMEM((e, 1), F32), pltpu.VMEM((e, 1), F32)],
        compiler_params=_compiler_params(("arbitrary", "arbitrary")),
        name="moe_plan",
    )(idx)


def _dispatch_kernel(dest_ref, pend_ref, h_ref, xs_hbm, zero_ref, sem):
    i = pl.program_id(0)
    tm = h_ref.shape[0]

    def block_clear(b):
        start = pl.multiple_of(b * MOE_BLOCK, MOE_BLOCK)
        return pltpu.make_async_copy(zero_ref, xs_hbm.at[pl.ds(start, MOE_BLOCK), :], sem.at[1])

    @pl.when(i == 0)
    def _():
        zero_ref[...] = jnp.zeros_like(zero_ref)
        shift = int(math.log2(MOE_BLOCK))
        n_blocks = xs_hbm.shape[0] // MOE_BLOCK
        first_unused = lax.shift_right_logical(pend_ref[N_EXPERTS - 1], shift)

        def last_block(e):
            return lax.shift_right_logical(jnp.maximum(pend_ref[e] - MOE_BLOCK, 0), shift)

        def zstart(e, c):
            block_clear(last_block(e)).start()
            return c
        lax.fori_loop(0, N_EXPERTS, zstart, 0)

        def tstart(b, c):
            block_clear(b).start()
            return c
        lax.fori_loop(first_unused, n_blocks, tstart, 0)

        def zwait(e, c):
            block_clear(0).wait()
            return c
        lax.fori_loop(0, N_EXPERTS, zwait, 0)
        lax.fori_loop(first_unused, n_blocks, zwait, 0)

    def row_copy(t, row):
        return pltpu.make_async_copy(h_ref.at[pl.ds(t, 1), :], xs_hbm.at[pl.ds(row, 1), :], sem.at[0])

    def start(t, c):
        for k in range(TOP_K):
            row_copy(t, dest_ref[t * TOP_K + k]).start(priority=k % DMA_PRIORITIES)
        return c
    lax.fori_loop(0, tm, start, 0)

    tile_rows = xs_hbm.at[pl.ds(0, TOP_K * tm), :]
    pltpu.make_async_copy(tile_rows, tile_rows, sem.at[0]).wait()


def moe_dispatch(h_packed, dest_tk, pad_end, rows):
    t, w = h_packed.shape
    tm = DISPATCH_TILE
    dflat = dest_tk.reshape(t * TOP_K)
    return pl.pallas_call(
        _dispatch_kernel,
        grid=(t // tm,),
        in_specs=[pl.BlockSpec((tm * TOP_K,), lambda i: (i,), memory_space=pltpu.SMEM),
                  pl.BlockSpec((N_EXPERTS,), lambda i: (0,), memory_space=pltpu.SMEM),
                  pl.BlockSpec((tm, w), lambda i: (i, 0))],
        out_specs=pl.BlockSpec(memory_space=pl.ANY),
        out_shape=jax.ShapeDtypeStruct((rows, w), U32),
        scratch_shapes=[pltpu.VMEM((MOE_BLOCK, w), U32), pltpu.SemaphoreType.DMA((2,))],
        compiler_params=_compiler_params(("arbitrary",)),
        name="moe_dispatch",
    )(dflat, pad_end.reshape(N_EXPERTS), h_packed)


def _expert_changed(blk_e_ref, i):
    prev = blk_e_ref[jnp.maximum(i - 1, 0)]
    return (i == 0) | (blk_e_ref[i] != prev)


def _gmm1_kernel(blk_e_ref, next_e_ref, nused_ref, x_ref, w1_hbm, w3_hbm, h_ref, st1_ref, st3_ref, w1b_ref, w3b_ref,
                 sem):
    i = pl.program_id(0)

    def weight_copies(e):
        return (pltpu.make_async_copy(w1_hbm.at[e], st1_ref, sem.at[0]),
                pltpu.make_async_copy(w3_hbm.at[e], st3_ref, sem.at[1]))

    @pl.when(i == 0)
    def _():
        for cp in weight_copies(blk_e_ref[0]):
            cp.start()

    @pl.when(i < nused_ref[0])
    def _():
        @pl.when(_expert_changed(blk_e_ref, i))
        def _():
            for cp in weight_copies(blk_e_ref[i]):
                cp.wait()
            w1b_ref[...] = st1_ref[...].astype(BF16)
            w3b_ref[...] = st3_ref[...].astype(BF16)

            @pl.when(next_e_ref[i] >= 0)
            def _():
                for cp in weight_copies(next_e_ref[i]):
                    cp.start()

        lo, hi = _unpack_bf16_pair(x_ref[...])
        lo = lo.astype(BF16)
        hi = hi.astype(BF16)
        half = lo.shape[1]
        g = (jnp.dot(lo, w1b_ref[:half, :], preferred_element_type=F32)
             + jnp.dot(hi, w1b_ref[half:, :], preferred_element_type=F32))
        u = (jnp.dot(lo, w3b_ref[:half, :], preferred_element_type=F32)
             + jnp.dot(hi, w3b_ref[half:, :], preferred_element_type=F32))
        h_ref[...] = (g * jax.nn.sigmoid(g) * u).astype(h_ref.dtype)

    @pl.when(i >= nused_ref[0])
    def _():
        h_ref[...] = jnp.zeros_like(h_ref)


def _expert_run_tables(blk_e, n_used):
    nb = blk_e.shape[0]
    pos = jnp.arange(nb, dtype=I32)
    prev = jnp.concatenate([blk_e[:1], blk_e[:-1]])
    starts = (blk_e != prev) & (pos < n_used[0]) & (pos > 0)
    start_pos = jnp.where(starts, pos, nb)
    shifted = jnp.concatenate([start_pos[1:], jnp.full((1,), nb, I32)])
    nxt = lax.cummin(shifted, axis=0, reverse=True)
    nxt2 = jnp.where(nxt < nb, nxt[jnp.minimum(nxt, nb - 1)], nb)

    def expert_at(p):
        return jnp.where(p < nb, blk_e[jnp.minimum(p, nb - 1)], -1).astype(I32)

    parity = (jnp.cumsum(starts.astype(I32)) % 2).astype(I32)
    return expert_at(nxt), expert_at(nxt2), parity


def grouped_gate_up(xs, blk_e, next_e, n_used, w1, w3):
    rows, half = xs.shape
    d = 2 * half
    nb = rows // MOE_BLOCK
    f = w1.shape[2]

    def row_blk(i, nu):
        return jnp.minimum(i, nu[0] - 1)

    grid_spec = pltpu.PrefetchScalarGridSpec(
        num_scalar_prefetch=3,
        grid=(nb,),
        in_specs=[pl.BlockSpec((MOE_BLOCK, half), lambda i, e, ne, nu: (row_blk(i, nu), 0)),
                  pl.BlockSpec(memory_space=pl.ANY),
                  pl.BlockSpec(memory_space=pl.ANY)],
        out_specs=pl.BlockSpec((MOE_BLOCK, f), lambda i, e, ne, nu: (i, 0)),
        scratch_shapes=[pltpu.VMEM((d, f), F32), pltpu.VMEM((d, f), F32),
                        pltpu.VMEM((d, f), BF16), pltpu.VMEM((d, f), BF16),
                        pltpu.SemaphoreType.DMA((2,))],
    )
    return pl.pallas_call(
        _gmm1_kernel,
        grid_spec=grid_spec,
        out_shape=jax.ShapeDtypeStruct((rows, f), BF16),
        compiler_params=_compiler_params(("arbitrary",)),
        name="moe_gate_up",
    )(blk_e, next_e, n_used, xs, w1, w3)


def _gmm2_kernel(blk_e_ref, next_e_ref, next2_e_ref, parity_ref, nused_ref, h_ref, w2_hbm, y_ref, st2_ref, w2b_ref,
                 sem):
    i = pl.program_id(0)

    def weight_copy(e, slot):
        return pltpu.make_async_copy(w2_hbm.at[e], st2_ref.at[slot], sem.at[slot])

    @pl.when(i == 0)
    def _():
        weight_copy(blk_e_ref[0], 0).start()

        @pl.when(next_e_ref[0] >= 0)
        def _():
            weight_copy(next_e_ref[0], 1).start()

    @pl.when(i < nused_ref[0])
    def _():
        @pl.when(_expert_changed(blk_e_ref, i))
        def _():
            slot = parity_ref[i]
            weight_copy(blk_e_ref[i], slot).wait()
            w2b_ref[...] = st2_ref[slot].astype(BF16)

            @pl.when(next2_e_ref[i] >= 0)
            def _():
                weight_copy(next2_e_ref[i], slot).start()

        y = jnp.dot(h_ref[...], w2b_ref[...], preferred_element_type=F32)
        half = y.shape[1] // 2
        y_ref[...] = _pack_bf16_pair(y[:, :half], y[:, half:])

    @pl.when(i >= nused_ref[0])
    def _():
        y_ref[...] = jnp.zeros_like(y_ref)


def grouped_down(hs, blk_e, next_e, next2_e, parity, n_used, w2):
    rows, f = hs.shape
    nb = rows // MOE_BLOCK
    d = w2.shape[2]

    def row_blk(i, nu):
        return jnp.minimum(i, nu[0] - 1)

    grid_spec = pltpu.PrefetchScalarGridSpec(
        num_scalar_prefetch=5,
        grid=(nb,),
        in_specs=[pl.BlockSpec((MOE_BLOCK, f), lambda i, e, n1, n2, par, nu: (row_blk(i, nu), 0)),
                  pl.BlockSpec(memory_space=pl.ANY)],
        out_specs=pl.BlockSpec((MOE_BLOCK, d // 2), lambda i, e, n1, n2, par, nu: (i, 0)),
        scratch_shapes=[pltpu.VMEM((2, f, d), F32), pltpu.VMEM((f, d), BF16), pltpu.SemaphoreType.DMA((2,))],
    )
    return pl.pallas_call(
        _gmm2_kernel,
        grid_spec=grid_spec,
        out_shape=jax.ShapeDtypeStruct((rows, d // 2), U32),
        compiler_params=_compiler_params(("arbitrary",)),
        name="moe_down",
    )(blk_e, next_e, next2_e, parity, n_used, hs, w2)


def _combine_kernel(dcur_ref, dnext_ref, y_hbm, gate_ref, shared_ref, x_ref, gt_ref, nw_ref, o_ref, gbuf, sem):
    i = pl.program_id(0)
    n = pl.num_programs(0)
    tm = x_ref.shape[0]
    slot = lax.rem(i, 2)

    def row_copy(s, src_row, dst_row):
        return pltpu.make_async_copy(y_hbm.at[pl.ds(src_row, 1), :], gbuf.at[s, pl.ds(dst_row, 1), :], sem.at[s])

    def start_tile(d_ref, s):
        def body(t, carry):
            for k in range(TOP_K):
                row_copy(s, d_ref[t * TOP_K + k], k * tm + t).start(priority=k % DMA_PRIORITIES)
            return carry
        lax.fori_loop(0, tm, body, 0)

    @pl.when(i == 0)
    def _():
        start_tile(dcur_ref, 0)

    @pl.when(i + 1 < n)
    def _():
        start_tile(dnext_ref, 1 - slot)

    pltpu.make_async_copy(y_hbm.at[pl.ds(0, TOP_K * tm), :], gbuf.at[slot], sem.at[slot]).wait()

    gate = gate_ref[...]
    half = gbuf.shape[2]
    acc_lo = shared_ref[:, :half].astype(F32)
    acc_hi = shared_ref[:, half:].astype(F32)
    for k in range(TOP_K):
        lo, hi = _unpack_bf16_pair(gbuf[slot, k * tm:(k + 1) * tm, :])
        gk = gate[:, k:k + 1]
        acc_lo = acc_lo + gk * lo
        acc_hi = acc_hi + gk * hi
    ms = (jnp.sum(acc_lo * acc_lo, axis=-1, keepdims=True) + jnp.sum(acc_hi * acc_hi, axis=-1, keepdims=True)) \
        * (1.0 / (2 * half))
    r = lax.rsqrt(ms + EPS)
    o_ref[:, :half] = x_ref[:, :half] + gt_ref[:, :half] * (acc_lo * r * nw_ref[:, :half])
    o_ref[:, half:] = x_ref[:, half:] + gt_ref[:, half:] * (acc_hi * r * nw_ref[:, half:])


def moe_combine(ys, dest_tk, gate_tk, shared, x, gt, nw):
    t, d = x.shape
    tm = COMBINE_TILE
    nt = t // tm
    dflat = dest_tk.reshape(t * TOP_K)
    return pl.pallas_call(
        _combine_kernel,
        grid=(nt,),
        in_specs=[pl.BlockSpec((tm * TOP_K,), lambda i: (i,), memory_space=pltpu.SMEM),
                  pl.BlockSpec((tm * TOP_K,), lambda i: (jnp.minimum(i + 1, nt - 1),), memory_space=pltpu.SMEM),
                  pl.BlockSpec(memory_space=pl.ANY),
                  pl.BlockSpec((tm, TOP_K), lambda i: (i, 0)),
                  pl.BlockSpec((tm, d), lambda i: (i, 0)),
                  pl.BlockSpec((tm, d), lambda i: (i, 0)),
                  pl.BlockSpec((1, d), lambda i: (0, 0)),
                  pl.BlockSpec((1, d), lambda i: (0, 0))],
        out_specs=pl.BlockSpec((tm, d), lambda i: (i, 0)),
        out_shape=jax.ShapeDtypeStruct((t, d), F32),
        scratch_shapes=[pltpu.VMEM((2, TOP_K * tm, d // 2), U32), pltpu.SemaphoreType.DMA((2,))],
        compiler_params=_compiler_params(("arbitrary",)),
        name="moe_combine",
    )(dflat, dflat, ys, gate_tk, shared, x, gt, nw)


def kernel(x, c, w_mod, b_mod, norm_pre1, norm_post1, w_in, conv_w, a_log, dt_bias, dn_norm_w, w_up_dn,
           w_up_swa, w_out, rel_bias, norm_pre2, norm_post2, w_router, router_bias, w1, w3, w2, ws1, ws3, ws2):
    B, S, D = x.shape
    assert B == 1
    T = B * S
    x = x.reshape(T, D)
    depth = w_mod.shape[0]
    n_swa_cols = 3 * SWA_W
    nb = -(-(T * TOP_K) // MOE_BLOCK) + N_EXPERTS
    for layer in range(depth):
        sc = jnp.broadcast_to(jax.nn.silu(c), (8, D))
        mod = dense_matmul(sc, w_mod[layer], tm=8, tn=1024, out_dtype=F32, name="adaln_mod")[:B] + b_mod[layer]
        sh1, sc1, gt1, sh2, sc2, gt2 = jnp.split(mod, 6, axis=-1)

        def vec(v):
            return v[layer].reshape(1, D)

        h = prenorm_modulate(x, vec(norm_pre1), sc1, sh1)
        w_t = jnp.swapaxes(w_in[layer], 0, 1)
        proj_dn = dense_matmul_nt(h, w_t, row_lo=0, row_hi=COL_BA, tm=1024, tn=512, out_dtype=F32,
                                  name="in_proj_dn")
        ba = dense_matmul_nt(h, w_t, row_lo=COL_BA, row_hi=COL_BA + LANES, tm=1024, tn=LANES, out_dtype=F32,
                             name="in_proj_ba")
        gates = dense_matmul_nt(h, w_t, row_lo=COL_GATES, row_hi=COL_END, tm=1024, tn=512, out_dtype=BF16,
                                name="in_proj_gates")
        o_dn = gated_deltanet(proj_dn, ba, conv_w[layer], a_log[layer], dt_bias[layer], dn_norm_w[layer])
        swa_os, swa_ls = [], []
        for gi, (_, dilation) in enumerate(SWA_GROUPS):
            qkv = project_residue_major(h, w_t, dilation=dilation, row_lo=COL_SWA + gi * n_swa_cols,
                                        row_hi=COL_SWA + (gi + 1) * n_swa_cols, tm=1024, tn=512,
                                        name=f"in_proj_swa_d{dilation}")
            heads = slice(gi * SWA_HEADS_PER_GROUP, (gi + 1) * SWA_HEADS_PER_GROUP)
            o_g, lse_g = swa_group(qkv, rel_bias[:, heads], dilation)
            swa_os.append(o_g)
            swa_ls.append(lse_g)
        mixed = mix_up(o_dn, swa_os, swa_ls, gates, w_up_dn[layer], w_up_swa[layer])
        o = dense_matmul(mixed, w_out[layer], tm=1024, tn=512, out_dtype=F32, name="out_proj")
        x1, h2b, h2p, idx, gate = moe_route(o, x, gt1, vec(norm_post1), vec(norm_pre2), sc2, sh2,
                                            w_router[layer], router_bias[layer])
        dest, blk_e, pad_end, n_used = moe_plan(idx, nb)
        dest_tk = dest.T
        gate_tk = gate.T
        xs = moe_dispatch(h2p, dest_tk, pad_end, nb * MOE_BLOCK)
        blk_e = blk_e.reshape(nb)
        n_used = n_used.reshape(1)
        next_e, next2_e, run_parity = _expert_run_tables(blk_e, n_used)
        hs = grouped_gate_up(xs, blk_e, next_e, n_used, w1[layer], w3[layer])
        ys = grouped_down(hs, blk_e, next_e, next2_e, run_parity, n_used, w2[layer])
        hsh = swiglu_up(h2b, ws1[layer], ws3[layer], tm=1024, tn=D_EXPERT // 3, name="shared_gate_up")
        shared = dense_matmul(hsh, ws2[layer], tm=1024, tn=512, out_dtype=F32, name="shared_down")
        x = moe_combine(ys, dest_tk, gate_tk, shared, x1, gt2, vec(norm_post2))
    return x.reshape(B, S, D)
```

```python
import math

import jax
import jax.numpy as jnp
from jax import lax
from jax.experimental import pallas as pl
from jax.experimental.pallas import tpu as pltpu

D_MODEL = 4096
HEAD_DIM = 128
EPS = 1e-6
DN_HEADS = 16
DN_CONV = 4
DN_QK = DN_HEADS * HEAD_DIM
DN_VW = DN_HEADS * HEAD_DIM
SWA_GROUPS = ((128, 1), (512, 4), (2048, 16))
SWA_HEADS_PER_GROUP = 8
SWA_BLOCK = 128
SWA_W = SWA_HEADS_PER_GROUP * HEAD_DIM
REL_BUCKETS = 32
REL_MAX_DIST = 2048
N_EXPERTS = 64
N_EXPERT_GROUPS = 8
TOPK_GROUPS = 4
TOP_K = 8
D_EXPERT = 768
ROUTED_SCALE = 2.5
MOE_BLOCK = 256

COL_BA = 4 * DN_QK
COL_SWA = COL_BA + 2 * DN_HEADS
COL_GATES = COL_SWA + 3 * len(SWA_GROUPS) * SWA_W
COL_END = COL_GATES + 2 * D_MODEL

LANES = 128
SUBLANES = 8
DMA_PRIORITIES = 2
WEIGHT_STREAM_PRIORITY = 1
VMEM_LIMIT_BYTES = 56 * 1024 * 1024

DN_TILE = 256
DN_HEADS_PER_STEP = 8
DN_INV_BASE = 16
CONV_PAD = 8
ROUTE_TILE = 256
PLAN_TILE = 512
DISPATCH_TILE = 256
COMBINE_TILE = 128

F32 = jnp.float32
BF16 = jnp.bfloat16
U32 = jnp.uint32
I32 = jnp.int32
HI_MASK = 0xFFFF0000

_NT = (((1,), (1,)), ((), ()))


def _compiler_params(semantics):
    return pltpu.CompilerParams(dimension_semantics=semantics, vmem_limit_bytes=VMEM_LIMIT_BYTES)


def _pack_bf16_pair(lo, hi):
    lo_bits = pltpu.bitcast(lo.astype(BF16).astype(F32), U32) >> jnp.uint32(16)
    hi_bits = pltpu.bitcast(hi.astype(BF16).astype(F32), U32)
    return lo_bits | hi_bits


def _unpack_bf16_pair(w):
    lo = pltpu.bitcast(w << jnp.uint32(16), F32)
    hi = pltpu.bitcast(w & jnp.uint32(HI_MASK), F32)
    return lo, hi


def _mm_kernel(a_ref, b_ref, o_ref):
    a = a_ref[...].astype(BF16)
    b = b_ref[...].astype(BF16)
    o_ref[...] = jnp.dot(a, b, preferred_element_type=F32).astype(o_ref.dtype)


def dense_matmul(a, b, *, tm, tn, out_dtype, name, col_lo=0, col_hi=None):
    m, k = a.shape
    col_hi = b.shape[1] if col_hi is None else col_hi
    n = col_hi - col_lo
    assert m % tm == 0 and col_lo % tn == 0
    off = col_lo // tn
    return pl.pallas_call(
        _mm_kernel,
        grid=(m // tm, pl.cdiv(n, tn)),
        in_specs=[pl.BlockSpec((tm, k), lambda i, j: (i, 0)),
                  pl.BlockSpec((k, tn), lambda i, j: (0, j + off))],
        out_specs=pl.BlockSpec((tm, tn), lambda i, j: (i, j)),
        out_shape=jax.ShapeDtypeStruct((m, n), out_dtype),
        compiler_params=_compiler_params(("parallel", "arbitrary")),
        name=name,
    )(a, b)


def _swiglu_up_kernel(a_ref, wg_ref, wu_ref, o_ref):
    a = a_ref[...]
    g = jnp.dot(a, wg_ref[...].astype(BF16), preferred_element_type=F32)
    u = jnp.dot(a, wu_ref[...].astype(BF16), preferred_element_type=F32)
    o_ref[...] = (g * jax.nn.sigmoid(g) * u).astype(o_ref.dtype)


def swiglu_up(a, wg, wu, *, tm, tn, name):
    m, k = a.shape
    n = wg.shape[1]
    assert m % tm == 0 and n % tn == 0
    return pl.pallas_call(
        _swiglu_up_kernel,
        grid=(m // tm, n // tn),
        in_specs=[pl.BlockSpec((tm, k), lambda i, j: (i, 0)),
                  pl.BlockSpec((k, tn), lambda i, j: (0, j)),
                  pl.BlockSpec((k, tn), lambda i, j: (0, j))],
        out_specs=pl.BlockSpec((tm, tn), lambda i, j: (i, j)),
        out_shape=jax.ShapeDtypeStruct((m, n), BF16),
        compiler_params=_compiler_params(("parallel", "arbitrary")),
        name=name,
    )(a, wg, wu)


def _weight_rows_spec(k, tn, row_lo):
    assert row_lo % SUBLANES == 0 and tn % SUBLANES == 0
    return pl.BlockSpec((pl.Element(tn), pl.Element(k)),
                        lambda i, j: ((row_lo // SUBLANES + j * (tn // SUBLANES)) * SUBLANES, 0))


def _mm_nt_kernel(a_ref, w_ref, o_ref):
    a = a_ref[...].astype(BF16)
    w = w_ref[...].astype(BF16)
    o_ref[...] = lax.dot_general(a, w, _NT, preferred_element_type=F32).astype(o_ref.dtype)


def dense_matmul_nt(a, w_t, *, row_lo, row_hi, tm, tn, out_dtype, name):
    m, k = a.shape
    n = row_hi - row_lo
    assert m % tm == 0 and n % tn == 0
    return pl.pallas_call(
        _mm_nt_kernel,
        grid=(m // tm, n // tn),
        in_specs=[pl.BlockSpec((tm, k), lambda i, j: (i, 0)), _weight_rows_spec(k, tn, row_lo)],
        out_specs=pl.BlockSpec((tm, tn), lambda i, j: (i, j)),
        out_shape=jax.ShapeDtypeStruct((m, n), out_dtype),
        compiler_params=_compiler_params(("parallel", "arbitrary")),
        name=name,
    )(a, w_t)


def _prenorm_kernel(x_ref, nw_ref, sc_ref, sh_ref, h_ref):
    x = x_ref[...]
    y = x * lax.rsqrt(jnp.mean(x * x, axis=-1, keepdims=True) + EPS) * nw_ref[...]
    h_ref[...] = (y * (1.0 + sc_ref[...]) + sh_ref[...]).astype(h_ref.dtype)


def prenorm_modulate(x, nw, sc, sh, *, tm=512):
    t, d = x.shape
    vec = pl.BlockSpec((1, d), lambda i: (0, 0))
    return pl.pallas_call(
        _prenorm_kernel,
        grid=(t // tm,),
        in_specs=[pl.BlockSpec((tm, d), lambda i: (i, 0)), vec, vec, vec],
        out_specs=pl.BlockSpec((tm, d), lambda i: (i, 0)),
        out_shape=jax.ShapeDtypeStruct((t, d), BF16),
        compiler_params=_compiler_params(("parallel",)),
        name="prenorm_modulate",
    )(x, nw, sc, sh)


def _mm_residue_kernel(a_ref, w_ref, o_ref, res_ref):
    d = o_ref.shape[0]
    rows = o_ref.shape[1]
    res = lax.dot_general(a_ref[...], w_ref[...].astype(BF16), _NT, preferred_element_type=F32)
    if d == 1:
        o_ref[0] = res.astype(o_ref.dtype)
        return
    nslab = res.shape[1] // LANES
    for s in range(nslab):
        res_ref[s] = res[:, s * LANES:(s + 1) * LANES]
    for r in range(d):
        for s in range(nslab):
            o_ref[r, :, s * LANES:(s + 1) * LANES] = res_ref[s, pl.ds(r, rows, stride=d), :].astype(o_ref.dtype)


def project_residue_major(a, w_t, *, dilation, row_lo, row_hi, tm, tn, name):
    m, k = a.shape
    n = row_hi - row_lo
    assert m % tm == 0 and n % tn == 0 and tm % (dilation * 16) == 0
    rows = tm // dilation
    return pl.pallas_call(
        _mm_residue_kernel,
        grid=(m // tm, n // tn),
        in_specs=[pl.BlockSpec((tm, k), lambda i, j: (i, 0)), _weight_rows_spec(k, tn, row_lo)],
        out_specs=pl.BlockSpec((dilation, rows, tn), lambda i, j: (0, i, j)),
        out_shape=jax.ShapeDtypeStruct((dilation, m // dilation, n), BF16),
        scratch_shapes=[pltpu.VMEM((tn // LANES, tm, LANES), F32)],
        compiler_params=_compiler_params(("parallel", "arbitrary")),
        name=name,
    )(a, w_t)


_HEAD_BATCH = ((0,), (0,))
_BMM = (((2,), (1,)), _HEAD_BATCH)
_BMM_NT = (((2,), (2,)), _HEAD_BATCH)
_BMM_TN = (((1,), (1,)), _HEAD_BATCH)


def _bdot(a, b, dims=_BMM):
    return lax.dot_general(a.astype(BF16), b.astype(BF16), dims, preferred_element_type=F32)


def _unit_lower_inverse(low, row, col):
    c = low.shape[-1]
    s = DN_INV_BASE
    same = (row // s) == (col // s)
    ld = jnp.where(same, low, 0.0)
    x = jnp.where(row == col, 1.0, 0.0) - ld
    p = _bdot(ld, ld)
    steps = int(math.log2(s)) - 1
    for it in range(steps):
        x = x + _bdot(x, p)
        if it + 1 < steps:
            p = _bdot(p, p)
    while s < c:
        off = ((row // (2 * s)) == (col // (2 * s))) & ((row // s) != (col // s))
        b = jnp.where(off, low, 0.0)
        x = x - _bdot(_bdot(x, b), x)
        s *= 2
    return x


def _dn_kernel(q_ref, k_ref, v_ref, z_ref, cwq_ref, cwk_ref, cwv_ref, bcol_ref, gcol_ref, grow_ref, nw_ref,
               o_ref, state_ref, xe_ref):
    i = pl.program_id(1)
    c = q_ref.shape[0]
    gw = q_ref.shape[1]

    @pl.when(i == 0)
    def _():
        state_ref[...] = jnp.zeros_like(state_ref)
        xe_ref[0:CONV_PAD, :] = jnp.zeros((CONV_PAD, xe_ref.shape[1]), F32)

    xe_ref[CONV_PAD:CONV_PAD + c, 0:gw] = q_ref[...]
    xe_ref[CONV_PAD:CONV_PAD + c, gw:2 * gw] = k_ref[...]
    xe_ref[CONV_PAD:CONV_PAD + c, 2 * gw:3 * gw] = v_ref[...]

    def conv_silu(lo, w_ref):
        acc = w_ref[DN_CONV - 1:DN_CONV, :] * xe_ref[CONV_PAD:CONV_PAD + c, lo:lo + gw]
        for j in range(DN_CONV - 1):
            r0 = CONV_PAD - (DN_CONV - 1) + j
            acc = acc + w_ref[j:j + 1, :] * xe_ref[r0:r0 + c, lo:lo + gw]
        return acc * jax.nn.sigmoid(acc)

    hps = gw // HEAD_DIM

    def heads(x):
        return jnp.stack([x[:, j * HEAD_DIM:(j + 1) * HEAD_DIM] for j in range(hps)])

    qh = heads(conv_silu(0, cwq_ref))
    kh = heads(conv_silu(gw, cwk_ref))
    vh = heads(conv_silu(2 * gw, cwv_ref))
    xe_ref[0:CONV_PAD, :] = xe_ref[c:c + CONV_PAD, :]

    row = lax.broadcasted_iota(I32, (c, c), 0)
    col = lax.broadcasted_iota(I32, (c, c), 1)
    ge = row >= col
    gt = row > col
    bcols, gcols, grows = bcol_ref[0], gcol_ref[0], grow_ref[0]
    bcol = jnp.stack([bcols[:, j:j + 1] for j in range(hps)])
    gcol = jnp.stack([gcols[:, j:j + 1] for j in range(hps)])
    grow = jnp.stack([grows[j:j + 1, :] for j in range(hps)])

    qn = qh * (lax.rsqrt(jnp.sum(qh * qh, axis=-1, keepdims=True) + EPS) * (HEAD_DIM ** -0.5))
    kn = kh * lax.rsqrt(jnp.sum(kh * kh, axis=-1, keepdims=True) + EPS)
    diff = gcol - grow
    decay = jnp.where(ge, jnp.exp(jnp.where(ge, diff, 0.0)), 0.0)
    kb = kn * bcol
    kf = kn.astype(BF16)
    low = jnp.where(gt, _bdot(kb, kf, _BMM_NT) * decay, 0.0)
    attn = _bdot(qn, kf, _BMM_NT) * decay
    tinv = _unit_lower_inverse(low, row, col)
    eg = jnp.exp(gcol)
    rhs = jnp.concatenate([vh * bcol, kb * eg], axis=2)
    uw = _bdot(tinv, rhs)
    u, w = uw[:, :, :HEAD_DIM], uw[:, :, HEAD_DIM:]
    s = state_ref[...]
    sb = s.astype(BF16)
    v_new = u - _bdot(w, sb)
    vnb = v_new.astype(BF16)
    o = _bdot(qn * eg, sb) + _bdot(attn, vnb)
    glast = grow[:, :, c - 1:c]
    kdec = kn * jnp.exp(glast - gcol)
    state_ref[...] = s * jnp.exp(glast) + _bdot(kdec, vnb, _BMM_TN)
    on = o * lax.rsqrt(jnp.mean(o * o, axis=-1, keepdims=True) + EPS) * nw_ref[...]
    for j in range(hps):
        hs = slice(j * HEAD_DIM, (j + 1) * HEAD_DIM)
        zf = z_ref[:, hs]
        o_ref[:, hs] = (on[j] * (zf * jax.nn.sigmoid(zf))).astype(o_ref.dtype)


def _dn_gates(ba, a_log, dt_bias):
    t = ba.shape[0]
    hps = DN_HEADS_PER_STEP
    ng = DN_HEADS // hps
    beta = jax.nn.sigmoid(ba[:, :DN_HEADS])
    g = -jnp.exp(a_log) * jax.nn.softplus(ba[:, DN_HEADS:2 * DN_HEADS] + dt_bias)
    gc = jnp.cumsum(g.reshape(t // DN_TILE, DN_TILE, DN_HEADS), axis=1).reshape(t, DN_HEADS)
    bcol = beta.reshape(t, ng, hps).transpose(1, 0, 2)
    gcol = gc.reshape(t, ng, hps).transpose(1, 0, 2)
    grow = gc.reshape(t, ng, hps).transpose(1, 2, 0)
    return bcol, gcol, grow


def gated_deltanet(proj_dn, ba, conv_w, a_log, dt_bias, norm_w):
    t = proj_dn.shape[0]
    c = DN_TILE
    hps = DN_HEADS_PER_STEP
    gw = hps * HEAD_DIM
    ng = DN_HEADS // hps
    bcol, gcol, grow = _dn_gates(ba, a_log, dt_bias)
    nw = norm_w.reshape(1, HEAD_DIM).astype(F32)
    return pl.pallas_call(
        _dn_kernel,
        grid=(ng, t // c),
        in_specs=[pl.BlockSpec((c, gw), lambda g, i: (i, g)),
                  pl.BlockSpec((c, gw), lambda g, i: (i, ng + g)),
                  pl.BlockSpec((c, gw), lambda g, i: (i, 2 * ng + g)),
                  pl.BlockSpec((c, gw), lambda g, i: (i, 3 * ng + g)),
                  pl.BlockSpec((DN_CONV, gw), lambda g, i: (0, g)),
                  pl.BlockSpec((DN_CONV, gw), lambda g, i: (0, ng + g)),
                  pl.BlockSpec((DN_CONV, gw), lambda g, i: (0, 2 * ng + g)),
                  pl.BlockSpec((1, c, hps), lambda g, i: (g, i, 0)),
                  pl.BlockSpec((1, c, hps), lambda g, i: (g, i, 0)),
                  pl.BlockSpec((1, hps, c), lambda g, i: (g, 0, i)),
                  pl.BlockSpec((1, HEAD_DIM), lambda g, i: (0, 0))],
        out_specs=pl.BlockSpec((c, gw), lambda g, i: (i, g)),
        out_shape=jax.ShapeDtypeStruct((t, DN_VW), BF16),
        scratch_shapes=[pltpu.VMEM((hps, HEAD_DIM, HEAD_DIM), F32),
                        pltpu.VMEM((c + CONV_PAD, 3 * gw), F32)],
        compiler_params=_compiler_params(("parallel", "arbitrary")),
        name="gated_deltanet",
    )(proj_dn, proj_dn, proj_dn, proj_dn, conv_w, conv_w, conv_w, bcol, gcol, grow, nw)


def _swa_kernel(q_ref, kp_ref, kc_ref, vp_ref, vc_ref, bp_ref, bc_ref, o_ref, lse_ref):
    n = pl.program_id(1)
    blk = SWA_BLOCK
    qi = lax.broadcasted_iota(I32, (blk, blk), 0)
    kj = lax.broadcasted_iota(I32, (blk, blk), 1)
    valid_prev = (kj >= qi) & (n > 0)
    valid_cur = kj <= qi
    scale = HEAD_DIM ** -0.5
    nh = SWA_HEADS_PER_GROUP

    def heads(ref):
        return jnp.stack([ref[:, h * HEAD_DIM:(h + 1) * HEAD_DIM] for h in range(nh)])

    q = heads(q_ref)
    sp = lax.dot_general(q, heads(kp_ref), _BMM_NT, preferred_element_type=F32) * scale + bp_ref[...]
    sc = lax.dot_general(q, heads(kc_ref), _BMM_NT, preferred_element_type=F32) * scale + bc_ref[...]
    sp = jnp.where(valid_prev, sp, -jnp.inf)
    sc = jnp.where(valid_cur, sc, -jnp.inf)
    m = jnp.maximum(jnp.max(sp, axis=-1, keepdims=True), jnp.max(sc, axis=-1, keepdims=True))
    pp = jnp.exp(sp - m)
    pc = jnp.exp(sc - m)
    den = jnp.sum(pp, axis=-1, keepdims=True) + jnp.sum(pc, axis=-1, keepdims=True)
    o = (lax.dot_general(pp.astype(BF16), heads(vp_ref), _BMM, preferred_element_type=F32)
         + lax.dot_general(pc.astype(BF16), heads(vc_ref), _BMM, preferred_element_type=F32))
    o = o / den
    lse = m + jnp.log(den)
    lane = lax.broadcasted_iota(I32, (blk, HEAD_DIM), 1)
    lse_all = jnp.zeros((blk, HEAD_DIM), F32)
    for h in range(nh):
        o_ref[:, h * HEAD_DIM:(h + 1) * HEAD_DIM] = o[h].astype(o_ref.dtype)
        lse_all = jnp.where(lane == h, lse[h], lse_all)
    lse_ref[...] = lse_all


def _t5_causal_bucket(dist):
    max_exact = REL_BUCKETS // 2
    d = jnp.maximum(dist, 0)
    log_ratio = jnp.log(jnp.maximum(d, 1).astype(F32) / max_exact) / math.log(REL_MAX_DIST / max_exact)
    large = jnp.minimum(max_exact + (log_ratio * (REL_BUCKETS - max_exact)).astype(I32), REL_BUCKETS - 1)
    return jnp.where(d < max_exact, d, large)


def _swa_bias_blocks(rel_bias_g, dilation):
    blk = SWA_BLOCK
    qi = jnp.arange(blk, dtype=I32)[:, None]
    kj = jnp.arange(blk, dtype=I32)[None, :]
    table = rel_bias_g.astype(F32)

    def lookup(dist):
        onehot = jax.nn.one_hot(_t5_causal_bucket(dist), REL_BUCKETS, dtype=F32)
        return jnp.einsum('qkb,bh->hqk', onehot, table, precision=lax.Precision.HIGHEST)

    return lookup((qi + blk - kj) * dilation), lookup((qi - kj) * dilation)


def swa_group(qkv, rel_bias_g, dilation):
    d, length, _ = qkv.shape
    blk = SWA_BLOCK
    nb = length // blk
    bp, bc = _swa_bias_blocks(rel_bias_g, dilation)

    def prev(n):
        return jnp.maximum(n - 1, 0)

    def blk3(f):
        return pl.BlockSpec((None, blk, SWA_W), f)

    return pl.pallas_call(
        _swa_kernel,
        grid=(d, nb),
        in_specs=[blk3(lambda r, n: (r, n, 0)),
                  blk3(lambda r, n: (r, prev(n), 1)), blk3(lambda r, n: (r, n, 1)),
                  blk3(lambda r, n: (r, prev(n), 2)), blk3(lambda r, n: (r, n, 2)),
                  pl.BlockSpec((SWA_HEADS_PER_GROUP, blk, blk), lambda r, n: (0, 0, 0)),
                  pl.BlockSpec((SWA_HEADS_PER_GROUP, blk, blk), lambda r, n: (0, 0, 0))],
        out_specs=[blk3(lambda r, n: (r, n, 0)),
                   pl.BlockSpec((None, blk, HEAD_DIM), lambda r, n: (r, n, 0))],
        out_shape=[jax.ShapeDtypeStruct((d, length, SWA_W), BF16),
                   jax.ShapeDtypeStruct((d, length, HEAD_DIM), F32)],
        compiler_params=_compiler_params(("parallel", "arbitrary")),
        name=f"swa_d{dilation}",
    )(qkv, qkv, qkv, qkv, qkv, bp, bc)


def _mix_kernel(odn_ref, o1_ref, o4_ref, o16_ref, l1_ref, l4_ref, l16_ref, gd_ref, gs_ref, wd_ref, ws_ref,
                out_ref, a_ref, oscr_ref, lscr_ref):
    j = pl.program_id(1)
    tm = odn_ref.shape[0]
    groups = ((o1_ref, l1_ref), (o4_ref, l4_ref), (o16_ref, l16_ref))

    @pl.when(j == 0)
    def _():
        for g, (o_ref, l_ref) in enumerate(groups):
            d = o_ref.shape[0]
            rows = tm // d
            for r in range(d):
                if d == 1:
                    lscr_ref[g] = l_ref[0]
                else:
                    lscr_ref[g, pl.ds(r, rows, stride=d), :] = l_ref[r]
                for s in range(SWA_HEADS_PER_GROUP):
                    blk = o_ref[r, :, s * HEAD_DIM:(s + 1) * HEAD_DIM].astype(F32)
                    if d == 1:
                        oscr_ref[g, s] = blk
                    else:
                        oscr_ref[g, s, pl.ds(r, rows, stride=d), :] = blk
        ls = [lscr_ref[g] for g in range(len(groups))]
        m = jnp.maximum(jnp.maximum(ls[0], ls[1]), ls[2])
        es = [jnp.exp(l - m) for l in ls]
        inv = 1.0 / (es[0] + es[1] + es[2])
        for s in range(SWA_HEADS_PER_GROUP):
            acc = jnp.zeros((tm, HEAD_DIM), F32)
            for g in range(len(groups)):
                acc = acc + (es[g] * inv)[:, s:s + 1] * oscr_ref[g, s]
            a_ref[:, s * HEAD_DIM:(s + 1) * HEAD_DIM] = acc.astype(a_ref.dtype)

    y_dn = jnp.dot(odn_ref[...], wd_ref[...].astype(BF16), preferred_element_type=F32)
    y_swa = jnp.dot(a_ref[...], ws_ref[...].astype(BF16), preferred_element_type=F32)
    mixed = jax.nn.sigmoid(gd_ref[...].astype(F32)) * y_dn + jax.nn.sigmoid(gs_ref[...].astype(F32)) * y_swa
    out_ref[...] = mixed.astype(out_ref.dtype)


def mix_up(o_dn, swa_os, swa_ls, gates, w_up_dn, w_up_swa, *, tm=1024, tn=256):
    t = o_dn.shape[0]
    dm = w_up_dn.shape[1]
    nj = dm // tn
    in_specs = [pl.BlockSpec((tm, o_dn.shape[1]), lambda i, j: (i, 0))]
    for o in swa_os:
        d = o.shape[0]
        in_specs.append(pl.BlockSpec((d, tm // d, SWA_W), lambda i, j: (0, i, 0)))
    for l in swa_ls:
        d = l.shape[0]
        in_specs.append(pl.BlockSpec((d, tm // d, HEAD_DIM), lambda i, j: (0, i, 0)))
    in_specs += [pl.BlockSpec((tm, tn), lambda i, j: (i, j)),
                 pl.BlockSpec((tm, tn), lambda i, j: (i, nj + j)),
                 pl.BlockSpec((w_up_dn.shape[0], tn), lambda i, j: (0, j)),
                 pl.BlockSpec((w_up_swa.shape[0], tn), lambda i, j: (0, j))]
    ng = len(swa_os)
    return pl.pallas_call(
        _mix_kernel,
        grid=(t // tm, nj),
        in_specs=in_specs,
        out_specs=pl.BlockSpec((tm, tn), lambda i, j: (i, j)),
        out_shape=jax.ShapeDtypeStruct((t, dm), BF16),
        scratch_shapes=[pltpu.VMEM((tm, SWA_W), BF16),
                        pltpu.VMEM((ng, SWA_HEADS_PER_GROUP, tm, HEAD_DIM), F32),
                        pltpu.VMEM((ng, tm, HEAD_DIM), F32)],
        compiler_params=_compiler_params(("parallel", "arbitrary")),
        name="mix_up",
    )(o_dn, *swa_os, *swa_ls, gates, gates, w_up_dn, w_up_swa)


def _route_kernel(o_ref, x_ref, gt1_ref, np1_ref, np2_ref, sc2_ref, sh2_ref, wr_ref, rb_ref,
                  x1_ref, h2b_ref, h2p_ref, idx_ref, gate_ref):
    o = o_ref[...]
    x1 = x_ref[...] + gt1_ref[...] * (o * lax.rsqrt(jnp.mean(o * o, axis=-1, keepdims=True) + EPS) * np1_ref[...])
    x1_ref[...] = x1
    h2 = (x1 * lax.rsqrt(jnp.mean(x1 * x1, axis=-1, keepdims=True) + EPS) * np2_ref[...]) * (1.0 + sc2_ref[...]) \
        + sh2_ref[...]
    h2b_ref[...] = h2.astype(BF16)
    half = h2.shape[1] // 2
    h2p_ref[...] = _pack_bf16_pair(h2[:, :half], h2[:, half:])

    tm = h2.shape[0]
    e, g, w = N_EXPERTS, N_EXPERT_GROUPS, N_EXPERTS // N_EXPERT_GROUPS
    logits = lax.dot_general(wr_ref[...], h2, _NT, precision=lax.Precision.HIGHEST,
                             preferred_element_type=F32)
    scores = jax.nn.sigmoid(logits)
    sel = (scores + rb_ref[...]).reshape(g, w, tm)
    scores3 = scores.reshape(g, w, tm)
    neg = -jnp.inf
    wi = lax.broadcasted_iota(I32, (g, w, tm), 1)
    gi = lax.broadcasted_iota(I32, (g, w, tm), 0)
    m1 = jnp.max(sel, axis=1, keepdims=True)
    first = jnp.min(jnp.where(sel == m1, wi, w), axis=1, keepdims=True)
    m2 = jnp.max(jnp.where(wi == first, neg, sel), axis=1, keepdims=True)
    grp = m1 + m2
    gi1 = lax.broadcasted_iota(I32, (g, 1, tm), 0)
    chosen = jnp.zeros((g, 1, tm), F32)
    for _ in range(TOPK_GROUPS):
        m = jnp.max(grp, axis=0, keepdims=True)
        fg = jnp.min(jnp.where(grp == m, gi1, g), axis=0, keepdims=True)
        hit = gi1 == fg
        chosen = jnp.where(hit, 1.0, chosen)
        grp = jnp.where(hit, neg, grp)
    cand = jnp.where(chosen > 0.5, sel, neg)
    ei = gi * w + wi
    idx_rows, sc_rows = [], []
    for _ in range(TOP_K):
        m = jnp.max(jnp.max(cand, axis=0, keepdims=True), axis=1, keepdims=True)
        fe = jnp.min(jnp.min(jnp.where(cand == m, ei, e), axis=0, keepdims=True), axis=1, keepdims=True)
        hit = ei == fe
        sc = jnp.sum(jnp.sum(jnp.where(hit, scores3, 0.0), axis=0, keepdims=True), axis=1, keepdims=True)
        cand = jnp.where(hit, neg, cand)
        idx_rows.append(fe.reshape(1, tm))
        sc_rows.append(sc.reshape(1, tm))
    idx = jnp.concatenate(idx_rows, axis=0)
    sc = jnp.concatenate(sc_rows, axis=0)
    idx_ref[...] = idx
    gate_ref[...] = sc / jnp.sum(sc, axis=0, keepdims=True) * ROUTED_SCALE


def moe_route(o, x, gt1, np1, np2, sc2, sh2, w_router, router_bias):
    t, d = x.shape
    tm = ROUTE_TILE
    wr_t = w_router.T.astype(F32)
    rb = router_bias.reshape(N_EXPERTS, 1).astype(F32)

    def row(i):
        return (i, 0)

    def fixed(i):
        return (0, 0)

    vec = pl.BlockSpec((1, d), fixed)
    return pl.pallas_call(
        _route_kernel,
        grid=(t // tm,),
        in_specs=[pl.BlockSpec((tm, d), row), pl.BlockSpec((tm, d), row), vec, vec, vec, vec, vec,
                  pl.BlockSpec((N_EXPERTS, d), fixed), pl.BlockSpec((N_EXPERTS, 1), fixed)],
        out_specs=[pl.BlockSpec((tm, d), row), pl.BlockSpec((tm, d), row), pl.BlockSpec((tm, d // 2), row),
                   pl.BlockSpec((TOP_K, tm), lambda i: (0, i)), pl.BlockSpec((TOP_K, tm), lambda i: (0, i))],
        out_shape=[jax.ShapeDtypeStruct((t, d), F32), jax.ShapeDtypeStruct((t, d), BF16),
                   jax.ShapeDtypeStruct((t, d // 2), U32),
                   jax.ShapeDtypeStruct((TOP_K, t), I32), jax.ShapeDtypeStruct((TOP_K, t), F32)],
        compiler_params=_compiler_params(("parallel",)),
        name="moe_route",
    )(o, x, gt1, np1, np2, sc2, sh2, wr_t, rb)


def _plan_kernel(idx_ref, dest_ref, blk_e_ref, pad_end_ref, nused_ref, counts_ref, carry_ref, pstart_ref):
    p = pl.program_id(0)
    i = pl.program_id(1)
    e = N_EXPERTS
    tn = idx_ref.shape[1]
    idx = idx_ref[...]
    ei = lax.broadcasted_iota(I32, (e, tn), 0)
    member = jnp.zeros((e, tn), F32)
    for k in range(TOP_K):
        member = member + jnp.where(idx[k:k + 1, :] == ei, 1.0, 0.0)
    tile_counts = jnp.sum(member, axis=1, keepdims=True)

    @pl.when((p == 0) & (i == 0))
    def _():
        counts_ref[...] = jnp.zeros_like(counts_ref)

    @pl.when(p == 0)
    def _():
        counts_ref[...] += tile_counts

    @pl.when((p == 1) & (i == 0))
    def _():
        counts = counts_ref[...]
        padded = jnp.ceil(counts * (1.0 / MOE_BLOCK)) * MOE_BLOCK
        sub = lax.broadcasted_iota(I32, (e, e), 0)
        lan = lax.broadcasted_iota(I32, (e, e), 1)
        start_row = jnp.sum(jnp.where(sub < lan, padded, 0.0), axis=0, keepdims=True)
        end_row = jnp.sum(jnp.where(sub <= lan, padded, 0.0), axis=0, keepdims=True)
        pstart_ref[...] = jnp.sum(jnp.where(sub == lan, start_row, 0.0), axis=1, keepdims=True)
        carry_ref[...] = jnp.zeros_like(carry_ref)
        pad_end_ref[...] = end_row.astype(I32)
        nb = blk_e_ref.shape[0]
        bstart = (lax.broadcasted_iota(I32, (nb, e), 0) * MOE_BLOCK).astype(F32)
        below = jnp.sum(jnp.where(end_row <= bstart, 1.0, 0.0), axis=1, keepdims=True)
        blk_e_ref[...] = jnp.minimum(below, e - 1.0).astype(I32)
        nused_ref[...] = (end_row[:, e - 1:e] * (1.0 / MOE_BLOCK)).astype(I32)

    @pl.when(p == 1)
    def _():
        tt = lax.broadcasted_iota(I32, (tn, tn), 0)
        tc = lax.broadcasted_iota(I32, (tn, tn), 1)
        before = jnp.where(tt < tc, 1.0, 0.0).astype(BF16)
        prefix = jnp.dot(member.astype(BF16), before, preferred_element_type=F32)
        base = prefix + carry_ref[...] + pstart_ref[...]
        rows = []
        for k in range(TOP_K):
            rows.append(jnp.sum(jnp.where(idx[k:k + 1, :] == ei, base, 0.0), axis=0, keepdims=True))
        dest_ref[...] = jnp.concatenate(rows, axis=0).astype(I32)
        carry_ref[...] += tile_counts


def moe_plan(idx, nb):
    k, t = idx.shape
    tn = PLAN_TILE
    e = N_EXPERTS
    return pl.pallas_call(
        _plan_kernel,
        grid=(2, t // tn),
        in_specs=[pl.BlockSpec((k, tn), lambda p, i: (0, i))],
        out_specs=[pl.BlockSpec((k, tn), lambda p, i: (0, i * p)),
                   pl.BlockSpec((nb, 1), lambda p, i: (0, 0)),
                   pl.BlockSpec((1, e), lambda p, i: (0, 0)),
                   pl.BlockSpec((1, 1), lambda p, i: (0, 0))],
        out_shape=[jax.ShapeDtypeStruct((k, t), I32), jax.ShapeDtypeStruct((nb, 1), I32),
                   jax.ShapeDtypeStruct((1, e), I32), jax.ShapeDtypeStruct((1, 1), I32)],
        scratch_shapes=[pltpu.VMEM((e, 1), F32), pltpu.VMEM((e, 1), F32), pltpu.VMEM((e, 1), F32)],
        compiler_params=_compiler_params(("arbitrary", "arbitrary")),
        name="moe_plan",
    )(idx)


def _dispatch_kernel(dest_ref, pend_ref, h_ref, xs_hbm, zero_ref, sem):
    i = pl.program_id(0)
    tm = h_ref.shape[0]

    def block_clear(b):
        start = pl.multiple_of(b * MOE_BLOCK, MOE_BLOCK)
        return pltpu.make_async_copy(zero_ref, xs_hbm.at[pl.ds(start, MOE_BLOCK), :], sem.at[1])

    @pl.when(i == 0)
    def _():
        zero_ref[...] = jnp.zeros_like(zero_ref)
        shift = int(math.log2(MOE_BLOCK))
        n_blocks = xs_hbm.shape[0] // MOE_BLOCK
        first_unused = lax.shift_right_logical(pend_ref[N_EXPERTS - 1], shift)

        def last_block(e):
            return lax.shift_right_logical(jnp.maximum(pend_ref[e] - MOE_BLOCK, 0), shift)

        def zstart(e, c):
            block_clear(last_block(e)).start()
            return c
        lax.fori_loop(0, N_EXPERTS, zstart, 0)

        def tstart(b, c):
            block_clear(b).start()
            return c
        lax.fori_loop(first_unused, n_blocks, tstart, 0)

        def zwait(e, c):
            block_clear(0).wait()
            return c
        lax.fori_loop(0, N_EXPERTS, zwait, 0)
        lax.fori_loop(first_unused, n_blocks, zwait, 0)

    def row_copy(t, row):
        return pltpu.make_async_copy(h_ref.at[pl.ds(t, 1), :], xs_hbm.at[pl.ds(row, 1), :], sem.at[0])

    def start(t, c):
        for k in range(TOP_K):
            row_copy(t, dest_ref[t * TOP_K + k]).start(priority=k % DMA_PRIORITIES)
        return c
    lax.fori_loop(0, tm, start, 0)

    tile_rows = xs_hbm.at[pl.ds(0, TOP_K * tm), :]
    pltpu.make_async_copy(tile_rows, tile_rows, sem.at[0]).wait()


def moe_dispatch(h_packed, dest_tk, pad_end, rows):
    t, w = h_packed.shape
    tm = DISPATCH_TILE
    dflat = dest_tk.reshape(t * TOP_K)
    return pl.pallas_call(
        _dispatch_kernel,
        grid=(t // tm,),
        in_specs=[pl.BlockSpec((tm * TOP_K,), lambda i: (i,), memory_space=pltpu.SMEM),
                  pl.BlockSpec((N_EXPERTS,), lambda i: (0,), memory_space=pltpu.SMEM),
                  pl.BlockSpec((tm, w), lambda i: (i, 0))],
        out_specs=pl.BlockSpec(memory_space=pl.ANY),
        out_shape=jax.ShapeDtypeStruct((rows, w), U32),
        scratch_shapes=[pltpu.VMEM((MOE_BLOCK, w), U32), pltpu.SemaphoreType.DMA((2,))],
        compiler_params=_compiler_params(("arbitrary",)),
        name="moe_dispatch",
    )(dflat, pad_end.reshape(N_EXPERTS), h_packed)


def _expert_changed(blk_e_ref, i):
    prev = blk_e_ref[jnp.maximum(i - 1, 0)]
    return (i == 0) | (blk_e_ref[i] != prev)


def _gmm1_kernel(blk_e_ref, next_e_ref, nused_ref, x_ref, w1_hbm, w3_hbm, h_ref, st1_ref, st3_ref, w1b_ref, w3b_ref,
                 sem):
    i = pl.program_id(0)

    def weight_copies(e):
        return (pltpu.make_async_copy(w1_hbm.at[e], st1_ref, sem.at[0]),
                pltpu.make_async_copy(w3_hbm.at[e], st3_ref, sem.at[1]))

    @pl.when(i == 0)
    def _():
        for cp in weight_copies(blk_e_ref[0]):
            cp.start(priority=WEIGHT_STREAM_PRIORITY)

    @pl.when(i < nused_ref[0])
    def _():
        @pl.when(_expert_changed(blk_e_ref, i))
        def _():
            for cp in weight_copies(blk_e_ref[i]):
                cp.wait()
            w1b_ref[...] = st1_ref[...].astype(BF16)
            w3b_ref[...] = st3_ref[...].astype(BF16)

            @pl.when(next_e_ref[i] >= 0)
            def _():
                for cp in weight_copies(next_e_ref[i]):
                    cp.start(priority=WEIGHT_STREAM_PRIORITY)

        lo, hi = _unpack_bf16_pair(x_ref[...])
        lo = lo.astype(BF16)
        hi = hi.astype(BF16)
        half = lo.shape[1]
        g = (jnp.dot(lo, w1b_ref[:half, :], preferred_element_type=F32)
             + jnp.dot(hi, w1b_ref[half:, :], preferred_element_type=F32))
        u = (jnp.dot(lo, w3b_ref[:half, :], preferred_element_type=F32)
             + jnp.dot(hi, w3b_ref[half:, :], preferred_element_type=F32))
        h_ref[...] = (g * jax.nn.sigmoid(g) * u).astype(h_ref.dtype)

    @pl.when(i >= nused_ref[0])
    def _():
        h_ref[...] = jnp.zeros_like(h_ref)


def _expert_run_tables(blk_e, n_used):
    nb = blk_e.shape[0]
    pos = jnp.arange(nb, dtype=I32)
    prev = jnp.concatenate([blk_e[:1], blk_e[:-1]])
    starts = (blk_e != prev) & (pos < n_used[0]) & (pos > 0)
    start_pos = jnp.where(starts, pos, nb)
    shifted = jnp.concatenate([start_pos[1:], jnp.full((1,), nb, I32)])
    nxt = lax.cummin(shifted, axis=0, reverse=True)
    nxt2 = jnp.where(nxt < nb, nxt[jnp.minimum(nxt, nb - 1)], nb)

    def expert_at(p):
        return jnp.where(p < nb, blk_e[jnp.minimum(p, nb - 1)], -1).astype(I32)

    parity = (jnp.cumsum(starts.astype(I32)) % 2).astype(I32)
    return expert_at(nxt), expert_at(nxt2), parity


def grouped_gate_up(xs, blk_e, next_e, n_used, w1, w3):
    rows, half = xs.shape
    d = 2 * half
    nb = rows // MOE_BLOCK
    f = w1.shape[2]

    def row_blk(i, nu):
        return jnp.minimum(i, nu[0] - 1)

    grid_spec = pltpu.PrefetchScalarGridSpec(
        num_scalar_prefetch=3,
        grid=(nb,),
        in_specs=[pl.BlockSpec((MOE_BLOCK, half), lambda i, e, ne, nu: (row_blk(i, nu), 0)),
                  pl.BlockSpec(memory_space=pl.ANY),
                  pl.BlockSpec(memory_space=pl.ANY)],
        out_specs=pl.BlockSpec((MOE_BLOCK, f), lambda i, e, ne, nu: (i, 0)),
        scratch_shapes=[pltpu.VMEM((d, f), F32), pltpu.VMEM((d, f), F32),
                        pltpu.VMEM((d, f), BF16), pltpu.VMEM((d, f), BF16),
                        pltpu.SemaphoreType.DMA((2,))],
    )
    return pl.pallas_call(
        _gmm1_kernel,
        grid_spec=grid_spec,
        out_shape=jax.ShapeDtypeStruct((rows, f), BF16),
        compiler_params=_compiler_params(("arbitrary",)),
        name="moe_gate_up",
    )(blk_e, next_e, n_used, xs, w1, w3)


def _gmm2_kernel(blk_e_ref, next_e_ref, next2_e_ref, parity_ref, nused_ref, h_ref, w2_hbm, y_ref, st2_ref, w2b_ref,
                 sem):
    i = pl.program_id(0)

    def weight_copy(e, slot):
        return pltpu.make_async_copy(w2_hbm.at[e], st2_ref.at[slot], sem.at[slot])

    @pl.when(i == 0)
    def _():
        weight_copy(blk_e_ref[0], 0).start(priority=WEIGHT_STREAM_PRIORITY)

        @pl.when(next_e_ref[0] >= 0)
        def _():
            weight_copy(next_e_ref[0], 1).start(priority=WEIGHT_STREAM_PRIORITY)

    @pl.when(i < nused_ref[0])
    def _():
        @pl.when(_expert_changed(blk_e_ref, i))
        def _():
            slot = parity_ref[i]
            weight_copy(blk_e_ref[i], slot).wait()
            w2b_ref[...] = st2_ref[slot].astype(BF16)

            @pl.when(next2_e_ref[i] >= 0)
            def _():
                weight_copy(next2_e_ref[i], slot).start(priority=WEIGHT_STREAM_PRIORITY)

        y = jnp.dot(h_ref[...], w2b_ref[...], preferred_element_type=F32)
        half = y.shape[1] // 2
        y_ref[...] = _pack_bf16_pair(y[:, :half], y[:, half:])

    @pl.when(i >= nused_ref[0])
    def _():
        y_ref[...] = jnp.zeros_like(y_ref)


def grouped_down(hs, blk_e, next_e, next2_e, parity, n_used, w2):
    rows, f = hs.shape
    nb = rows // MOE_BLOCK
    d = w2.shape[2]

    def row_blk(i, nu):
        return jnp.minimum(i, nu[0] - 1)

    grid_spec = pltpu.PrefetchScalarGridSpec(
        num_scalar_prefetch=5,
        grid=(nb,),
        in_specs=[pl.BlockSpec((MOE_BLOCK, f), lambda i, e, n1, n2, par, nu: (row_blk(i, nu), 0)),
                  pl.BlockSpec(memory_space=pl.ANY)],
        out_specs=pl.BlockSpec((MOE_BLOCK, d // 2), lambda i, e, n1, n2, par, nu: (i, 0)),
        scratch_shapes=[pltpu.VMEM((2, f, d), F32), pltpu.VMEM((f, d), BF16), pltpu.SemaphoreType.DMA((2,))],
    )
    return pl.pallas_call(
        _gmm2_kernel,
        grid_spec=grid_spec,
        out_shape=jax.ShapeDtypeStruct((rows, d // 2), U32),
        compiler_params=_compiler_params(("arbitrary",)),
        name="moe_down",
    )(blk_e, next_e, next2_e, parity, n_used, hs, w2)


def _combine_kernel(dcur_ref, dnext_ref, y_hbm, gate_ref, shared_ref, x_ref, gt_ref, nw_ref, o_ref, gbuf, sem):
    i = pl.program_id(0)
    n = pl.num_programs(0)
    tm = x_ref.shape[0]
    slot = lax.rem(i, 2)

    def row_copy(s, src_row, dst_row):
        return pltpu.make_async_copy(y_hbm.at[pl.ds(src_row, 1), :], gbuf.at[s, pl.ds(dst_row, 1), :], sem.at[s])

    def start_tile(d_ref, s):
        def body(t, carry):
            for k in range(TOP_K):
                row_copy(s, d_ref[t * TOP_K + k], k * tm + t).start(priority=k % DMA_PRIORITIES)
            return carry
        lax.fori_loop(0, tm, body, 0)

    @pl.when(i == 0)
    def _():
        start_tile(dcur_ref, 0)

    @pl.when(i + 1 < n)
    def _():
        start_tile(dnext_ref, 1 - slot)

    pltpu.make_async_copy(y_hbm.at[pl.ds(0, TOP_K * tm), :], gbuf.at[slot], sem.at[slot]).wait()

    gate = gate_ref[...]
    half = gbuf.shape[2]
    acc_lo = shared_ref[:, :half].astype(F32)
    acc_hi = shared_ref[:, half:].astype(F32)
    for k in range(TOP_K):
        lo, hi = _unpack_bf16_pair(gbuf[slot, k * tm:(k + 1) * tm, :])
        gk = gate[:, k:k + 1]
        acc_lo = acc_lo + gk * lo
        acc_hi = acc_hi + gk * hi
    ms = (jnp.sum(acc_lo * acc_lo, axis=-1, keepdims=True) + jnp.sum(acc_hi * acc_hi, axis=-1, keepdims=True)) \
        * (1.0 / (2 * half))
    r = lax.rsqrt(ms + EPS)
    o_ref[:, :half] = x_ref[:, :half] + gt_ref[:, :half] * (acc_lo * r * nw_ref[:, :half])
    o_ref[:, half:] = x_ref[:, half:] + gt_ref[:, half:] * (acc_hi * r * nw_ref[:, half:])


def moe_combine(ys, dest_tk, gate_tk, shared, x, gt, nw):
    t, d = x.shape
    tm = COMBINE_TILE
    nt = t // tm
    dflat = dest_tk.reshape(t * TOP_K)
    return pl.pallas_call(
        _combine_kernel,
        grid=(nt,),
        in_specs=[pl.BlockSpec((tm * TOP_K,), lambda i: (i,), memory_space=pltpu.SMEM),
                  pl.BlockSpec((tm * TOP_K,), lambda i: (jnp.minimum(i + 1, nt - 1),), memory_space=pltpu.SMEM),
                  pl.BlockSpec(memory_space=pl.ANY),
                  pl.BlockSpec((tm, TOP_K), lambda i: (i, 0)),
                  pl.BlockSpec((tm, d), lambda i: (i, 0)),
                  pl.BlockSpec((tm, d), lambda i: (i, 0)),
                  pl.BlockSpec((1, d), lambda i: (0, 0)),
                  pl.BlockSpec((1, d), lambda i: (0, 0))],
        out_specs=pl.BlockSpec((tm, d), lambda i: (i, 0)),
        out_shape=jax.ShapeDtypeStruct((t, d), F32),
        scratch_shapes=[pltpu.VMEM((2, TOP_K * tm, d // 2), U32), pltpu.SemaphoreType.DMA((2,))],
        compiler_params=_compiler_params(("arbitrary",)),
        name="moe_combine",
    )(dflat, dflat, ys, gate_tk, shared, x, gt, nw)


def kernel(x, c, w_mod, b_mod, norm_pre1, norm_post1, w_in, conv_w, a_log, dt_bias, dn_norm_w, w_up_dn,
           w_up_swa, w_out, rel_bias, norm_pre2, norm_post2, w_router, router_bias, w1, w3, w2, ws1, ws3, ws2):
    B, S, D = x.shape
    assert B == 1
    T = B * S
    x = x.reshape(T, D)
    depth = w_mod.shape[0]
    n_swa_cols = 3 * SWA_W
    nb = -(-(T * TOP_K) // MOE_BLOCK) + N_EXPERTS
    for layer in range(depth):
        sc = jnp.broadcast_to(jax.nn.silu(c), (8, D))
        mod = dense_matmul(sc, w_mod[layer], tm=8, tn=1024, out_dtype=F32, name="adaln_mod")[:B] + b_mod[layer]
        sh1, sc1, gt1, sh2, sc2, gt2 = jnp.split(mod, 6, axis=-1)

        def vec(v):
            return v[layer].reshape(1, D)

        h = prenorm_modulate(x, vec(norm_pre1), sc1, sh1)
        w_t = jnp.swapaxes(w_in[layer], 0, 1)
        proj_dn = dense_matmul_nt(h, w_t, row_lo=0, row_hi=COL_BA, tm=1024, tn=512, out_dtype=F32,
                                  name="in_proj_dn")
        ba = dense_matmul_nt(h, w_t, row_lo=COL_BA, row_hi=COL_BA + LANES, tm=1024, tn=LANES, out_dtype=F32,
                             name="in_proj_ba")
        gates = dense_matmul_nt(h, w_t, row_lo=COL_GATES, row_hi=COL_END, tm=1024, tn=512, out_dtype=BF16,
                                name="in_proj_gates")
        o_dn = gated_deltanet(proj_dn, ba, conv_w[layer], a_log[layer], dt_bias[layer], dn_norm_w[layer])
        swa_os, swa_ls = [], []
        for gi, (_, dilation) in enumerate(SWA_GROUPS):
            qkv = project_residue_major(h, w_t, dilation=dilation, row_lo=COL_SWA + gi * n_swa_cols,
                                        row_hi=COL_SWA + (gi + 1) * n_swa_cols, tm=1024, tn=512,
                                        name=f"in_proj_swa_d{dilation}")
            heads = slice(gi * SWA_HEADS_PER_GROUP, (gi + 1) * SWA_HEADS_PER_GROUP)
            o_g, lse_g = swa_group(qkv, rel_bias[:, heads], dilation)
            swa_os.append(o_g)
            swa_ls.append(lse_g)
        mixed = mix_up(o_dn, swa_os, swa_ls, gates, w_up_dn[layer], w_up_swa[layer])
        o = dense_matmul(mixed, w_out[layer], tm=1024, tn=512, out_dtype=F32, name="out_proj")
        x1, h2b, h2p, idx, gate = moe_route(o, x, gt1, vec(norm_post1), vec(norm_pre2), sc2, sh2,
                                            w_router[layer], router_bias[layer])
        dest, blk_e, pad_end, n_used = moe_plan(idx, nb)
        dest_tk = dest.T
        gate_tk = gate.T
        xs = moe_dispatch(h2p, dest_tk, pad_end, nb * MOE_BLOCK)
        blk_e = blk_e.reshape(nb)
        n_used = n_used.reshape(1)
        next_e, next2_e, run_parity = _expert_run_tables(blk_e, n_used)
        hs = grouped_gate_up(xs, blk_e, next_e, n_used, w1[layer], w3[layer])
        ys = grouped_down(hs, blk_e, next_e, next2_e, run_parity, n_used, w2[layer])
        hsh = swiglu_up(h2b, ws1[layer], ws3[layer], tm=1024, tn=D_EXPERT // 3, name="shared_gate_up")
        shared = dense_matmul(hsh, ws2[layer], tm=1024, tn=512, out_dtype=F32, name="shared_down")
        x = moe_combine(ys, dest_tk, gate_tk, shared, x1, gt2, vec(norm_post2))
    return x.reshape(B, S, D)
```
